```python
import math
import jax, jax.numpy as jnp
from jax import lax
import numpy as np

D_MODEL = 1024
BATCH = 2
SEQ = 8192
DEPTH = 4
DEC_BATCH = 16
DEC_SEQ = 4096
PAST_LEN = 128

HEAD_DIM = 64
A_HEADS = 8
A_KV_HEADS = 2
A_GROUP = A_HEADS // A_KV_HEADS
B_HEADS = 8
A_WIDTH = A_HEADS * HEAD_DIM
KV_WIDTH = A_KV_HEADS * HEAD_DIM
B_WIDTH = B_HEADS * HEAD_DIM
MIX_WIDTH = A_WIDTH + B_WIDTH
IN_WIDTH = A_WIDTH + 2 * KV_WIDTH + 3 * B_WIDTH
IN_SPLITS = [A_WIDTH, A_WIDTH + KV_WIDTH, A_WIDTH + 2 * KV_WIDTH,
             A_WIDTH + 2 * KV_WIDTH + B_WIDTH, A_WIDTH + 2 * KV_WIDTH + 2 * B_WIDTH]
GRID_W = 64
QBLK = 128
WIN_R = 8
WIN_C = 16
ROPE_THETA = 10000.0
N_EXPERTS = 16
N_GROUPS = 4
EXPERTS_PER_GROUP = N_EXPERTS // N_GROUPS
TOP_K = 2
D_FF_EXPERT = 512
PLE_DIM = 256
LN_EPS = 1e-5
QK_EPS = 1e-6
NEG_INF = -1e30
DEEPNORM_ALPHA = (2 * DEPTH) ** 0.25
DEEPNORM_BETA = (8 * DEPTH) ** -0.25

kernel_name = "hymba_axial_gqa_natten_grouped_moe_encoder"


def _layer_norm(x, g, b):
    xf = x.astype(jnp.float32)
    mu = jnp.mean(xf, axis=-1, keepdims=True)
    var = jnp.mean(jnp.square(xf - mu), axis=-1, keepdims=True)
    y = (xf - mu) * lax.rsqrt(var + LN_EPS) * g.astype(jnp.float32) + b.astype(jnp.float32)
    return y.astype(x.dtype)


def _rms_norm(x, g, eps):
    xf = x.astype(jnp.float32)
    y = xf * lax.rsqrt(jnp.mean(jnp.square(xf), axis=-1, keepdims=True) + eps) * g.astype(jnp.float32)
    return y.astype(x.dtype)


def _axial_rope(n_tokens):
    t = jnp.arange(n_tokens, dtype=jnp.int32)
    row = (t // GRID_W).astype(jnp.float32)
    col = (t % GRID_W).astype(jnp.float32)
    n_freq = HEAD_DIM // 4
    inv_freq = ROPE_THETA ** (-jnp.arange(n_freq, dtype=jnp.float32) / n_freq)
    ang = jnp.concatenate([row[:, None] * inv_freq, col[:, None] * inv_freq], axis=-1)
    return jnp.cos(ang), jnp.sin(ang)


def _apply_axial_rope(x, cos, sin):
    nb, s_len, nh, _ = x.shape
    n_freq = HEAD_DIM // 4
    xh = x.astype(jnp.float32).reshape(nb, s_len, nh, 2, 2, n_freq)
    x1 = xh[..., 0, :]
    x2 = xh[..., 1, :]
    c = cos.reshape(s_len, 1, 2, n_freq)
    s = sin.reshape(s_len, 1, 2, n_freq)
    y = jnp.stack([x1 * c - x2 * s, x1 * s + x2 * c], axis=-2).reshape(x.shape)
    return y.astype(x.dtype)


def _global_gqa(q, k, v, q_gain, k_gain):
    nb, s_len = q.shape[0], q.shape[1]
    cos, sin = _axial_rope(s_len)
    q = _apply_axial_rope(_rms_norm(q, q_gain, QK_EPS), cos, sin) * (HEAD_DIM ** -0.5)
    k = _apply_axial_rope(_rms_norm(k, k_gain, QK_EPS), cos, sin)
    nblk = s_len // QBLK
    qb = q.reshape(nb, nblk, QBLK, A_KV_HEADS, A_GROUP, HEAD_DIM).transpose(1, 0, 2, 3, 4, 5)

    def block(qblk):
        s = jnp.einsum('bqkgd,bskd->bkgqs', qblk, k).astype(jnp.float32)
        pr = jax.nn.softmax(s, axis=-1).astype(v.dtype)
        return jnp.einsum('bkgqs,bskd->bqkgd', pr, v)

    o = lax.map(block, qb)
    return o.transpose(1, 0, 2, 3, 4, 5).reshape(nb, s_len, A_WIDTH)


def _neighborhood_attn(q, k, v, rpb):
    nb, s_len, nh, hd = q.shape
    rows = s_len // GRID_W
    wr = min(WIN_R, rows)
    rows_per_blk = QBLK // GRID_W
    br = min(wr + rows_per_blk - 1, rows)
    nblk = s_len // QBLK
    kg = k.reshape(nb, rows, GRID_W, nh, hd)
    vg = v.reshape(nb, rows, GRID_W, nh, hd)
    q = q * (HEAD_DIM ** -0.5)
    q_local = jnp.arange(QBLK, dtype=jnp.int32)
    qr_off = q_local // GRID_W
    qc = q_local % GRID_W
    k_local = jnp.arange(br * GRID_W, dtype=jnp.int32)
    kr_off = k_local // GRID_W
    kc = k_local % GRID_W
    cs = jnp.clip(qc - WIN_C // 2, 0, GRID_W - WIN_C)
    col_ok = (kc[None, :] >= cs[:, None]) & (kc[None, :] < cs[:, None] + WIN_C)
    ic = jnp.clip(kc[None, :] - qc[:, None] + WIN_C - 1, 0, 2 * WIN_C - 2)

    def block(j):
        r0 = j * rows_per_blk
        bs = jnp.clip(r0 - wr // 2, 0, rows - br)
        qblk = lax.dynamic_slice_in_dim(q, j * QBLK, QBLK, axis=1)
        kb = lax.dynamic_slice_in_dim(kg, bs, br, axis=1).reshape(nb, br * GRID_W, nh, hd)
        vb = lax.dynamic_slice_in_dim(vg, bs, br, axis=1).reshape(nb, br * GRID_W, nh, hd)
        qr = r0 + qr_off
        kr = bs + kr_off
        rs = jnp.clip(qr - wr // 2, 0, rows - wr)
        row_ok = (kr[None, :] >= rs[:, None]) & (kr[None, :] < rs[:, None] + wr)
        ir = jnp.clip(kr[None, :] - qr[:, None] + WIN_R - 1, 0, 2 * WIN_R - 2)
        bias = rpb[:, ir, ic].astype(jnp.float32)
        s = jnp.einsum('bqhd,bkhd->bhqk', qblk, kb).astype(jnp.float32) + bias
        s = jnp.where((row_ok & col_ok)[None, None], s, NEG_INF)
        pr = jax.nn.softmax(s, axis=-1).astype(vb.dtype)
        return jnp.einsum('bhqk,bkhd->bqhd', pr, vb)

    o = lax.map(block, jnp.arange(nblk, dtype=jnp.int32))
    return o.transpose(1, 0, 2, 3, 4).reshape(nb, s_len, B_WIDTH)


def _mixer(x, w_in, q_gain, k_gain, rpb, g_a, g_b, w_out):
    nb, s_len, _ = x.shape
    proj = x @ w_in
    qa, ka, va, qn, kn, vn = jnp.split(proj, IN_SPLITS, axis=-1)
    qa = qa.reshape(nb, s_len, A_HEADS, HEAD_DIM)
    ka = ka.reshape(nb, s_len, A_KV_HEADS, HEAD_DIM)
    va = va.reshape(nb, s_len, A_KV_HEADS, HEAD_DIM)
    qn = qn.reshape(nb, s_len, B_HEADS, HEAD_DIM)
    kn = kn.reshape(nb, s_len, B_HEADS, HEAD_DIM)
    vn = vn.reshape(nb, s_len, B_HEADS, HEAD_DIM)
    out_a = _rms_norm(_global_gqa(qa, ka, va, q_gain, k_gain), g_a, LN_EPS)
    out_b = _rms_norm(_neighborhood_attn(qn, kn, vn, rpb), g_b, LN_EPS)
    return jnp.concatenate([out_a, out_b], axis=-1) @ w_out


def _moe(h, w_router, b_router, w_gate, w_up, w_down):
    shp = h.shape
    hf = h.reshape(-1, D_MODEL)
    logits = (hf @ w_router).astype(jnp.float32) + b_router.astype(jnp.float32)
    probs = jax.nn.softmax(logits, axis=-1)
    grp = probs.reshape(-1, N_GROUPS, EXPERTS_PER_GROUP)
    grp_score = jnp.sum(lax.top_k(grp, TOP_K)[0], axis=-1)
    sel = jnp.argmax(grp_score, axis=-1)
    in_grp = (jnp.arange(N_EXPERTS) // EXPERTS_PER_GROUP)[None, :] == sel[:, None]
    top_w, top_i = lax.top_k(jnp.where(in_grp, probs, -1.0), TOP_K)
    top_w = top_w / jnp.sum(top_w, axis=-1, keepdims=True)
    gates = jnp.sum(jax.nn.one_hot(top_i, N_EXPERTS, dtype=jnp.float32) * top_w[..., None], axis=1).astype(h.dtype)
    out = jnp.zeros_like(hf)
    for e in range(N_EXPERTS):
        act = jax.nn.silu(hf @ w_gate[e]) * (hf @ w_up[e])
        out = out + gates[:, e:e + 1] * (act @ w_down[e])
    return out.reshape(shp)


def _trunk(x, p, w_in, q_norm_a, k_norm_a, rpb_b, g_out_a, g_out_b, w_out, ln_g, ln_b,
           w_router, b_router, w_gate, w_up, w_down, w_ple_gate, b_ple_gate, w_ple_proj):
    for i in range(DEPTH):
        mix = _mixer(x, w_in[i], q_norm_a[i], k_norm_a[i], rpb_b[i], g_out_a[i], g_out_b[i], w_out[i])
        h = _layer_norm(DEEPNORM_ALPHA * x + mix, ln_g[i, 0], ln_b[i, 0])
        ffn = _moe(h, w_router, b_router, w_gate[i], w_up[i], w_down[i])
        h = _layer_norm(DEEPNORM_ALPHA * h + ffn, ln_g[i, 1], ln_b[i, 1])
        pe = jax.nn.sigmoid(h @ w_ple_gate[i] + b_ple_gate[i]) * (p[i] @ w_ple_proj[i])
        x = _layer_norm(DEEPNORM_ALPHA * h + pe, ln_g[i, 2], ln_b[i, 2])
    return x


def setup_inputs(seed: int = 0) -> dict:
    key = jax.random.key(seed)
    ks = jax.random.split(key, 22)
    f32 = jnp.float32
    nrm = lambda k, shp: jax.random.normal(k, shp, f32)
    return {
        "x_prompt": nrm(ks[0], (BATCH, SEQ, D_MODEL)),
        "x_sample": nrm(ks[1], (DEC_BATCH, DEC_SEQ, D_MODEL)),
        "p_prompt": nrm(ks[2], (DEPTH, BATCH, SEQ, PLE_DIM)),
        "p_sample": nrm(ks[3], (DEPTH, DEC_BATCH, DEC_SEQ, PLE_DIM)),
        "w_in": nrm(ks[4], (DEPTH, D_MODEL, IN_WIDTH)) * D_MODEL ** -0.5,
        "q_norm_a": 1.0 + 0.02 * nrm(ks[5], (DEPTH, HEAD_DIM)),
        "k_norm_a": 1.0 + 0.02 * nrm(ks[6], (DEPTH, HEAD_DIM)),
        "rpb_b": 0.1 * nrm(ks[7], (DEPTH, B_HEADS, 2 * WIN_R - 1, 2 * WIN_C - 1)),
        "g_out_a": 1.0 + 0.02 * nrm(ks[8], (DEPTH, A_WIDTH)),
        "g_out_b": 1.0 + 0.02 * nrm(ks[9], (DEPTH, B_WIDTH)),
        "w_out": nrm(ks[10], (DEPTH, MIX_WIDTH, D_MODEL)) * (MIX_WIDTH ** -0.5 * DEEPNORM_BETA),
        "ln_g": 1.0 + 0.02 * nrm(ks[11], (DEPTH, 3, D_MODEL)),
        "ln_b": 0.02 * nrm(ks[12], (DEPTH, 3, D_MODEL)),
        "w_router": nrm(ks[13], (D_MODEL, N_EXPERTS)) * D_MODEL ** -0.5,
        "b_router": 0.01 * nrm(ks[14], (N_EXPERTS,)),
        "w_gate": nrm(ks[15], (DEPTH, N_EXPERTS, D_MODEL, D_FF_EXPERT)) * D_MODEL ** -0.5,
        "w_up": nrm(ks[16], (DEPTH, N_EXPERTS, D_MODEL, D_FF_EXPERT)) * D_MODEL ** -0.5,
        "w_down": nrm(ks[17], (DEPTH, N_EXPERTS, D_FF_EXPERT, D_MODEL)) * (D_FF_EXPERT ** -0.5 * DEEPNORM_BETA),
        "w_ple_gate": nrm(ks[18], (DEPTH, D_MODEL, D_MODEL)) * D_MODEL ** -0.5,
        "b_ple_gate": 0.01 * nrm(ks[19], (DEPTH, D_MODEL)),
        "w_ple_proj": nrm(ks[20], (DEPTH, PLE_DIM, D_MODEL)) * (PLE_DIM ** -0.5 * DEEPNORM_BETA),
    }


def reference(x_prompt, x_sample, p_prompt, p_sample, w_in, q_norm_a, k_norm_a, rpb_b, g_out_a, g_out_b,
              w_out, ln_g, ln_b, w_router, b_router, w_gate, w_up, w_down, w_ple_gate, b_ple_gate, w_ple_proj):
    y_prompt = _trunk(x_prompt, p_prompt, w_in, q_norm_a, k_norm_a, rpb_b, g_out_a, g_out_b, w_out, ln_g, ln_b,
                      w_router, b_router, w_gate, w_up, w_down, w_ple_gate, b_ple_gate, w_ple_proj)
    y_sample = _trunk(x_sample, p_sample, w_in, q_norm_a, k_norm_a, rpb_b, g_out_a, g_out_b, w_out, ln_g, ln_b,
                      w_router, b_router, w_gate, w_up, w_down, w_ple_gate, b_ple_gate, w_ple_proj)
    return (y_prompt, y_sample)
```

```python
import functools
import math

import jax
import jax.numpy as jnp
from jax import lax
from jax.experimental import pallas as pl
from jax.experimental.pallas import tpu as pltpu

F32 = jnp.float32
BF16 = jnp.bfloat16

D_MODEL = 1024
N_LAYERS = 4
HEAD_DIM = 64
A_HEADS = 8
A_KV_HEADS = 2
A_GROUP = A_HEADS // A_KV_HEADS
B_HEADS = 8
A_WIDTH = A_HEADS * HEAD_DIM
KV_WIDTH = A_KV_HEADS * HEAD_DIM
B_WIDTH = B_HEADS * HEAD_DIM
GRID_W = 64
QBLK = 128
WIN_R = 8
WIN_C = 16
ROPE_THETA = 10000.0
N_EXPERTS = 16
N_GROUPS = 4
EXPERTS_PER_GROUP = N_EXPERTS // N_GROUPS
D_FF_EXPERT = 512
PLE_DIM = 256
LN_EPS = 1e-5
QK_EPS = 1e-6
NEG_INF = -1e30
ALPHA = (2 * N_LAYERS) ** 0.25
Q_SCALE = HEAD_DIM ** -0.5

LANES = 128

TOKEN_TILE = 512
ATTN_Q_TILE = 256
ROW_TILE = 512
MOE_TILE = 1024
BAND_ROWS = 10
BAND_BLKS = BAND_ROWS * GRID_W // QBLK
N_BIAS_VARIANTS = 5
VMEM_LIMIT = 56 * 1024 * 1024

_QA0, _KA0, _VA0, _QN0, _VN0, _PT_ROWS = 0, 512, 640, 768, 1280, 1792


def _params(sem):
    return pltpu.CompilerParams(dimension_semantics=sem, vmem_limit_bytes=VMEM_LIMIT)


def _layer_norm(y, g, b):
    mu = jnp.mean(y, axis=-1, keepdims=True)
    d = y - mu
    var = jnp.mean(d * d, axis=-1, keepdims=True)
    return d * lax.rsqrt(var + LN_EPS) * g + b


def _rope_t(y, cos, sin):
    cr, cc = cos[0:16], cos[16:32]
    sr, sc = sin[0:16], sin[16:32]
    a1, a2, b1, b2 = y[0:16], y[16:32], y[32:48], y[48:64]
    return jnp.concatenate([a1 * cr - a2 * sr, a1 * sr + a2 * cr,
                            b1 * cc - b2 * sc, b1 * sc + b2 * cc], axis=0)


def _norm_rope_t(y, gain, cos, sin):
    ms = jnp.sum(y * y, axis=0, keepdims=True) * (1.0 / HEAD_DIM)
    return _rope_t(y * lax.rsqrt(ms + QK_EPS) * gain, cos, sin)


def _inproj_kernel(lidx_ref, x_ref, wt_ref, wkn_ref, cos_ref, sin_ref, qg_ref, kg_ref,
                   qz_ref, ka_ref, vat_ref, qnz_ref, kn_ref, vnt_ref):
    del lidx_ref
    xb = x_ref[0].astype(BF16)
    pt = lax.dot_general(wt_ref[0], xb, (((1,), (1,)), ((), ())),
                         preferred_element_type=F32)
    tm = xb.shape[0]
    cos = cos_ref[...]
    sin = sin_ref[...]
    qg = qg_ref[0] * Q_SCALE
    kg = kg_ref[0]
    zeros = jnp.zeros((HEAD_DIM, tm), BF16)
    for h in range(A_HEADS):
        q = _norm_rope_t(pt[_QA0 + h * HEAD_DIM:_QA0 + (h + 1) * HEAD_DIM], qg, cos, sin).astype(BF16)
        if h // A_GROUP == 0:
            qz_ref[0, h] = jnp.concatenate([q, zeros], axis=0)
        else:
            qz_ref[0, h] = jnp.concatenate([zeros, q], axis=0)
    kt = jnp.concatenate(
        [_norm_rope_t(pt[_KA0 + g * HEAD_DIM:_KA0 + (g + 1) * HEAD_DIM], kg, cos, sin)
         for g in range(A_KV_HEADS)], axis=0)
    ka_ref[0, 0] = kt.T.astype(BF16)
    vat_ref[0, 0] = pt[_VA0:_VA0 + KV_WIDTH].astype(BF16)
    for h in range(B_HEADS):
        q = (pt[_QN0 + h * HEAD_DIM:_QN0 + (h + 1) * HEAD_DIM] * Q_SCALE).astype(BF16)
        if h % 2 == 0:
            qnz_ref[0, h] = jnp.concatenate([q, zeros], axis=0)
        else:
            qnz_ref[0, h] = jnp.concatenate([zeros, q], axis=0)
    vnt_ref[0] = pt[_VN0:_VN0 + B_WIDTH].astype(BF16)
    kn = jnp.dot(xb, wkn_ref[0], preferred_element_type=F32).astype(BF16)
    kn_ref[0] = kn.reshape(tm // QBLK, QBLK, B_WIDTH)


def _inproj(lidx, x, wt, wkn, cos_t, sin_t, qg, kg):
    nb, s, _ = x.shape
    tm = TOKEN_TILE
    nc = s // tm
    out_shape = (
        jax.ShapeDtypeStruct((nb, A_HEADS, 2 * HEAD_DIM, s), BF16),
        jax.ShapeDtypeStruct((nb, nc, tm, KV_WIDTH), BF16),
        jax.ShapeDtypeStruct((nb, nc, KV_WIDTH, tm), BF16),
        jax.ShapeDtypeStruct((nb, B_HEADS, 2 * HEAD_DIM, s), BF16),
        jax.ShapeDtypeStruct((nb, s // QBLK, QBLK, B_WIDTH), BF16),
        jax.ShapeDtypeStruct((nb, B_WIDTH, s), BF16),
    )
    grid_spec = pltpu.PrefetchScalarGridSpec(
        num_scalar_prefetch=1,
        grid=(nb, nc),
        in_specs=[
            pl.BlockSpec((1, tm, D_MODEL), lambda b, c, l: (b, c, 0)),
            pl.BlockSpec((1, _PT_ROWS, D_MODEL), lambda b, c, l: (l[0], 0, 0)),
            pl.BlockSpec((1, D_MODEL, B_WIDTH), lambda b, c, l: (l[0], 0, 0)),
            pl.BlockSpec((HEAD_DIM // 2, tm), lambda b, c, l: (0, c)),
            pl.BlockSpec((HEAD_DIM // 2, tm), lambda b, c, l: (0, c)),
            pl.BlockSpec((1, HEAD_DIM, 1), lambda b, c, l: (l[0], 0, 0)),
            pl.BlockSpec((1, HEAD_DIM, 1), lambda b, c, l: (l[0], 0, 0)),
        ],
        out_specs=[
            pl.BlockSpec((1, A_HEADS, 2 * HEAD_DIM, tm), lambda b, c, l: (b, 0, 0, c)),
            pl.BlockSpec((1, 1, tm, KV_WIDTH), lambda b, c, l: (b, c, 0, 0)),
            pl.BlockSpec((1, 1, KV_WIDTH, tm), lambda b, c, l: (b, c, 0, 0)),
            pl.BlockSpec((1, B_HEADS, 2 * HEAD_DIM, tm), lambda b, c, l: (b, 0, 0, c)),
            pl.BlockSpec((1, tm // QBLK, QBLK, B_WIDTH), lambda b, c, l: (b, c, 0, 0)),
            pl.BlockSpec((1, B_WIDTH, tm), lambda b, c, l: (b, 0, c)),
        ],
    )
    return pl.pallas_call(
        _inproj_kernel, out_shape=out_shape, grid_spec=grid_spec,
        compiler_params=_params(("parallel", "parallel")), name="inproj",
    )(lidx, x, wt, wkn, cos_t, sin_t, qg, kg)


def _gattn_kernel(qz_ref, k_ref, vt_ref, o_ref, ot_ref):
    n_chunks = k_ref.shape[1]
    tq = qz_ref.shape[3]

    def head_body(h, carry):
        qh = qz_ref[0, h]
        g = h // A_GROUP

        def chunk_body(c, state):
            m, l, acc = state
            st = jnp.dot(k_ref[0, c], qh, preferred_element_type=F32)
            m_new = jnp.maximum(m, jnp.max(st, axis=0, keepdims=True))
            alpha = jnp.exp(m - m_new)
            p = jnp.exp(st - m_new)
            l = alpha * l + jnp.sum(p, axis=0, keepdims=True)
            vt = vt_ref[0, c, pl.ds(pl.multiple_of(g * HEAD_DIM, HEAD_DIM), HEAD_DIM), :]
            acc = alpha * acc + jnp.dot(vt, p.astype(BF16), preferred_element_type=F32)
            return m_new, l, acc

        init = (jnp.full((1, tq), -jnp.inf, F32), jnp.zeros((1, tq), F32), jnp.zeros((HEAD_DIM, tq), F32))
        _, l, acc = lax.fori_loop(0, n_chunks, chunk_body, init)
        ot_ref[h] = acc * (1.0 / l)
        return carry

    lax.fori_loop(0, A_HEADS, head_body, 0)
    o_ref[0] = ot_ref[...].reshape(A_WIDTH, tq).T


def _gattn(qz, ka, vat):
    nb, _, _, s = qz.shape
    nc, tk = ka.shape[1], ka.shape[2]
    tq = ATTN_Q_TILE
    return pl.pallas_call(
        _gattn_kernel,
        out_shape=jax.ShapeDtypeStruct((nb, s, A_WIDTH), F32),
        grid=(nb, s // tq),
        in_specs=[
            pl.BlockSpec((1, A_HEADS, 2 * HEAD_DIM, tq), lambda b, i: (b, 0, 0, i)),
            pl.BlockSpec((1, nc, tk, KV_WIDTH), lambda b, i: (b, 0, 0, 0)),
            pl.BlockSpec((1, nc, KV_WIDTH, tk), lambda b, i: (b, 0, 0, 0)),
        ],
        out_specs=pl.BlockSpec((1, tq, A_WIDTH), lambda b, i: (b, i, 0)),
        scratch_shapes=[pltpu.VMEM((A_HEADS, HEAD_DIM, tq), F32)],
        compiler_params=_params(("parallel", "parallel")), name="gattn",
    )(qz, ka, vat)


def _nattn_kernel(*refs):
    qz_ref = refs[0]
    k_refs = refs[1:1 + BAND_BLKS]
    v_refs = refs[1 + BAND_BLKS:1 + 2 * BAND_BLKS]
    bias_ref = refs[1 + 2 * BAND_BLKS]
    o_ref = refs[2 + 2 * BAND_BLKS]
    outs = []
    for h in range(B_HEADS):
        lo = (h // 2) * LANES
        kp = jnp.concatenate([k_refs[t][0, 0, :, lo:lo + LANES] for t in range(BAND_BLKS)], axis=0)
        st = jnp.dot(kp, qz_ref[0, h], preferred_element_type=F32) + bias_ref[0, h]
        m = jnp.max(st, axis=0, keepdims=True)
        p = jnp.exp(st - m)
        l = jnp.sum(p, axis=0, keepdims=True)
        vt = jnp.concatenate([v_refs[t][0, h * HEAD_DIM:(h + 1) * HEAD_DIM, :] for t in range(BAND_BLKS)], axis=1)
        acc = jnp.dot(vt, p.astype(BF16), preferred_element_type=F32)
        outs.append(acc * (1.0 / l))
    o_ref[0] = jnp.concatenate(outs, axis=0).T


def _band_start_blk(j, nblk):
    rows_per_blk = QBLK // GRID_W
    rows = nblk * rows_per_blk
    bs = jnp.clip(j * rows_per_blk - WIN_R // 2, 0, rows - BAND_ROWS)
    return bs // rows_per_blk


def _bias_variant(j, nblk):
    return jnp.where(j < 2, j, jnp.where(j >= nblk - 2, j - nblk + N_BIAS_VARIANTS, 2))


def _nattn(lidx, qnz, kn, vnt, bias_t):
    nb, _, _, s = qnz.shape
    nblk = s // QBLK
    k_specs = [pl.BlockSpec((1, 1, QBLK, B_WIDTH),
                            functools.partial(lambda b, j, l, t: (b, _band_start_blk(j, nblk) + t, 0, 0), t=t))
               for t in range(BAND_BLKS)]
    v_specs = [pl.BlockSpec((1, B_WIDTH, QBLK),
                            functools.partial(lambda b, j, l, t: (b, 0, _band_start_blk(j, nblk) + t), t=t))
               for t in range(BAND_BLKS)]
    grid_spec = pltpu.PrefetchScalarGridSpec(
        num_scalar_prefetch=1,
        grid=(nb, nblk),
        in_specs=[pl.BlockSpec((1, B_HEADS, 2 * HEAD_DIM, QBLK), lambda b, j, l: (b, 0, 0, j))]
        + k_specs + v_specs
        + [pl.BlockSpec((1, B_HEADS, BAND_ROWS * GRID_W, QBLK),
                        lambda b, j, l: (l[0] * N_BIAS_VARIANTS + _bias_variant(j, nblk), 0, 0, 0))],
        out_specs=pl.BlockSpec((1, QBLK, B_WIDTH), lambda b, j, l: (b, j, 0)),
    )

    def kernel(lidx_ref, *refs):
        del lidx_ref
        _nattn_kernel(*refs)

    return pl.pallas_call(
        kernel, out_shape=jax.ShapeDtypeStruct((nb, s, B_WIDTH), F32), grid_spec=grid_spec,
        compiler_params=_params(("parallel", "arbitrary")), name="nattn",
    )(lidx, qnz, *([kn] * BAND_BLKS), *([vnt] * BAND_BLKS), bias_t)


def _nattn_bias_tables(rpb):
    n_layers = rpb.shape[0]
    rows_per_blk = QBLK // GRID_W
    ql = jnp.arange(QBLK, dtype=jnp.int32)
    kl = jnp.arange(BAND_ROWS * GRID_W, dtype=jnp.int32)
    qr_off, qc = ql // GRID_W, ql % GRID_W
    kr_off, kc = kl // GRID_W, kl % GRID_W
    cs = jnp.clip(qc - WIN_C // 2, 0, GRID_W - WIN_C)
    col_ok = (kc[:, None] >= cs[None, :]) & (kc[:, None] < cs[None, :] + WIN_C)
    ic = jnp.clip(kc[:, None] - qc[None, :] + WIN_C - 1, 0, 2 * WIN_C - 2)
    rows = 16
    tables = []
    for r0 in (0, 2, 4, rows - 4, rows - 2):
        bs = min(max(r0 - WIN_R // 2, 0), rows - BAND_ROWS)
        qr = r0 + qr_off
        kr = bs + kr_off
        rs = jnp.clip(qr - WIN_R // 2, 0, rows - WIN_R)
        row_ok = (kr[:, None] >= rs[None, :]) & (kr[:, None] < rs[None, :] + WIN_R)
        ir = jnp.clip(kr[:, None] - qr[None, :] + WIN_R - 1, 0, 2 * WIN_R - 2)
        bias = rpb[:, :, ir, ic].astype(F32)
        tables.append(jnp.where((row_ok & col_ok)[None, None], bias, NEG_INF))
    t = jnp.stack(tables, axis=1)
    return t.reshape(n_layers * N_BIAS_VARIANTS, B_HEADS, BAND_ROWS * GRID_W, QBLK)


def _group_member(x, d, wrapped):
    return jnp.where(wrapped, pltpu.roll(x, EXPERTS_PER_GROUP - d, 0), pltpu.roll(x, N_EXPERTS - d, 0))


def _route_t(logits_t):
    mx = jnp.max(logits_t, axis=0, keepdims=True)
    ex = jnp.exp(logits_t - mx)
    probs = ex / jnp.sum(ex, axis=0, keepdims=True)
    e_idx = lax.broadcasted_iota(jnp.int32, probs.shape, 0)
    pos = e_idx % EXPERTS_PER_GROUP
    rank = jnp.zeros(probs.shape, jnp.int32)
    for d in range(1, EXPERTS_PER_GROUP):
        wrapped = pos + d >= EXPERTS_PER_GROUP
        other = _group_member(probs, d, wrapped)
        beats = (other > probs) | ((other == probs) & wrapped)
        rank = rank + beats.astype(jnp.int32)
    top = rank < 2
    kept = jnp.where(top, probs, 0.0)
    score = kept
    for d in range(1, EXPERTS_PER_GROUP):
        score = score + _group_member(kept, d, pos + d >= EXPERTS_PER_GROUP)
    grp = e_idx // EXPERTS_PER_GROUP
    lost = jnp.zeros(probs.shape, jnp.bool_)
    for d in range(1, N_GROUPS):
        other = pltpu.roll(score, N_EXPERTS - d * EXPERTS_PER_GROUP, 0)
        earlier = grp + d >= N_GROUPS
        lost = lost | (other > score) | ((other == score) & earlier)
    return jnp.where(top & jnp.logical_not(lost), probs / score, 0.0)


def _post_attn_kernel(lidx_ref, oa_ref, ob_ref, x_ref, wo_ref, ga_ref, gb_ref, lng_ref, lnb_ref,
                      wrh_ref, wrl_ref, br_ref, h_ref, hb_ref, gates_ref):
    del lidx_ref
    a = oa_ref[...]
    b = ob_ref[...]
    na = a * lax.rsqrt(jnp.mean(a * a, axis=-1, keepdims=True) + LN_EPS) * ga_ref[0]
    nb_ = b * lax.rsqrt(jnp.mean(b * b, axis=-1, keepdims=True) + LN_EPS) * gb_ref[0]
    cat = jnp.concatenate([na, nb_], axis=-1).astype(BF16)
    mix = jnp.dot(cat, wo_ref[0], preferred_element_type=F32)
    h = _layer_norm(ALPHA * x_ref[...] + mix, lng_ref[0, 0:1], lnb_ref[0, 0:1])
    h_ref[...] = h
    h_hi = h.astype(BF16)
    hb_ref[...] = h_hi
    h_lo = (h - h_hi.astype(F32)).astype(BF16)
    dn = (((1,), (1,)), ((), ()))
    logits_t = (lax.dot_general(wrh_ref[...], h_hi, dn, preferred_element_type=F32)
                + lax.dot_general(wrh_ref[...], h_lo, dn, preferred_element_type=F32)
                + lax.dot_general(wrl_ref[...], h_hi, dn, preferred_element_type=F32)) + br_ref[...]
    gates_t = _route_t(logits_t)
    tm = h.shape[0]
    padded = jnp.concatenate([gates_t, jnp.zeros((LANES - N_EXPERTS, tm), F32)], axis=0)
    gates_ref[...] = padded.T


def _post_attn(lidx, oa, ob, x, wo, ga, gb, lng, lnb, wrh, wrl, br):
    t = x.shape[0]
    tm = ROW_TILE
    row = lambda i, l: (i, 0)
    lay = lambda i, l: (l[0], 0, 0)
    full = lambda i, l: (0, 0)
    grid_spec = pltpu.PrefetchScalarGridSpec(
        num_scalar_prefetch=1,
        grid=(t // tm,),
        in_specs=[
            pl.BlockSpec((tm, A_WIDTH), row),
            pl.BlockSpec((tm, B_WIDTH), row),
            pl.BlockSpec((tm, D_MODEL), row),
            pl.BlockSpec((1, D_MODEL, D_MODEL), lay),
            pl.BlockSpec((1, 1, A_WIDTH), lay),
            pl.BlockSpec((1, 1, B_WIDTH), lay),
            pl.BlockSpec((1, 3, D_MODEL), lay),
            pl.BlockSpec((1, 3, D_MODEL), lay),
            pl.BlockSpec((N_EXPERTS, D_MODEL), full),
            pl.BlockSpec((N_EXPERTS, D_MODEL), full),
            pl.BlockSpec((N_EXPERTS, 1), full),
        ],
        out_specs=[
            pl.BlockSpec((tm, D_MODEL), row),
            pl.BlockSpec((tm, D_MODEL), row),
            pl.BlockSpec((tm, LANES), row),
        ],
    )
    return pl.pallas_call(
        _post_attn_kernel,
        out_shape=(jax.ShapeDtypeStruct((t, D_MODEL), F32),
                   jax.ShapeDtypeStruct((t, D_MODEL), BF16),
                   jax.ShapeDtypeStruct((t, LANES), F32)),
        grid_spec=grid_spec,
        compiler_params=_params(("parallel",)), name="post_attn",
    )(lidx, oa, ob, x, wo, ga, gb, lng, lnb, wrh, wrl, br)


def _moe_kernel(lidx_ref, hb_ref, gates_ref, wg_ref, wu_ref, wd_ref, o_ref):
    del lidx_ref
    e = pl.program_id(1)
    hb = hb_ref[...]
    gate = jnp.dot(hb, wg_ref[0, 0], preferred_element_type=F32)
    up = jnp.dot(hb, wu_ref[0, 0], preferred_element_type=F32)
    act = (gate * jax.nn.sigmoid(gate) * up).astype(BF16)
    y = jnp.dot(act, wd_ref[0, 0], preferred_element_type=F32)
    lane = lax.broadcasted_iota(jnp.int32, gates_ref.shape, 1)
    g_col = jnp.sum(jnp.where(lane == e, gates_ref[...], 0.0), axis=1, keepdims=True)
    contrib = g_col * y

    @pl.when(e == 0)
    def _():
        o_ref[...] = contrib

    @pl.when(e != 0)
    def _():
        o_ref[...] += contrib


def _moe(lidx, hb, gates, wg, wu, wd):
    t = hb.shape[0]
    tm = min(MOE_TILE, t)
    grid_spec = pltpu.PrefetchScalarGridSpec(
        num_scalar_prefetch=1,
        grid=(t // tm, N_EXPERTS),
        in_specs=[
            pl.BlockSpec((tm, D_MODEL), lambda i, e, l: (i, 0)),
            pl.BlockSpec((tm, LANES), lambda i, e, l: (i, 0)),
            pl.BlockSpec((1, 1, D_MODEL, D_FF_EXPERT), lambda i, e, l: (l[0], e, 0, 0)),
            pl.BlockSpec((1, 1, D_MODEL, D_FF_EXPERT), lambda i, e, l: (l[0], e, 0, 0)),
            pl.BlockSpec((1, 1, D_FF_EXPERT, D_MODEL), lambda i, e, l: (l[0], e, 0, 0)),
        ],
        out_specs=pl.BlockSpec((tm, D_MODEL), lambda i, e, l: (i, 0)),
    )
    return pl.pallas_call(
        _moe_kernel, out_shape=jax.ShapeDtypeStruct((t, D_MODEL), F32), grid_spec=grid_spec,
        compiler_params=_params(("parallel", "arbitrary")), name="moe",
    )(lidx, hb, gates, wg, wu, wd)


def _post_moe_kernel(lidx_ref, h_ref, f_ref, p_ref, wpg_ref, bpg_ref, wpp_ref, lng_ref, lnb_ref, x_ref):
    del lidx_ref
    h2 = _layer_norm(ALPHA * h_ref[...] + f_ref[...], lng_ref[0, 1:2], lnb_ref[0, 1:2])
    z = jnp.dot(h2.astype(BF16), wpg_ref[0], preferred_element_type=F32) + bpg_ref[0]
    proj = jnp.dot(p_ref[0].astype(BF16), wpp_ref[0], preferred_element_type=F32)
    pe = jax.nn.sigmoid(z) * proj
    x_ref[...] = _layer_norm(ALPHA * h2 + pe, lng_ref[0, 2:3], lnb_ref[0, 2:3])


def _post_moe(lidx, h, ffn, p, wpg, bpg, wpp, lng, lnb):
    t = h.shape[0]
    tm = ROW_TILE
    row = lambda i, l: (i, 0)
    lay = lambda i, l: (l[0], 0, 0)
    grid_spec = pltpu.PrefetchScalarGridSpec(
        num_scalar_prefetch=1,
        grid=(t // tm,),
        in_specs=[
            pl.BlockSpec((tm, D_MODEL), row),
            pl.BlockSpec((tm, D_MODEL), row),
            pl.BlockSpec((1, tm, PLE_DIM), lambda i, l: (l[0], i, 0)),
            pl.BlockSpec((1, D_MODEL, D_MODEL), lay),
            pl.BlockSpec((1, 1, D_MODEL), lay),
            pl.BlockSpec((1, PLE_DIM, D_MODEL), lay),
            pl.BlockSpec((1, 3, D_MODEL), lay),
            pl.BlockSpec((1, 3, D_MODEL), lay),
        ],
        out_specs=pl.BlockSpec((tm, D_MODEL), row),
    )
    return pl.pallas_call(
        _post_moe_kernel, out_shape=jax.ShapeDtypeStruct((t, D_MODEL), F32), grid_spec=grid_spec,
        compiler_params=_params(("parallel",)), name="post_moe",
    )(lidx, h, ffn, p, wpg, bpg, wpp, lng, lnb)


def _rope_tables_t(s):
    t = jnp.arange(s, dtype=jnp.int32)
    row = (t // GRID_W).astype(F32)
    col = (t % GRID_W).astype(F32)
    n_freq = HEAD_DIM // 4
    inv_freq = ROPE_THETA ** (-jnp.arange(n_freq, dtype=F32) / n_freq)
    ang = jnp.concatenate([inv_freq[:, None] * row[None, :], inv_freq[:, None] * col[None, :]], axis=0)
    return jnp.cos(ang), jnp.sin(ang)


def _prepare_weights(w_in, q_norm_a, k_norm_a, rpb_b, g_out_a, g_out_b, w_out, w_router, b_router,
                     w_gate, w_up, w_down, w_ple_gate, b_ple_gate, w_ple_proj):
    n_layers = w_in.shape[0]
    qa, ka, va, qn, kn, vn = jnp.split(w_in, [512, 640, 768, 1280, 1792], axis=-1)
    wt = jnp.swapaxes(jnp.concatenate([qa, ka, va, qn, vn], axis=-1), 1, 2).astype(BF16)
    wr_t = w_router.T
    wr_hi = wr_t.astype(BF16)
    wr_lo = (wr_t - wr_hi.astype(F32)).astype(BF16)
    return dict(
        wt=wt, wkn=kn.astype(BF16),
        qg=q_norm_a.reshape(n_layers, HEAD_DIM, 1), kg=k_norm_a.reshape(n_layers, HEAD_DIM, 1),
        bias_t=_nattn_bias_tables(rpb_b),
        ga=g_out_a.reshape(n_layers, 1, A_WIDTH), gb=g_out_b.reshape(n_layers, 1, B_WIDTH),
        wo=w_out.astype(BF16), wr_hi=wr_hi, wr_lo=wr_lo, br=b_router.reshape(N_EXPERTS, 1),
        wg=w_gate.astype(BF16), wu=w_up.astype(BF16), wd=w_down.astype(BF16),
        wpg=w_ple_gate.astype(BF16), bpg=b_ple_gate.reshape(n_layers, 1, D_MODEL), wpp=w_ple_proj.astype(BF16),
    )


def _layer(lidx, x, p_flat, w, ln_g, ln_b, rope):
    nb, s, _ = x.shape
    qz, ka, vat, qnz, kn, vnt = _inproj(lidx, x, w["wt"], w["wkn"], rope[0], rope[1], w["qg"], w["kg"])
    oa = _gattn(qz, ka, vat)
    ob = _nattn(lidx, qnz, kn, vnt, w["bias_t"])
    t = nb * s
    h, hb, gates = _post_attn(lidx, oa.reshape(t, A_WIDTH), ob.reshape(t, B_WIDTH), x.reshape(t, D_MODEL),
                              w["wo"], w["ga"], w["gb"], ln_g, ln_b, w["wr_hi"], w["wr_lo"], w["br"])
    ffn = _moe(lidx, hb, gates, w["wg"], w["wu"], w["wd"])
    xo = _post_moe(lidx, h, ffn, p_flat, w["wpg"], w["bpg"], w["wpp"], ln_g, ln_b)
    return xo.reshape(nb, s, D_MODEL)


def kernel(x_prompt, x_sample, p_prompt, p_sample, w_in, q_norm_a, k_norm_a, rpb_b, g_out_a, g_out_b, w_out, ln_g, ln_b, w_router, b_router, w_gate, w_up, w_down, w_ple_gate, b_ple_gate, w_ple_proj):
    n_layers = w_in.shape[0]
    w = _prepare_weights(w_in, q_norm_a, k_norm_a, rpb_b, g_out_a, g_out_b, w_out, w_router, b_router,
                         w_gate, w_up, w_down, w_ple_gate, b_ple_gate, w_ple_proj)
    trunks = []
    for x, p in ((x_prompt, p_prompt), (x_sample, p_sample)):
        nb, s, _ = x.shape
        trunks.append((x, p.reshape(n_layers, nb * s, PLE_DIM), _rope_tables_t(s)))

    def body(xs, layer):
        lidx = jnp.reshape(layer, (1,)).astype(jnp.int32)
        new = tuple(_layer(lidx, x, trunks[i][1], w, ln_g, ln_b, trunks[i][2]) for i, x in enumerate(xs))
        return new, None

    (y_prompt, y_sample), _ = lax.scan(body, (trunks[0][0], trunks[1][0]), jnp.arange(n_layers, dtype=jnp.int32))
    return (y_prompt, y_sample)
```

```python
import functools
import math

import jax
import jax.numpy as jnp
import numpy as np
from jax import lax
from jax.experimental import pallas as pl
from jax.experimental.pallas import tpu as pltpu

F32 = jnp.float32
BF16 = jnp.bfloat16

D_MODEL = 1024
N_LAYERS = 4
HEAD_DIM = 64
A_HEADS = 8
A_KV_HEADS = 2
A_GROUP = A_HEADS // A_KV_HEADS
B_HEADS = 8
A_WIDTH = A_HEADS * HEAD_DIM
KV_WIDTH = A_KV_HEADS * HEAD_DIM
B_WIDTH = B_HEADS * HEAD_DIM
GRID_W = 64
QBLK = 128
WIN_R = 8
WIN_C = 16
ROPE_THETA = 10000.0
N_EXPERTS = 16
N_GROUPS = 4
EXPERTS_PER_GROUP = N_EXPERTS // N_GROUPS
D_FF_EXPERT = 512
PLE_DIM = 256
LN_EPS = 1e-5
QK_EPS = 1e-6
NEG_INF = -1e30
ALPHA = (2 * N_LAYERS) ** 0.25
Q_SCALE = HEAD_DIM ** -0.5
LOG2_E = math.log2(math.e)

LANES = 128

TOKEN_TILE = 512
ATTN_Q_TILE = 512
ROW_TILE = 512
MOE_TILE = 1024
MOE_SUB = 128
BAND_ROWS = 10
BAND_BLKS = BAND_ROWS * GRID_W // QBLK
N_BIAS_VARIANTS = 5
VMEM_LIMIT = 56 * 1024 * 1024

_QA0, _KA0, _VA0, _PT_ROWS = 0, 512, 640, 768
_QN0, _KN0, _VN0, _NAT_COLS = 0, 512, 1024, 1536
ONES_ROWS = 16
VT_ROWS = HEAD_DIM + ONES_ROWS


def _params(sem):
    return pltpu.CompilerParams(dimension_semantics=sem, vmem_limit_bytes=VMEM_LIMIT)


def _layer_norm(y, g, b):
    mu = jnp.mean(y, axis=-1, keepdims=True)
    d = y - mu
    var = jnp.mean(d * d, axis=-1, keepdims=True)
    return d * lax.rsqrt(var + LN_EPS) * g + b


def _rope_t(y, cos, sin):
    cr, cc = cos[0:16], cos[16:32]
    sr, sc = sin[0:16], sin[16:32]
    a1, a2, b1, b2 = y[0:16], y[16:32], y[32:48], y[48:64]
    return jnp.concatenate([a1 * cr - a2 * sr, a1 * sr + a2 * cr,
                            b1 * cc - b2 * sc, b1 * sc + b2 * cc], axis=0)


def _norm_rope_t(y, gain, cos, sin):
    ms = jnp.sum(y * y, axis=0, keepdims=True) * (1.0 / HEAD_DIM)
    return _rope_t(y * lax.rsqrt(ms + QK_EPS) * gain, cos, sin)


def _inproj_kernel(lidx_ref, x_ref, wt_ref, wn_ref, cos_ref, sin_ref, qg_ref, kg_ref,
                   qz_ref, ka_ref, vat_ref, qn_ref, kn_ref, vn_ref):
    del lidx_ref
    xb = x_ref[0].astype(BF16)
    pt = lax.dot_general(wt_ref[0], xb, (((1,), (1,)), ((), ())),
                         preferred_element_type=F32)
    tm = xb.shape[0]
    cos = cos_ref[...]
    sin = sin_ref[...]
    qg = qg_ref[0] * (Q_SCALE * LOG2_E)
    kg = kg_ref[0]
    zeros = jnp.zeros((HEAD_DIM, tm), BF16)
    for h in range(A_HEADS):
        q = _norm_rope_t(pt[_QA0 + h * HEAD_DIM:_QA0 + (h + 1) * HEAD_DIM], qg, cos, sin).astype(BF16)
        if h // A_GROUP == 0:
            qz_ref[0, h] = jnp.concatenate([q, zeros], axis=0)
        else:
            qz_ref[0, h] = jnp.concatenate([zeros, q], axis=0)
    kt = jnp.concatenate(
        [_norm_rope_t(pt[_KA0 + g * HEAD_DIM:_KA0 + (g + 1) * HEAD_DIM], kg, cos, sin)
         for g in range(A_KV_HEADS)], axis=0)
    ka_ref[0, 0] = kt.T.astype(BF16)
    ones = jnp.ones((ONES_ROWS, tm), BF16)
    for g in range(A_KV_HEADS):
        v = pt[_VA0 + g * HEAD_DIM:_VA0 + (g + 1) * HEAD_DIM].astype(BF16)
        vat_ref[0, 0, g] = jnp.concatenate([v, ones], axis=0)
    nat = jnp.dot(xb, wn_ref[0], preferred_element_type=F32)
    qn_ref[0] = (nat[:, _QN0:_QN0 + B_WIDTH] * Q_SCALE).astype(BF16)
    kn_ref[0] = nat[:, _KN0:_KN0 + B_WIDTH].astype(BF16).reshape(tm // QBLK, QBLK, B_WIDTH)
    vn_ref[0] = nat[:, _VN0:_VN0 + B_WIDTH].astype(BF16).reshape(tm // QBLK, QBLK, B_WIDTH)


def _inproj(lidx, x, wt, wn, cos_t, sin_t, qg, kg):
    nb, s, _ = x.shape
    tm = TOKEN_TILE
    nc = s // tm
    out_shape = (
        jax.ShapeDtypeStruct((nb, A_HEADS, 2 * HEAD_DIM, s), BF16),
        jax.ShapeDtypeStruct((nb, nc, tm, KV_WIDTH), BF16),
        jax.ShapeDtypeStruct((nb, nc, A_KV_HEADS, VT_ROWS, tm), BF16),
        jax.ShapeDtypeStruct((nb, s, B_WIDTH), BF16),
        jax.ShapeDtypeStruct((nb, s // QBLK, QBLK, B_WIDTH), BF16),
        jax.ShapeDtypeStruct((nb, s // QBLK, QBLK, B_WIDTH), BF16),
    )
    grid_spec = pltpu.PrefetchScalarGridSpec(
        num_scalar_prefetch=1,
        grid=(nb, nc),
        in_specs=[
            pl.BlockSpec((1, tm, D_MODEL), lambda b, c, l: (b, c, 0)),
            pl.BlockSpec((1, _PT_ROWS, D_MODEL), lambda b, c, l: (l[0], 0, 0)),
            pl.BlockSpec((1, D_MODEL, _NAT_COLS), lambda b, c, l: (l[0], 0, 0)),
            pl.BlockSpec((HEAD_DIM // 2, tm), lambda b, c, l: (0, c)),
            pl.BlockSpec((HEAD_DIM // 2, tm), lambda b, c, l: (0, c)),
            pl.BlockSpec((1, HEAD_DIM, 1), lambda b, c, l: (l[0], 0, 0)),
            pl.BlockSpec((1, HEAD_DIM, 1), lambda b, c, l: (l[0], 0, 0)),
        ],
        out_specs=[
            pl.BlockSpec((1, A_HEADS, 2 * HEAD_DIM, tm), lambda b, c, l: (b, 0, 0, c)),
            pl.BlockSpec((1, 1, tm, KV_WIDTH), lambda b, c, l: (b, c, 0, 0)),
            pl.BlockSpec((1, 1, A_KV_HEADS, VT_ROWS, tm), lambda b, c, l: (b, c, 0, 0, 0)),
            pl.BlockSpec((1, tm, B_WIDTH), lambda b, c, l: (b, c, 0)),
            pl.BlockSpec((1, tm // QBLK, QBLK, B_WIDTH), lambda b, c, l: (b, c, 0, 0)),
            pl.BlockSpec((1, tm // QBLK, QBLK, B_WIDTH), lambda b, c, l: (b, c, 0, 0)),
        ],
    )
    return pl.pallas_call(
        _inproj_kernel, out_shape=out_shape, grid_spec=grid_spec,
        compiler_params=_params(("parallel", "parallel")), name="inproj",
    )(lidx, x, wt, wn, cos_t, sin_t, qg, kg)


def _col_max(st):
    slabs = [st[r:r + 128] for r in range(0, st.shape[0], 128)]
    while len(slabs) > 1:
        slabs = [jnp.maximum(slabs[i], slabs[i + 1]) for i in range(0, len(slabs), 2)]
    return jnp.max(slabs[0], axis=0, keepdims=True)


def _gattn_kernel(qz_ref, k_ref, vt_ref, o_ref, *scratch):
    m_refs, acc_refs = scratch[:A_HEADS], scratch[A_HEADS:]
    n_chunks = k_ref.shape[1]
    for h in range(A_HEADS):
        m_refs[h][...] = jnp.full(m_refs[h].shape, -jnp.inf, F32)
        acc_refs[h][...] = jnp.zeros(acc_refs[h].shape, F32)

    def chunk_body(c, carry):
        kc = k_ref[0, c]
        scores = lambda h: jnp.dot(kc, qz_ref[0, h], preferred_element_type=F32)
        st_next = scores(0)
        for h in range(A_HEADS):
            st = st_next
            if h + 1 < A_HEADS:
                st_next = scores(h + 1)
            m_old = m_refs[h][...]
            m_new = jnp.maximum(m_old, _col_max(st))
            alpha = jnp.exp2(m_old - m_new)
            p = jnp.exp2(st - m_new).astype(BF16)
            pv = jnp.dot(vt_ref[0, c, h // A_GROUP], p, preferred_element_type=F32)
            acc_refs[h][...] = alpha * acc_refs[h][...] + pv
            m_refs[h][...] = m_new
        return carry

    lax.fori_loop(0, n_chunks, chunk_body, 0)
    outs = []
    for h in range(A_HEADS):
        acc = acc_refs[h][...]
        outs.append(acc[0:HEAD_DIM] * (1.0 / acc[HEAD_DIM:HEAD_DIM + 1]))
    o_ref[0] = jnp.concatenate(outs, axis=0).T


def _gattn(qz, ka, vat):
    nb, _, _, s = qz.shape
    nc, tk = ka.shape[1], ka.shape[2]
    tq = ATTN_Q_TILE
    return pl.pallas_call(
        _gattn_kernel,
        out_shape=jax.ShapeDtypeStruct((nb, s, A_WIDTH), F32),
        grid=(nb, s // tq),
        in_specs=[
            pl.BlockSpec((1, A_HEADS, 2 * HEAD_DIM, tq), lambda b, i: (b, 0, 0, i)),
            pl.BlockSpec((1, nc, tk, KV_WIDTH), lambda b, i: (b, 0, 0, 0)),
            pl.BlockSpec((1, nc, A_KV_HEADS, VT_ROWS, tk), lambda b, i: (b, 0, 0, 0, 0)),
        ],
        out_specs=pl.BlockSpec((1, tq, A_WIDTH), lambda b, i: (b, i, 0)),
        scratch_shapes=[pltpu.VMEM((1, tq), F32)] * A_HEADS + [pltpu.VMEM((VT_ROWS, tq), F32)] * A_HEADS,
        compiler_params=_params(("parallel", "parallel")), name="gattn",
    )(qz, ka, vat)


def _nattn_kernel(*refs):
    q_ref = refs[0]
    k_refs = refs[1:1 + BAND_BLKS]
    v_refs = refs[1 + BAND_BLKS:1 + 2 * BAND_BLKS]
    bias_ref = refs[1 + 2 * BAND_BLKS]
    o_ref = refs[2 + 2 * BAND_BLKS]
    first_head = lax.broadcasted_iota(jnp.int32, (QBLK, LANES), 1) < HEAD_DIM
    dn = (((1,), (1,)), ((), ()))
    def band(refs, c):
        return jnp.concatenate([refs[t][0, 0, :, c * LANES:(c + 1) * LANES] for t in range(BAND_BLKS)], axis=0)

    def scores(h):
        c, hh = h // 2, h % 2
        qp = q_ref[0, :, c * LANES:(c + 1) * LANES]
        qm = jnp.where(first_head if hh == 0 else jnp.logical_not(first_head), qp, jnp.zeros_like(qp))
        return lax.dot_general(qm, band(k_refs, c), dn, preferred_element_type=F32) + bias_ref[0, h]

    outs = []
    s_next = scores(0)
    for h in range(B_HEADS):
        s = s_next
        if h + 1 < B_HEADS:
            s_next = scores(h + 1)
        m = jnp.max(s, axis=-1, keepdims=True)
        p = jnp.exp(s - m)
        l = jnp.sum(p, axis=-1, keepdims=True)
        outs.append(jnp.dot(p.astype(BF16), band(v_refs, h // 2), preferred_element_type=F32) * (1.0 / l))
    pairs = [jnp.where(first_head, outs[2 * c], outs[2 * c + 1]) for c in range(B_HEADS // 2)]
    o_ref[0] = jnp.concatenate(pairs, axis=-1)


def _band_start_blk(j, nblk):
    rows_per_blk = QBLK // GRID_W
    rows = nblk * rows_per_blk
    bs = jnp.clip(j * rows_per_blk - WIN_R // 2, 0, rows - BAND_ROWS)
    return bs // rows_per_blk


def _bias_variant(j, nblk):
    return jnp.where(j < 2, j, jnp.where(j >= nblk - 2, j - nblk + N_BIAS_VARIANTS, 2))


def _nattn(lidx, qn, kn, vn, bias):
    nb, s, _ = qn.shape
    nblk = s // QBLK
    band_specs = [pl.BlockSpec((1, 1, QBLK, B_WIDTH),
                               functools.partial(lambda b, j, l, t: (b, _band_start_blk(j, nblk) + t, 0, 0), t=t))
                  for t in range(BAND_BLKS)]
    grid_spec = pltpu.PrefetchScalarGridSpec(
        num_scalar_prefetch=1,
        grid=(nb, nblk),
        in_specs=[pl.BlockSpec((1, QBLK, B_WIDTH), lambda b, j, l: (b, j, 0))]
        + band_specs + band_specs
        + [pl.BlockSpec((1, B_HEADS, QBLK, BAND_ROWS * GRID_W),
                        lambda b, j, l: (l[0] * N_BIAS_VARIANTS + _bias_variant(j, nblk), 0, 0, 0))],
        out_specs=pl.BlockSpec((1, QBLK, B_WIDTH), lambda b, j, l: (b, j, 0)),
    )

    def kernel(lidx_ref, *refs):
        del lidx_ref
        _nattn_kernel(*refs)

    return pl.pallas_call(
        kernel, out_shape=jax.ShapeDtypeStruct((nb, s, B_WIDTH), F32), grid_spec=grid_spec,
        compiler_params=_params(("parallel", "arbitrary")), name="nattn",
    )(lidx, qn, *([kn] * BAND_BLKS), *([vn] * BAND_BLKS), bias)


def _nattn_bias_tables(rpb):
    n_layers = rpb.shape[0]
    n_r, n_c = 2 * WIN_R - 1, 2 * WIN_C - 1
    qc = np.arange(GRID_W)[:, None]
    kc = np.arange(GRID_W)[None, :]
    ic = np.clip(kc - qc + WIN_C - 1, 0, n_c - 1)
    cs = np.clip(qc - WIN_C // 2, 0, GRID_W - WIN_C)
    col_ok = (kc >= cs) & (kc < cs + WIN_C)
    onehot = jnp.asarray((ic.reshape(-1)[None, :] == np.arange(n_c)[:, None]).astype(np.float32))
    cols = jnp.dot(rpb.astype(F32).reshape(-1, n_c), onehot, precision=lax.Precision.HIGHEST)
    cols = cols.reshape(n_layers, B_HEADS, n_r, GRID_W, GRID_W)
    rows_per_blk = QBLK // GRID_W
    rows = 16
    variants = []
    for r0 in (0, 2, 4, rows - 4, rows - 2):
        bs = min(max(r0 - WIN_R // 2, 0), rows - BAND_ROWS)
        q_rows = []
        for qo in range(rows_per_blk):
            qr = r0 + qo
            rs = min(max(qr - WIN_R // 2, 0), rows - WIN_R)
            k_rows = []
            for ko in range(BAND_ROWS):
                kr = bs + ko
                ir = min(max(kr - qr + WIN_R - 1, 0), n_r - 1)
                if rs <= kr < rs + WIN_R:
                    k_rows.append(jnp.where(jnp.asarray(col_ok)[None, None], cols[:, :, ir], NEG_INF))
                else:
                    k_rows.append(jnp.full((n_layers, B_HEADS, GRID_W, GRID_W), NEG_INF, F32))
            q_rows.append(jnp.concatenate(k_rows, axis=-1))
        variants.append(jnp.concatenate(q_rows, axis=-2))
    t = jnp.stack(variants, axis=1)
    return t.reshape(n_layers * N_BIAS_VARIANTS, B_HEADS, QBLK, BAND_ROWS * GRID_W)


def _group_member(x, d, wrapped):
    return jnp.where(wrapped, pltpu.roll(x, EXPERTS_PER_GROUP - d, 0), pltpu.roll(x, N_EXPERTS - d, 0))


def _route_t(logits_t):
    mx = jnp.max(logits_t, axis=0, keepdims=True)
    ex = jnp.exp(logits_t - mx)
    probs = ex / jnp.sum(ex, axis=0, keepdims=True)
    e_idx = lax.broadcasted_iota(jnp.int32, probs.shape, 0)
    pos = e_idx % EXPERTS_PER_GROUP
    rank = jnp.zeros(probs.shape, jnp.int32)
    for d in range(1, EXPERTS_PER_GROUP):
        wrapped = pos + d >= EXPERTS_PER_GROUP
        other = _group_member(probs, d, wrapped)
        beats = (other > probs) | ((other == probs) & wrapped)
        rank = rank + beats.astype(jnp.int32)
    top = rank < 2
    kept = jnp.where(top, probs, 0.0)
    score = kept
    for d in range(1, EXPERTS_PER_GROUP):
        score = score + _group_member(kept, d, pos + d >= EXPERTS_PER_GROUP)
    grp = e_idx // EXPERTS_PER_GROUP
    lost = jnp.zeros(probs.shape, jnp.bool_)
    for d in range(1, N_GROUPS):
        other = pltpu.roll(score, N_EXPERTS - d * EXPERTS_PER_GROUP, 0)
        earlier = grp + d >= N_GROUPS
        lost = lost | (other > score) | ((other == score) & earlier)
    won = jnp.logical_not(lost)
    gates = jnp.where(top & won, probs / score, 0.0)
    group_onehot = jnp.where(won & (pos == 0), 1.0, 0.0)
    return gates, group_onehot


def _post_attn_kernel(lidx_ref, oa_ref, ob_ref, x_ref, wo_ref, ga_ref, gb_ref, lng_ref, lnb_ref,
                      wrh_ref, wrl_ref, br_ref, h_ref, hb_ref, gates_ref, goh_ref):
    del lidx_ref
    a = oa_ref[...]
    b = ob_ref[...]
    na = a * lax.rsqrt(jnp.mean(a * a, axis=-1, keepdims=True) + LN_EPS) * ga_ref[0]
    nb_ = b * lax.rsqrt(jnp.mean(b * b, axis=-1, keepdims=True) + LN_EPS) * gb_ref[0]
    cat = jnp.concatenate([na, nb_], axis=-1).astype(BF16)
    mix = jnp.dot(cat, wo_ref[0], preferred_element_type=F32)
    h = _layer_norm(ALPHA * x_ref[...] + mix, lng_ref[0, 0:1], lnb_ref[0, 0:1])
    h_ref[...] = h
    h_hi = h.astype(BF16)
    hb_ref[...] = h_hi
    h_lo = (h - h_hi.astype(F32)).astype(BF16)
    dn = (((1,), (1,)), ((), ()))
    logits_t = (lax.dot_general(wrh_ref[...], h_hi, dn, preferred_element_type=F32)
                + lax.dot_general(wrh_ref[...], h_lo, dn, preferred_element_type=F32)
                + lax.dot_general(wrl_ref[...], h_hi, dn, preferred_element_type=F32)) + br_ref[...]
    gates_t, group_onehot = _route_t(logits_t)
    goh_ref[...] = group_onehot.astype(BF16)
    tm = h.shape[0]
    padded = jnp.concatenate([gates_t, jnp.zeros((LANES - N_EXPERTS, tm), F32)], axis=0)
    gates_ref[...] = padded.T


def _post_attn(lidx, oa, ob, x, wo, ga, gb, lng, lnb, wrh, wrl, br):
    t = x.shape[0]
    tm = ROW_TILE
    row = lambda i, l: (i, 0)
    lay = lambda i, l: (l[0], 0, 0)
    full = lambda i, l: (0, 0)
    grid_spec = pltpu.PrefetchScalarGridSpec(
        num_scalar_prefetch=1,
        grid=(t // tm,),
        in_specs=[
            pl.BlockSpec((tm, A_WIDTH), row),
            pl.BlockSpec((tm, B_WIDTH), row),
            pl.BlockSpec((tm, D_MODEL), row),
            pl.BlockSpec((1, D_MODEL, D_MODEL), lay),
            pl.BlockSpec((1, 1, A_WIDTH), lay),
            pl.BlockSpec((1, 1, B_WIDTH), lay),
            pl.BlockSpec((1, 3, D_MODEL), lay),
            pl.BlockSpec((1, 3, D_MODEL), lay),
            pl.BlockSpec((N_EXPERTS, D_MODEL), full),
            pl.BlockSpec((N_EXPERTS, D_MODEL), full),
            pl.BlockSpec((N_EXPERTS, 1), full),
        ],
        out_specs=[
            pl.BlockSpec((tm, D_MODEL), row),
            pl.BlockSpec((tm, D_MODEL), row),
            pl.BlockSpec((tm, LANES), row),
            pl.BlockSpec((N_EXPERTS, tm), lambda i, l: (0, i)),
        ],
    )
    return pl.pallas_call(
        _post_attn_kernel,
        out_shape=(jax.ShapeDtypeStruct((t, D_MODEL), F32),
                   jax.ShapeDtypeStruct((t, D_MODEL), BF16),
                   jax.ShapeDtypeStruct((t, LANES), F32),
                   jax.ShapeDtypeStruct((N_EXPERTS, t), BF16)),
        grid_spec=grid_spec,
        compiler_params=_params(("parallel",)), name="post_attn",
    )(lidx, oa, ob, x, wo, ga, gb, lng, lnb, wrh, wrl, br)


def _moe_kernel(lidx_ref, bounds_ref, hb_ref, gates_ref, goh_ref, upper_ref, wg_ref, wu_ref, wd_ref, o_ref,
                perm_ref, xs_ref, gs_ref, ys_ref):
    del lidx_ref
    i = pl.program_id(0)
    e = pl.program_id(1)
    tm = hb_ref.shape[0]
    base = i * (2 * N_GROUPS)

    @pl.when(e == 0)
    def _sort():
        goh = goh_ref[...]
        ranks = jnp.dot(goh, upper_ref[...], preferred_element_type=F32)
        row = lax.broadcasted_iota(jnp.int32, (N_EXPERTS, 1), 0)
        start = jnp.zeros((N_EXPERTS, 1), F32)
        for g in range(N_GROUPS):
            start = jnp.where(row == g * EXPERTS_PER_GROUP, bounds_ref[base + g].astype(F32), start)
        pos = jnp.sum(goh.astype(F32) * (ranks + start), axis=0, keepdims=True)
        slot = lax.broadcasted_iota(jnp.int32, (tm, tm), 0)
        perm = jnp.where(slot == pos.astype(jnp.int32), 1.0, 0.0).astype(BF16)
        perm_ref[...] = perm
        xs_ref[...] = jnp.dot(perm, hb_ref[...], preferred_element_type=F32).astype(BF16)
        gates = gates_ref[...]
        g_hi = gates.astype(BF16)
        g_lo = (gates - g_hi.astype(F32)).astype(BF16)
        gs_ref[...] = (jnp.dot(perm, g_hi, preferred_element_type=F32)
                       + jnp.dot(perm, g_lo, preferred_element_type=F32))
        ys_ref[...] = jnp.zeros(ys_ref.shape, F32)

    grp = e // EXPERTS_PER_GROUP
    first = bounds_ref[base + grp] // MOE_SUB
    stop = (bounds_ref[base + N_GROUPS + grp] + MOE_SUB - 1) // MOE_SUB
    lane = lax.broadcasted_iota(jnp.int32, (MOE_SUB, LANES), 1)

    def sub_body(s, carry):
        rows = pl.ds(pl.multiple_of(s * MOE_SUB, MOE_SUB), MOE_SUB)
        x = xs_ref[rows, :]
        gate = jnp.dot(x, wg_ref[0, 0], preferred_element_type=F32)
        up = jnp.dot(x, wu_ref[0, 0], preferred_element_type=F32)
        act = (gate * jax.nn.sigmoid(gate) * up).astype(BF16)
        y = jnp.dot(act, wd_ref[0, 0], preferred_element_type=F32)
        g_col = jnp.sum(jnp.where(lane == e, gs_ref[rows, :], 0.0), axis=1, keepdims=True)
        ys_ref[rows, :] += g_col * y
        return carry

    lax.fori_loop(first, stop, sub_body, 0)

    @pl.when(e == N_EXPERTS - 1)
    def _unsort():
        o_ref[...] = lax.dot_general(perm_ref[...], ys_ref[...].astype(BF16), (((0,), (0,)), ((), ())),
                                     preferred_element_type=F32)


def _tile_bounds(goh, tm):
    nt = goh.shape[1] // tm
    cnt = goh[::EXPERTS_PER_GROUP].astype(F32).reshape(N_GROUPS, nt, tm).sum(-1).T.astype(jnp.int32)
    end = jnp.cumsum(cnt, axis=1)
    return jnp.concatenate([end - cnt, end], axis=1).reshape(-1)


def _moe(lidx, hb, gates, goh, wg, wu, wd):
    t = hb.shape[0]
    tm = min(MOE_TILE, t)
    bounds = _tile_bounds(goh, tm)
    upper = (jnp.arange(tm)[:, None] < jnp.arange(tm)[None, :]).astype(BF16)
    grid_spec = pltpu.PrefetchScalarGridSpec(
        num_scalar_prefetch=2,
        grid=(t // tm, N_EXPERTS),
        in_specs=[
            pl.BlockSpec((tm, D_MODEL), lambda i, e, l, b: (i, 0)),
            pl.BlockSpec((tm, LANES), lambda i, e, l, b: (i, 0)),
            pl.BlockSpec((N_EXPERTS, tm), lambda i, e, l, b: (0, i)),
            pl.BlockSpec((tm, tm), lambda i, e, l, b: (0, 0)),
            pl.BlockSpec((1, 1, D_MODEL, D_FF_EXPERT), lambda i, e, l, b: (l[0], e, 0, 0)),
            pl.BlockSpec((1, 1, D_MODEL, D_FF_EXPERT), lambda i, e, l, b: (l[0], e, 0, 0)),
            pl.BlockSpec((1, 1, D_FF_EXPERT, D_MODEL), lambda i, e, l, b: (l[0], e, 0, 0)),
        ],
        out_specs=pl.BlockSpec((tm, D_MODEL), lambda i, e, l, b: (i, 0)),
        scratch_shapes=[pltpu.VMEM((tm, tm), BF16), pltpu.VMEM((tm, D_MODEL), BF16),
                        pltpu.VMEM((tm, LANES), F32), pltpu.VMEM((tm, D_MODEL), F32)],
    )
    return pl.pallas_call(
        _moe_kernel, out_shape=jax.ShapeDtypeStruct((t, D_MODEL), F32), grid_spec=grid_spec,
        compiler_params=_params(("parallel", "arbitrary")), name="moe",
    )(lidx, bounds, hb, gates, goh, upper, wg, wu, wd)


def _post_moe_kernel(lidx_ref, h_ref, f_ref, p_ref, wpg_ref, bpg_ref, wpp_ref, lng_ref, lnb_ref, x_ref):
    del lidx_ref
    h2 = _layer_norm(ALPHA * h_ref[...] + f_ref[...], lng_ref[0, 1:2], lnb_ref[0, 1:2])
    z = jnp.dot(h2.astype(BF16), wpg_ref[0], preferred_element_type=F32) + bpg_ref[0]
    proj = jnp.dot(p_ref[0].astype(BF16), wpp_ref[0], preferred_element_type=F32)
    pe = jax.nn.sigmoid(z) * proj
    x_ref[...] = _layer_norm(ALPHA * h2 + pe, lng_ref[0, 2:3], lnb_ref[0, 2:3])


def _post_moe(lidx, h, ffn, p, wpg, bpg, wpp, lng, lnb):
    t = h.shape[0]
    tm = ROW_TILE
    row = lambda i, l: (i, 0)
    lay = lambda i, l: (l[0], 0, 0)
    grid_spec = pltpu.PrefetchScalarGridSpec(
        num_scalar_prefetch=1,
        grid=(t // tm,),
        in_specs=[
            pl.BlockSpec((tm, D_MODEL), row),
            pl.BlockSpec((tm, D_MODEL), row),
            pl.BlockSpec((1, tm, PLE_DIM), lambda i, l: (l[0], i, 0)),
            pl.BlockSpec((1, D_MODEL, D_MODEL), lay),
            pl.BlockSpec((1, 1, D_MODEL), lay),
            pl.BlockSpec((1, PLE_DIM, D_MODEL), lay),
            pl.BlockSpec((1, 3, D_MODEL), lay),
            pl.BlockSpec((1, 3, D_MODEL), lay),
        ],
        out_specs=pl.BlockSpec((tm, D_MODEL), row),
    )
    return pl.pallas_call(
        _post_moe_kernel, out_shape=jax.ShapeDtypeStruct((t, D_MODEL), F32), grid_spec=grid_spec,
        compiler_params=_params(("parallel",)), name="post_moe",
    )(lidx, h, ffn, p, wpg, bpg, wpp, lng, lnb)


def _rope_tables_t(s):
    t = jnp.arange(s, dtype=jnp.int32)
    row = (t // GRID_W).astype(F32)
    col = (t % GRID_W).astype(F32)
    n_freq = HEAD_DIM // 4
    inv_freq = ROPE_THETA ** (-jnp.arange(n_freq, dtype=F32) / n_freq)
    ang = jnp.concatenate([inv_freq[:, None] * row[None, :], inv_freq[:, None] * col[None, :]], axis=0)
    return jnp.cos(ang), jnp.sin(ang)


def _prepare_weights(w_in, q_norm_a, k_norm_a, rpb_b, g_out_a, g_out_b, w_out, w_router, b_router,
                     w_gate, w_up, w_down, w_ple_gate, b_ple_gate, w_ple_proj):
    n_layers = w_in.shape[0]
    qa, ka, va, qn, kn, vn = jnp.split(w_in, [512, 640, 768, 1280, 1792], axis=-1)
    wt = jnp.swapaxes(jnp.concatenate([qa, ka, va], axis=-1), 1, 2).astype(BF16)
    wn = jnp.concatenate([qn, kn, vn], axis=-1).astype(BF16)
    wr_t = w_router.T
    wr_hi = wr_t.astype(BF16)
    wr_lo = (wr_t - wr_hi.astype(F32)).astype(BF16)
    return dict(
        wt=wt, wn=wn,
        qg=q_norm_a.reshape(n_layers, HEAD_DIM, 1), kg=k_norm_a.reshape(n_layers, HEAD_DIM, 1),
        bias=_nattn_bias_tables(rpb_b),
        ga=g_out_a.reshape(n_layers, 1, A_WIDTH), gb=g_out_b.reshape(n_layers, 1, B_WIDTH),
        wo=w_out.astype(BF16), wr_hi=wr_hi, wr_lo=wr_lo, br=b_router.reshape(N_EXPERTS, 1),
        wg=w_gate.astype(BF16), wu=w_up.astype(BF16), wd=w_down.astype(BF16),
        wpg=w_ple_gate.astype(BF16), bpg=b_ple_gate.reshape(n_layers, 1, D_MODEL), wpp=w_ple_proj.astype(BF16),
    )


def _layer(lidx, x, p_flat, w, ln_g, ln_b, rope):
    nb, s, _ = x.shape
    qz, ka, vat, qn, kn, vn = _inproj(lidx, x, w["wt"], w["wn"], rope[0], rope[1], w["qg"], w["kg"])
    oa = _gattn(qz, ka, vat)
    ob = _nattn(lidx, qn, kn, vn, w["bias"])
    t = nb * s
    h, hb, gates, goh = _post_attn(lidx, oa.reshape(t, A_WIDTH), ob.reshape(t, B_WIDTH), x.reshape(t, D_MODEL),
                              w["wo"], w["ga"], w["gb"], ln_g, ln_b, w["wr_hi"], w["wr_lo"], w["br"])
    ffn = _moe(lidx, hb, gates, goh, w["wg"], w["wu"], w["wd"])
    xo = _post_moe(lidx, h, ffn, p_flat, w["wpg"], w["bpg"], w["wpp"], ln_g, ln_b)
    return xo.reshape(nb, s, D_MODEL)


def kernel(x_prompt, x_sample, p_prompt, p_sample, w_in, q_norm_a, k_norm_a, rpb_b, g_out_a, g_out_b, w_out, ln_g, ln_b, w_router, b_router, w_gate, w_up, w_down, w_ple_gate, b_ple_gate, w_ple_proj):
    n_layers = w_in.shape[0]
    w = _prepare_weights(w_in, q_norm_a, k_norm_a, rpb_b, g_out_a, g_out_b, w_out, w_router, b_router,
                         w_gate, w_up, w_down, w_ple_gate, b_ple_gate, w_ple_proj)
    trunks = []
    for x, p in ((x_prompt, p_prompt), (x_sample, p_sample)):
        nb, s, _ = x.shape
        trunks.append((x, p.reshape(n_layers, nb * s, PLE_DIM), _rope_tables_t(s)))

    def body(xs, layer):
        lidx = jnp.reshape(layer, (1,)).astype(jnp.int32)
        new = tuple(_layer(lidx, x, trunks[i][1], w, ln_g, ln_b, trunks[i][2]) for i, x in enumerate(xs))
        return new, None

    (y_prompt, y_sample), _ = lax.scan(body, (trunks[0][0], trunks[1][0]), jnp.arange(n_layers, dtype=jnp.int32))
    return (y_prompt, y_sample)
```

```python
import functools
import math

import jax
import jax.numpy as jnp
import numpy as np
from jax import lax
from jax.experimental import pallas as pl
from jax.experimental.pallas import tpu as pltpu

F32 = jnp.float32
BF16 = jnp.bfloat16

D_MODEL = 1024
N_LAYERS = 4
HEAD_DIM = 64
A_HEADS = 8
A_KV_HEADS = 2
A_GROUP = A_HEADS // A_KV_HEADS
B_HEADS = 8
A_WIDTH = A_HEADS * HEAD_DIM
KV_WIDTH = A_KV_HEADS * HEAD_DIM
B_WIDTH = B_HEADS * HEAD_DIM
GRID_W = 64
QBLK = 128
WIN_R = 8
WIN_C = 16
ROPE_THETA = 10000.0
N_EXPERTS = 16
N_GROUPS = 4
EXPERTS_PER_GROUP = N_EXPERTS // N_GROUPS
D_FF_EXPERT = 512
PLE_DIM = 256
LN_EPS = 1e-5
QK_EPS = 1e-6
NEG_INF = -1e30
ALPHA = (2 * N_LAYERS) ** 0.25
Q_SCALE = HEAD_DIM ** -0.5
LOG2_E = math.log2(math.e)

LANES = 128

TOKEN_TILE = 512
ATTN_Q_TILE = 512
ROW_TILE = 512
MOE_TILE = 1024
MOE_SUB = 128
EXPERTS_PER_STEP = 2
BAND_ROWS = 10
BAND_BLKS = BAND_ROWS * GRID_W // QBLK
N_BIAS_VARIANTS = 5
VMEM_LIMIT = 56 * 1024 * 1024

_QA0, _KA0, _VA0, _PT_ROWS = 0, 512, 640, 768
_QN0, _KN0, _VN0, _NAT_COLS = 0, 512, 1024, 1536
ONES_ROWS = 16
VT_ROWS = HEAD_DIM + ONES_ROWS
SHIFT_BOUND_MAX = 40.0


def _params(sem):
    return pltpu.CompilerParams(dimension_semantics=sem, vmem_limit_bytes=VMEM_LIMIT)


def _layer_norm(y, g, b):
    mu = jnp.mean(y, axis=-1, keepdims=True)
    d = y - mu
    var = jnp.mean(d * d, axis=-1, keepdims=True)
    return d * lax.rsqrt(var + LN_EPS) * g + b


def _rope_t(y, cos, sin):
    cr, cc = cos[0:16], cos[16:32]
    sr, sc = sin[0:16], sin[16:32]
    a1, a2, b1, b2 = y[0:16], y[16:32], y[32:48], y[48:64]
    return jnp.concatenate([a1 * cr - a2 * sr, a1 * sr + a2 * cr,
                            b1 * cc - b2 * sc, b1 * sc + b2 * cc], axis=0)


def _norm_rope_t(y, gain, cos, sin):
    ms = jnp.sum(y * y, axis=0, keepdims=True) * (1.0 / HEAD_DIM)
    return _rope_t(y * lax.rsqrt(ms + QK_EPS) * gain, cos, sin)


def _inproj_kernel(lidx_ref, x_ref, wt_ref, wn_ref, cos_ref, sin_ref, qg_ref, kg_ref,
                   qz_ref, ka_ref, vat_ref, qn_ref, kn_ref, vn_ref, qnorm_ref, knorm_ref):
    del lidx_ref
    xb = x_ref[0].astype(BF16)
    pt = lax.dot_general(wt_ref[0], xb, (((1,), (1,)), ((), ())),
                         preferred_element_type=F32)
    tm = xb.shape[0]
    cos = cos_ref[...]
    sin = sin_ref[...]
    qg = qg_ref[0] * (Q_SCALE * LOG2_E)
    kg = kg_ref[0]
    zeros = jnp.zeros((HEAD_DIM, tm), BF16)
    head_row = lax.broadcasted_iota(jnp.int32, (A_HEADS, 1), 0)
    l2 = lambda y: jnp.sqrt(jnp.sum(y * y, axis=0, keepdims=True))
    qnorm = jnp.zeros((A_HEADS, tm), F32)
    for h in range(A_HEADS):
        qf = _norm_rope_t(pt[_QA0 + h * HEAD_DIM:_QA0 + (h + 1) * HEAD_DIM], qg, cos, sin)
        qnorm = jnp.where(head_row == h, l2(qf), qnorm)
        q = qf.astype(BF16)
        if h // A_GROUP == 0:
            qz_ref[0, h] = jnp.concatenate([q, zeros], axis=0)
        else:
            qz_ref[0, h] = jnp.concatenate([zeros, q], axis=0)
    qnorm_ref[0] = qnorm
    ks = [_norm_rope_t(pt[_KA0 + g * HEAD_DIM:_KA0 + (g + 1) * HEAD_DIM], kg, cos, sin) for g in range(A_KV_HEADS)]
    knorm = jnp.zeros((A_HEADS, tm), F32)
    for g in range(A_KV_HEADS):
        knorm = jnp.where(head_row == g, l2(ks[g]), knorm)
    knorm_ref[0, 0] = knorm
    kt = jnp.concatenate(ks, axis=0)
    ka_ref[0, 0] = kt.T.astype(BF16)
    ones = jnp.ones((ONES_ROWS, tm), BF16)
    for g in range(A_KV_HEADS):
        v = pt[_VA0 + g * HEAD_DIM:_VA0 + (g + 1) * HEAD_DIM].astype(BF16)
        vat_ref[0, 0, g] = jnp.concatenate([v, ones], axis=0)
    nat = jnp.dot(xb, wn_ref[0], preferred_element_type=F32)
    qn_ref[0] = (nat[:, _QN0:_QN0 + B_WIDTH] * Q_SCALE).astype(BF16)
    kn_ref[0] = nat[:, _KN0:_KN0 + B_WIDTH].astype(BF16).reshape(tm // QBLK, QBLK, B_WIDTH)
    vn_ref[0] = nat[:, _VN0:_VN0 + B_WIDTH].astype(BF16).reshape(tm // QBLK, QBLK, B_WIDTH)


def _inproj(lidx, x, wt, wn, cos_t, sin_t, qg, kg):
    nb, s, _ = x.shape
    tm = TOKEN_TILE
    nc = s // tm
    out_shape = (
        jax.ShapeDtypeStruct((nb, A_HEADS, 2 * HEAD_DIM, s), BF16),
        jax.ShapeDtypeStruct((nb, nc, tm, KV_WIDTH), BF16),
        jax.ShapeDtypeStruct((nb, nc, A_KV_HEADS, VT_ROWS, tm), BF16),
        jax.ShapeDtypeStruct((nb, s, B_WIDTH), BF16),
        jax.ShapeDtypeStruct((nb, s // QBLK, QBLK, B_WIDTH), BF16),
        jax.ShapeDtypeStruct((nb, s // QBLK, QBLK, B_WIDTH), BF16),
        jax.ShapeDtypeStruct((nb, A_HEADS, s), F32),
        jax.ShapeDtypeStruct((nb, nc, A_HEADS, tm), F32),
    )
    grid_spec = pltpu.PrefetchScalarGridSpec(
        num_scalar_prefetch=1,
        grid=(nb, nc),
        in_specs=[
            pl.BlockSpec((1, tm, D_MODEL), lambda b, c, l: (b, c, 0)),
            pl.BlockSpec((1, _PT_ROWS, D_MODEL), lambda b, c, l: (l[0], 0, 0)),
            pl.BlockSpec((1, D_MODEL, _NAT_COLS), lambda b, c, l: (l[0], 0, 0)),
            pl.BlockSpec((HEAD_DIM // 2, tm), lambda b, c, l: (0, c)),
            pl.BlockSpec((HEAD_DIM // 2, tm), lambda b, c, l: (0, c)),
            pl.BlockSpec((1, HEAD_DIM, 1), lambda b, c, l: (l[0], 0, 0)),
            pl.BlockSpec((1, HEAD_DIM, 1), lambda b, c, l: (l[0], 0, 0)),
        ],
        out_specs=[
            pl.BlockSpec((1, A_HEADS, 2 * HEAD_DIM, tm), lambda b, c, l: (b, 0, 0, c)),
            pl.BlockSpec((1, 1, tm, KV_WIDTH), lambda b, c, l: (b, c, 0, 0)),
            pl.BlockSpec((1, 1, A_KV_HEADS, VT_ROWS, tm), lambda b, c, l: (b, c, 0, 0, 0)),
            pl.BlockSpec((1, tm, B_WIDTH), lambda b, c, l: (b, c, 0)),
            pl.BlockSpec((1, tm // QBLK, QBLK, B_WIDTH), lambda b, c, l: (b, c, 0, 0)),
            pl.BlockSpec((1, tm // QBLK, QBLK, B_WIDTH), lambda b, c, l: (b, c, 0, 0)),
            pl.BlockSpec((1, A_HEADS, tm), lambda b, c, l: (b, 0, c)),
            pl.BlockSpec((1, 1, A_HEADS, tm), lambda b, c, l: (b, c, 0, 0)),
        ],
    )
    return pl.pallas_call(
        _inproj_kernel, out_shape=out_shape, grid_spec=grid_spec,
        compiler_params=_params(("parallel", "parallel")), name="inproj",
    )(lidx, x, wt, wn, cos_t, sin_t, qg, kg)


def _col_max(st):
    slabs = [st[r:r + 128] for r in range(0, st.shape[0], 128)]
    while len(slabs) > 1:
        slabs = [jnp.maximum(slabs[i], slabs[i + 1]) for i in range(0, len(slabs), 2)]
    return jnp.max(slabs[0], axis=0, keepdims=True)


def _gattn_kernel(qz_ref, k_ref, vt_ref, qn_ref, kn_ref, o_ref, *scratch):
    m_refs, acc_refs = scratch[:A_HEADS], scratch[A_HEADS:]
    n_chunks = k_ref.shape[1]
    for h in range(A_HEADS):
        acc_refs[h][...] = jnp.zeros(acc_refs[h].shape, F32)

    k_norm_max = jnp.max(jnp.max(kn_ref[0], axis=0), axis=1, keepdims=True)
    row = lax.broadcasted_iota(jnp.int32, (A_HEADS, 1), 0)
    k_of_head = jnp.where(row < A_GROUP, k_norm_max[0:1], k_norm_max[1:2])
    bound = qn_ref[0] * k_of_head
    bounded = jnp.max(bound) <= SHIFT_BOUND_MAX

    def chunk_body(c, carry, online):
        kc = k_ref[0, c]
        scores = lambda h: jnp.dot(kc, qz_ref[0, h], preferred_element_type=F32)
        st_next = scores(0)
        for h in range(A_HEADS):
            st = st_next
            if h + 1 < A_HEADS:
                st_next = scores(h + 1)
            vt = vt_ref[0, c, h // A_GROUP]
            if online:
                m_old = m_refs[h][...]
                m_new = jnp.maximum(m_old, _col_max(st))
                alpha = jnp.exp2(m_old - m_new)
                p = jnp.exp2(st - m_new).astype(BF16)
                acc_refs[h][...] = alpha * acc_refs[h][...] + jnp.dot(vt, p, preferred_element_type=F32)
                m_refs[h][...] = m_new
            else:
                p = jnp.exp2(st - m_refs[h][...]).astype(BF16)
                acc_refs[h][...] += jnp.dot(vt, p, preferred_element_type=F32)
        return carry

    @pl.when(bounded)
    def _():
        for h in range(A_HEADS):
            m_refs[h][...] = bound[h:h + 1]
        lax.fori_loop(0, n_chunks, functools.partial(chunk_body, online=False), 0)

    @pl.when(jnp.logical_not(bounded))
    def _():
        for h in range(A_HEADS):
            m_refs[h][...] = jnp.full(m_refs[h].shape, -jnp.inf, F32)
        lax.fori_loop(0, n_chunks, functools.partial(chunk_body, online=True), 0)

    outs = []
    for h in range(A_HEADS):
        acc = acc_refs[h][...]
        outs.append(acc[0:HEAD_DIM] * (1.0 / acc[HEAD_DIM:HEAD_DIM + 1]))
    o_ref[0] = jnp.concatenate(outs, axis=0).T


def _gattn(qz, ka, vat, qnorm, knorm):
    nb, _, _, s = qz.shape
    nc, tk = ka.shape[1], ka.shape[2]
    tq = ATTN_Q_TILE
    return pl.pallas_call(
        _gattn_kernel,
        out_shape=jax.ShapeDtypeStruct((nb, s, A_WIDTH), F32),
        grid=(nb, s // tq),
        in_specs=[
            pl.BlockSpec((1, A_HEADS, 2 * HEAD_DIM, tq), lambda b, i: (b, 0, 0, i)),
            pl.BlockSpec((1, nc, tk, KV_WIDTH), lambda b, i: (b, 0, 0, 0)),
            pl.BlockSpec((1, nc, A_KV_HEADS, VT_ROWS, tk), lambda b, i: (b, 0, 0, 0, 0)),
            pl.BlockSpec((1, A_HEADS, tq), lambda b, i: (b, 0, i)),
            pl.BlockSpec((1, nc, A_HEADS, tk), lambda b, i: (b, 0, 0, 0)),
        ],
        out_specs=pl.BlockSpec((1, tq, A_WIDTH), lambda b, i: (b, i, 0)),
        scratch_shapes=[pltpu.VMEM((1, tq), F32)] * A_HEADS + [pltpu.VMEM((VT_ROWS, tq), F32)] * A_HEADS,
        compiler_params=_params(("parallel", "parallel")), name="gattn",
    )(qz, ka, vat, qnorm, knorm)


def _nattn_kernel(*refs):
    q_ref = refs[0]
    k_refs = refs[1:1 + BAND_BLKS]
    v_refs = refs[1 + BAND_BLKS:1 + 2 * BAND_BLKS]
    bias_ref = refs[1 + 2 * BAND_BLKS]
    o_ref = refs[2 + 2 * BAND_BLKS]
    first_head = lax.broadcasted_iota(jnp.int32, (QBLK, LANES), 1) < HEAD_DIM
    dn = (((1,), (1,)), ((), ()))
    def band(refs, c):
        return jnp.concatenate([refs[t][0, 0, :, c * LANES:(c + 1) * LANES] for t in range(BAND_BLKS)], axis=0)

    def scores(h):
        c, hh = h // 2, h % 2
        qp = q_ref[0, :, c * LANES:(c + 1) * LANES]
        qm = jnp.where(first_head if hh == 0 else jnp.logical_not(first_head), qp, jnp.zeros_like(qp))
        return lax.dot_general(qm, band(k_refs, c), dn, preferred_element_type=F32) + bias_ref[0, h]

    outs = []
    s_next = scores(0)
    for h in range(B_HEADS):
        s = s_next
        if h + 1 < B_HEADS:
            s_next = scores(h + 1)
        m = jnp.max(s, axis=-1, keepdims=True)
        p = jnp.exp(s - m)
        l = jnp.sum(p, axis=-1, keepdims=True)
        outs.append(jnp.dot(p.astype(BF16), band(v_refs, h // 2), preferred_element_type=F32) * (1.0 / l))
    pairs = [jnp.where(first_head, outs[2 * c], outs[2 * c + 1]) for c in range(B_HEADS // 2)]
    o_ref[0] = jnp.concatenate(pairs, axis=-1)


def _band_start_blk(j, nblk):
    rows_per_blk = QBLK // GRID_W
    rows = nblk * rows_per_blk
    bs = jnp.clip(j * rows_per_blk - WIN_R // 2, 0, rows - BAND_ROWS)
    return bs // rows_per_blk


def _bias_variant(j, nblk):
    return jnp.where(j < 2, j, jnp.where(j >= nblk - 2, j - nblk + N_BIAS_VARIANTS, 2))


def _nattn(lidx, qn, kn, vn, bias):
    nb, s, _ = qn.shape
    nblk = s // QBLK
    band_specs = [pl.BlockSpec((1, 1, QBLK, B_WIDTH),
                               functools.partial(lambda b, j, l, t: (b, _band_start_blk(j, nblk) + t, 0, 0), t=t))
                  for t in range(BAND_BLKS)]
    grid_spec = pltpu.PrefetchScalarGridSpec(
        num_scalar_prefetch=1,
        grid=(nb, nblk),
        in_specs=[pl.BlockSpec((1, QBLK, B_WIDTH), lambda b, j, l: (b, j, 0))]
        + band_specs + band_specs
        + [pl.BlockSpec((1, B_HEADS, QBLK, BAND_ROWS * GRID_W),
                        lambda b, j, l: (l[0] * N_BIAS_VARIANTS + _bias_variant(j, nblk), 0, 0, 0))],
        out_specs=pl.BlockSpec((1, QBLK, B_WIDTH), lambda b, j, l: (b, j, 0)),
    )

    def kernel(lidx_ref, *refs):
        del lidx_ref
        _nattn_kernel(*refs)

    return pl.pallas_call(
        kernel, out_shape=jax.ShapeDtypeStruct((nb, s, B_WIDTH), F32), grid_spec=grid_spec,
        compiler_params=_params(("parallel", "arbitrary")), name="nattn",
    )(lidx, qn, *([kn] * BAND_BLKS), *([vn] * BAND_BLKS), bias)


def _nattn_bias_tables(rpb):
    n_layers = rpb.shape[0]
    n_r, n_c = 2 * WIN_R - 1, 2 * WIN_C - 1
    qc = np.arange(GRID_W)[:, None]
    kc = np.arange(GRID_W)[None, :]
    ic = np.clip(kc - qc + WIN_C - 1, 0, n_c - 1)
    cs = np.clip(qc - WIN_C // 2, 0, GRID_W - WIN_C)
    col_ok = (kc >= cs) & (kc < cs + WIN_C)
    onehot = jnp.asarray((ic.reshape(-1)[None, :] == np.arange(n_c)[:, None]).astype(np.float32))
    cols = jnp.dot(rpb.astype(F32).reshape(-1, n_c), onehot, precision=lax.Precision.HIGHEST)
    cols = cols.reshape(n_layers, B_HEADS, n_r, GRID_W, GRID_W)
    rows_per_blk = QBLK // GRID_W
    rows = 16
    variants = []
    for r0 in (0, 2, 4, rows - 4, rows - 2):
        bs = min(max(r0 - WIN_R // 2, 0), rows - BAND_ROWS)
        q_rows = []
        for qo in range(rows_per_blk):
            qr = r0 + qo
            rs = min(max(qr - WIN_R // 2, 0), rows - WIN_R)
            k_rows = []
            for ko in range(BAND_ROWS):
                kr = bs + ko
                ir = min(max(kr - qr + WIN_R - 1, 0), n_r - 1)
                if rs <= kr < rs + WIN_R:
                    k_rows.append(jnp.where(jnp.asarray(col_ok)[None, None], cols[:, :, ir], NEG_INF))
                else:
                    k_rows.append(jnp.full((n_layers, B_HEADS, GRID_W, GRID_W), NEG_INF, F32))
            q_rows.append(jnp.concatenate(k_rows, axis=-1))
        variants.append(jnp.concatenate(q_rows, axis=-2))
    t = jnp.stack(variants, axis=1)
    return t.reshape(n_layers * N_BIAS_VARIANTS, B_HEADS, QBLK, BAND_ROWS * GRID_W)


def _group_member(x, d, wrapped):
    return jnp.where(wrapped, pltpu.roll(x, EXPERTS_PER_GROUP - d, 0), pltpu.roll(x, N_EXPERTS - d, 0))


def _route_t(logits_t):
    mx = jnp.max(logits_t, axis=0, keepdims=True)
    ex = jnp.exp(logits_t - mx)
    probs = ex / jnp.sum(ex, axis=0, keepdims=True)
    e_idx = lax.broadcasted_iota(jnp.int32, probs.shape, 0)
    pos = e_idx % EXPERTS_PER_GROUP
    rank = jnp.zeros(probs.shape, jnp.int32)
    for d in range(1, EXPERTS_PER_GROUP):
        wrapped = pos + d >= EXPERTS_PER_GROUP
        other = _group_member(probs, d, wrapped)
        beats = (other > probs) | ((other == probs) & wrapped)
        rank = rank + beats.astype(jnp.int32)
    top = rank < 2
    kept = jnp.where(top, probs, 0.0)
    score = kept
    for d in range(1, EXPERTS_PER_GROUP):
        score = score + _group_member(kept, d, pos + d >= EXPERTS_PER_GROUP)
    grp = e_idx // EXPERTS_PER_GROUP
    lost = jnp.zeros(probs.shape, jnp.bool_)
    for d in range(1, N_GROUPS):
        other = pltpu.roll(score, N_EXPERTS - d * EXPERTS_PER_GROUP, 0)
        earlier = grp + d >= N_GROUPS
        lost = lost | (other > score) | ((other == score) & earlier)
    won = jnp.logical_not(lost)
    gates = jnp.where(top & won, probs / score, 0.0)
    group_onehot = jnp.where(won & (pos == 0), 1.0, 0.0)
    return gates, group_onehot


def _post_attn_kernel(lidx_ref, oa_ref, ob_ref, x_ref, wo_ref, ga_ref, gb_ref, lng_ref, lnb_ref,
                      wrh_ref, wrl_ref, br_ref, h_ref, hb_ref, gates_ref, goh_ref):
    del lidx_ref
    a = oa_ref[...]
    b = ob_ref[...]
    na = a * lax.rsqrt(jnp.mean(a * a, axis=-1, keepdims=True) + LN_EPS) * ga_ref[0]
    nb_ = b * lax.rsqrt(jnp.mean(b * b, axis=-1, keepdims=True) + LN_EPS) * gb_ref[0]
    cat = jnp.concatenate([na, nb_], axis=-1).astype(BF16)
    mix = jnp.dot(cat, wo_ref[0], preferred_element_type=F32)
    h = _layer_norm(ALPHA * x_ref[...] + mix, lng_ref[0, 0:1], lnb_ref[0, 0:1])
    h_ref[...] = h
    h_hi = h.astype(BF16)
    hb_ref[...] = h_hi
    h_lo = (h - h_hi.astype(F32)).astype(BF16)
    dn = (((1,), (1,)), ((), ()))
    logits_t = (lax.dot_general(wrh_ref[...], h_hi, dn, preferred_element_type=F32)
                + lax.dot_general(wrh_ref[...], h_lo, dn, preferred_element_type=F32)
                + lax.dot_general(wrl_ref[...], h_hi, dn, preferred_element_type=F32)) + br_ref[...]
    gates_t, group_onehot = _route_t(logits_t)
    goh_ref[...] = group_onehot.astype(BF16)
    tm = h.shape[0]
    padded = jnp.concatenate([gates_t, jnp.zeros((LANES - N_EXPERTS, tm), F32)], axis=0)
    gates_ref[...] = padded.T


def _post_attn(lidx, oa, ob, x, wo, ga, gb, lng, lnb, wrh, wrl, br):
    t = x.shape[0]
    tm = ROW_TILE
    row = lambda i, l: (i, 0)
    lay = lambda i, l: (l[0], 0, 0)
    full = lambda i, l: (0, 0)
    grid_spec = pltpu.PrefetchScalarGridSpec(
        num_scalar_prefetch=1,
        grid=(t // tm,),
        in_specs=[
            pl.BlockSpec((tm, A_WIDTH), row),
            pl.BlockSpec((tm, B_WIDTH), row),
            pl.BlockSpec((tm, D_MODEL), row),
            pl.BlockSpec((1, D_MODEL, D_MODEL), lay),
            pl.BlockSpec((1, 1, A_WIDTH), lay),
            pl.BlockSpec((1, 1, B_WIDTH), lay),
            pl.BlockSpec((1, 3, D_MODEL), lay),
            pl.BlockSpec((1, 3, D_MODEL), lay),
            pl.BlockSpec((N_EXPERTS, D_MODEL), full),
            pl.BlockSpec((N_EXPERTS, D_MODEL), full),
            pl.BlockSpec((N_EXPERTS, 1), full),
        ],
        out_specs=[
            pl.BlockSpec((tm, D_MODEL), row),
            pl.BlockSpec((tm, D_MODEL), row),
            pl.BlockSpec((tm, LANES), row),
            pl.BlockSpec((N_EXPERTS, tm), lambda i, l: (0, i)),
        ],
    )
    return pl.pallas_call(
        _post_attn_kernel,
        out_shape=(jax.ShapeDtypeStruct((t, D_MODEL), F32),
                   jax.ShapeDtypeStruct((t, D_MODEL), BF16),
                   jax.ShapeDtypeStruct((t, LANES), F32),
                   jax.ShapeDtypeStruct((N_EXPERTS, t), BF16)),
        grid_spec=grid_spec,
        compiler_params=_params(("parallel",)), name="post_attn",
    )(lidx, oa, ob, x, wo, ga, gb, lng, lnb, wrh, wrl, br)


def _moe_kernel(lidx_ref, bounds_ref, hb_ref, gates_ref, goh_ref, upper_ref, wg_ref, wu_ref, wd_ref, o_ref,
                perm_ref, xs_ref, gs_ref, ys_ref):
    del lidx_ref
    i = pl.program_id(0)
    e = pl.program_id(1)
    tm = hb_ref.shape[0]
    base = i * (2 * N_GROUPS)

    @pl.when(e == 0)
    def _sort():
        goh = goh_ref[...]
        ranks = jnp.dot(goh, upper_ref[...], preferred_element_type=F32)
        row = lax.broadcasted_iota(jnp.int32, (N_EXPERTS, 1), 0)
        start = jnp.zeros((N_EXPERTS, 1), F32)
        for g in range(N_GROUPS):
            start = jnp.where(row == g * EXPERTS_PER_GROUP, bounds_ref[base + g].astype(F32), start)
        pos = jnp.sum(goh.astype(F32) * (ranks + start), axis=0, keepdims=True)
        slot = lax.broadcasted_iota(jnp.int32, (tm, tm), 0)
        perm = jnp.where(slot == pos.astype(jnp.int32), 1.0, 0.0).astype(BF16)
        perm_ref[...] = perm
        xs_ref[...] = jnp.dot(perm, hb_ref[...], preferred_element_type=F32).astype(BF16)
        gates = gates_ref[...]
        g_hi = gates.astype(BF16)
        g_lo = (gates - g_hi.astype(F32)).astype(BF16)
        gs_ref[...] = (jnp.dot(perm, g_hi, preferred_element_type=F32)
                       + jnp.dot(perm, g_lo, preferred_element_type=F32))
        ys_ref[...] = jnp.zeros(ys_ref.shape, F32)

    grp = (e * EXPERTS_PER_STEP) // EXPERTS_PER_GROUP
    first = bounds_ref[base + grp] // MOE_SUB
    stop = (bounds_ref[base + N_GROUPS + grp] + MOE_SUB - 1) // MOE_SUB
    lane = lax.broadcasted_iota(jnp.int32, (MOE_SUB, LANES), 1)

    def sub_body(s, carry):
        rows = pl.ds(pl.multiple_of(s * MOE_SUB, MOE_SUB), MOE_SUB)
        x = xs_ref[rows, :]
        gs = gs_ref[rows, :]
        total = None
        for j in range(EXPERTS_PER_STEP):
            gate = jnp.dot(x, wg_ref[0, 0, j], preferred_element_type=F32)
            up = jnp.dot(x, wu_ref[0, 0, j], preferred_element_type=F32)
            act = (gate * jax.nn.sigmoid(gate) * up).astype(BF16)
            y = jnp.dot(act, wd_ref[0, 0, j], preferred_element_type=F32)
            g_col = jnp.sum(jnp.where(lane == e * EXPERTS_PER_STEP + j, gs, 0.0), axis=1, keepdims=True)
            total = g_col * y if total is None else total + g_col * y
        ys_ref[rows, :] += total
        return carry

    lax.fori_loop(first, stop, sub_body, 0)

    @pl.when(e == N_EXPERTS // EXPERTS_PER_STEP - 1)
    def _unsort():
        o_ref[...] = lax.dot_general(perm_ref[...], ys_ref[...].astype(BF16), (((0,), (0,)), ((), ())),
                                     preferred_element_type=F32)


def _tile_bounds(goh, tm):
    nt = goh.shape[1] // tm
    cnt = goh[::EXPERTS_PER_GROUP].astype(F32).reshape(N_GROUPS, nt, tm).sum(-1).T.astype(jnp.int32)
    end = jnp.cumsum(cnt, axis=1)
    return jnp.concatenate([end - cnt, end], axis=1).reshape(-1)


def _moe(lidx, hb, gates, goh, wg, wu, wd):
    t = hb.shape[0]
    tm = min(MOE_TILE, t)
    bounds = _tile_bounds(goh, tm)
    n_steps = N_EXPERTS // EXPERTS_PER_STEP
    wg, wu, wd = (w.reshape(w.shape[0], n_steps, EXPERTS_PER_STEP, *w.shape[2:]) for w in (wg, wu, wd))
    upper =(jnp.arange(tm)[:, None] < jnp.arange(tm)[None, :]).astype(BF16)
    grid_spec = pltpu.PrefetchScalarGridSpec(
        num_scalar_prefetch=2,
        grid=(t // tm, n_steps),
        in_specs=[
            pl.BlockSpec((tm, D_MODEL), lambda i, e, l, b: (i, 0)),
            pl.BlockSpec((tm, LANES), lambda i, e, l, b: (i, 0)),
            pl.BlockSpec((N_EXPERTS, tm), lambda i, e, l, b: (0, i)),
            pl.BlockSpec((tm, tm), lambda i, e, l, b: (0, 0)),
            pl.BlockSpec((1, 1, EXPERTS_PER_STEP, D_MODEL, D_FF_EXPERT), lambda i, e, l, b: (l[0], e, 0, 0, 0)),
            pl.BlockSpec((1, 1, EXPERTS_PER_STEP, D_MODEL, D_FF_EXPERT), lambda i, e, l, b: (l[0], e, 0, 0, 0)),
            pl.BlockSpec((1, 1, EXPERTS_PER_STEP, D_FF_EXPERT, D_MODEL), lambda i, e, l, b: (l[0], e, 0, 0, 0)),
        ],
        out_specs=pl.BlockSpec((tm, D_MODEL), lambda i, e, l, b: (i, 0)),
        scratch_shapes=[pltpu.VMEM((tm, tm), BF16), pltpu.VMEM((tm, D_MODEL), BF16),
                        pltpu.VMEM((tm, LANES), F32), pltpu.VMEM((tm, D_MODEL), F32)],
    )
    return pl.pallas_call(
        _moe_kernel, out_shape=jax.ShapeDtypeStruct((t, D_MODEL), F32), grid_spec=grid_spec,
        compiler_params=_params(("parallel", "arbitrary")), name="moe",
    )(lidx, bounds, hb, gates, goh, upper, wg, wu, wd)


def _post_moe_kernel(lidx_ref, h_ref, f_ref, p_ref, wpg_ref, bpg_ref, wpp_ref, lng_ref, lnb_ref, x_ref):
    del lidx_ref
    h2 = _layer_norm(ALPHA * h_ref[...] + f_ref[...], lng_ref[0, 1:2], lnb_ref[0, 1:2])
    z = jnp.dot(h2.astype(BF16), wpg_ref[0], preferred_element_type=F32) + bpg_ref[0]
    proj = jnp.dot(p_ref[0].astype(BF16), wpp_ref[0], preferred_element_type=F32)
    pe = jax.nn.sigmoid(z) * proj
    x_ref[...] = _layer_norm(ALPHA * h2 + pe, lng_ref[0, 2:3], lnb_ref[0, 2:3])


def _post_moe(lidx, h, ffn, p, wpg, bpg, wpp, lng, lnb):
    t = h.shape[0]
    tm = ROW_TILE
    row = lambda i, l: (i, 0)
    lay = lambda i, l: (l[0], 0, 0)
    grid_spec = pltpu.PrefetchScalarGridSpec(
        num_scalar_prefetch=1,
        grid=(t // tm,),
        in_specs=[
            pl.BlockSpec((tm, D_MODEL), row),
            pl.BlockSpec((tm, D_MODEL), row),
            pl.BlockSpec((1, tm, PLE_DIM), lambda i, l: (l[0], i, 0)),
            pl.BlockSpec((1, D_MODEL, D_MODEL), lay),
            pl.BlockSpec((1, 1, D_MODEL), lay),
            pl.BlockSpec((1, PLE_DIM, D_MODEL), lay),
            pl.BlockSpec((1, 3, D_MODEL), lay),
            pl.BlockSpec((1, 3, D_MODEL), lay),
        ],
        out_specs=pl.BlockSpec((tm, D_MODEL), row),
    )
    return pl.pallas_call(
        _post_moe_kernel, out_shape=jax.ShapeDtypeStruct((t, D_MODEL), F32), grid_spec=grid_spec,
        compiler_params=_params(("parallel",)), name="post_moe",
    )(lidx, h, ffn, p, wpg, bpg, wpp, lng, lnb)


def _rope_tables_t(s):
    t = jnp.arange(s, dtype=jnp.int32)
    row = (t // GRID_W).astype(F32)
    col = (t % GRID_W).astype(F32)
    n_freq = HEAD_DIM // 4
    inv_freq = ROPE_THETA ** (-jnp.arange(n_freq, dtype=F32) / n_freq)
    ang = jnp.concatenate([inv_freq[:, None] * row[None, :], inv_freq[:, None] * col[None, :]], axis=0)
    return jnp.cos(ang), jnp.sin(ang)


def _prepare_weights(w_in, q_norm_a, k_norm_a, rpb_b, g_out_a, g_out_b, w_out, w_router, b_router,
                     w_gate, w_up, w_down, w_ple_gate, b_ple_gate, w_ple_proj):
    n_layers = w_in.shape[0]
    qa, ka, va, qn, kn, vn = jnp.split(w_in, [512, 640, 768, 1280, 1792], axis=-1)
    wt = jnp.swapaxes(jnp.concatenate([qa, ka, va], axis=-1), 1, 2).astype(BF16)
    wn = jnp.concatenate([qn, kn, vn], axis=-1).astype(BF16)
    wr_t = w_router.T
    wr_hi = wr_t.astype(BF16)
    wr_lo = (wr_t - wr_hi.astype(F32)).astype(BF16)
    return dict(
        wt=wt, wn=wn,
        qg=q_norm_a.reshape(n_layers, HEAD_DIM, 1), kg=k_norm_a.reshape(n_layers, HEAD_DIM, 1),
        bias=_nattn_bias_tables(rpb_b),
        ga=g_out_a.reshape(n_layers, 1, A_WIDTH), gb=g_out_b.reshape(n_layers, 1, B_WIDTH),
        wo=w_out.astype(BF16), wr_hi=wr_hi, wr_lo=wr_lo, br=b_router.reshape(N_EXPERTS, 1),
        wg=w_gate.astype(BF16), wu=w_up.astype(BF16), wd=w_down.astype(BF16),
        wpg=w_ple_gate.astype(BF16), bpg=b_ple_gate.reshape(n_layers, 1, D_MODEL), wpp=w_ple_proj.astype(BF16),
    )


def _layer(lidx, x, p_flat, w, ln_g, ln_b, rope):
    nb, s, _ = x.shape
    qz, ka, vat, qn, kn, vn, qnorm, knorm = _inproj(lidx, x, w["wt"], w["wn"], rope[0], rope[1], w["qg"], w["kg"])
    oa = _gattn(qz, ka, vat, qnorm, knorm)
    ob = _nattn(lidx, qn, kn, vn, w["bias"])
    t = nb * s
    h, hb, gates, goh = _post_attn(lidx, oa.reshape(t, A_WIDTH), ob.reshape(t, B_WIDTH), x.reshape(t, D_MODEL),
                              w["wo"], w["ga"], w["gb"], ln_g, ln_b, w["wr_hi"], w["wr_lo"], w["br"])
    ffn = _moe(lidx, hb, gates, goh, w["wg"], w["wu"], w["wd"])
    xo = _post_moe(lidx, h, ffn, p_flat, w["wpg"], w["bpg"], w["wpp"], ln_g, ln_b)
    return xo.reshape(nb, s, D_MODEL)


def kernel(x_prompt, x_sample, p_prompt, p_sample, w_in, q_norm_a, k_norm_a, rpb_b, g_out_a, g_out_b, w_out, ln_g, ln_b, w_router, b_router, w_gate, w_up, w_down, w_ple_gate, b_ple_gate, w_ple_proj):
    n_layers = w_in.shape[0]
    w = _prepare_weights(w_in, q_norm_a, k_norm_a, rpb_b, g_out_a, g_out_b, w_out, w_router, b_router,
                         w_gate, w_up, w_down, w_ple_gate, b_ple_gate, w_ple_proj)
    trunks = []
    for x, p in ((x_prompt, p_prompt), (x_sample, p_sample)):
        nb, s, _ = x.shape
        trunks.append((x, p.reshape(n_layers, nb * s, PLE_DIM), _rope_tables_t(s)))

    def body(xs, layer):
        lidx = jnp.reshape(layer, (1,)).astype(jnp.int32)
        new = tuple(_layer(lidx, x, trunks[i][1], w, ln_g, ln_b, trunks[i][2]) for i, x in enumerate(xs))
        return new, None

    (y_prompt, y_sample), _ = lax.scan(body, (trunks[0][0], trunks[1][0]), jnp.arange(n_layers, dtype=jnp.int32))
    return (y_prompt, y_sample)
```

```python
import functools
import math

import jax
import jax.numpy as jnp
import numpy as np
from jax import lax
from jax.experimental import pallas as pl
from jax.experimental.pallas import tpu as pltpu

F32 = jnp.float32
BF16 = jnp.bfloat16

D_MODEL = 1024
N_LAYERS = 4
HEAD_DIM = 64
A_HEADS = 8
A_KV_HEADS = 2
A_GROUP = A_HEADS // A_KV_HEADS
B_HEADS = 8
A_WIDTH = A_HEADS * HEAD_DIM
KV_WIDTH = A_KV_HEADS * HEAD_DIM
B_WIDTH = B_HEADS * HEAD_DIM
GRID_W = 64
QBLK = 128
WIN_R = 8
WIN_C = 16
ROPE_THETA = 10000.0
N_EXPERTS = 16
N_GROUPS = 4
EXPERTS_PER_GROUP = N_EXPERTS // N_GROUPS
D_FF_EXPERT = 512
PLE_DIM = 256
LN_EPS = 1e-5
QK_EPS = 1e-6
NEG_INF = -1e30
ALPHA = (2 * N_LAYERS) ** 0.25
Q_SCALE = HEAD_DIM ** -0.5
LOG2_E = math.log2(math.e)

LANES = 128

TOKEN_TILE = 512
ATTN_Q_TILE = 512
ROW_TILE = 512
POST_ROWS = 256
MOE_TILE = 1024
MOE_SUB = 128
EXPERTS_PER_STEP = 4
BAND_ROWS = 10
BAND_BLKS = BAND_ROWS * GRID_W // QBLK
N_BIAS_VARIANTS = 5
NATTN_AHEAD = 3
GATTN_AHEAD = 1
VMEM_LIMIT = 56 * 1024 * 1024

_QA0, _KA0, _VA0, _PT_ROWS = 0, 512, 640, 768
_QN0, _KN0, _VN0, _NAT_COLS = 0, 512, 1024, 1536
ONES_ROWS = 16
VT_ROWS = HEAD_DIM + ONES_ROWS
SHIFT_BOUND_MAX = 40.0


def _params(sem):
    return pltpu.CompilerParams(dimension_semantics=sem, vmem_limit_bytes=VMEM_LIMIT)


def _layer_norm(y, g, b):
    mu = jnp.mean(y, axis=-1, keepdims=True)
    d = y - mu
    var = jnp.mean(d * d, axis=-1, keepdims=True)
    return d * lax.rsqrt(var + LN_EPS) * g + b


def _rope_t(y, cos, sin):
    cr, cc = cos[0:16], cos[16:32]
    sr, sc = sin[0:16], sin[16:32]
    a1, a2, b1, b2 = y[0:16], y[16:32], y[32:48], y[48:64]
    return jnp.concatenate([a1 * cr - a2 * sr, a1 * sr + a2 * cr,
                            b1 * cc - b2 * sc, b1 * sc + b2 * cc], axis=0)


def _norm_rope_t(y, gain, cos, sin):
    ms = jnp.sum(y * y, axis=0, keepdims=True) * (1.0 / HEAD_DIM)
    return _rope_t(y * lax.rsqrt(ms + QK_EPS) * gain, cos, sin)


def _inproj_kernel(lidx_ref, x_ref, wt_ref, wn_ref, cos_ref, sin_ref, qg_ref, kg_ref,
                   qz_ref, ka_ref, vat_ref, qn_ref, kn_ref, vn_ref, qnorm_ref, knorm_ref):
    del lidx_ref
    xb = x_ref[0].astype(BF16)
    pt = lax.dot_general(wt_ref[0], xb, (((1,), (1,)), ((), ())),
                         preferred_element_type=F32)
    tm = xb.shape[0]
    cos = cos_ref[...]
    sin = sin_ref[...]
    qg = qg_ref[0] * (Q_SCALE * LOG2_E)
    kg = kg_ref[0]
    zeros = jnp.zeros((HEAD_DIM, tm), BF16)
    head_row = lax.broadcasted_iota(jnp.int32, (A_HEADS, 1), 0)
    l2 = lambda y: jnp.sqrt(jnp.sum(y * y, axis=0, keepdims=True))
    qnorm = jnp.zeros((A_HEADS, tm), F32)
    for h in range(A_HEADS):
        qf = _norm_rope_t(pt[_QA0 + h * HEAD_DIM:_QA0 + (h + 1) * HEAD_DIM], qg, cos, sin)
        qnorm = jnp.where(head_row == h, l2(qf), qnorm)
        q = qf.astype(BF16)
        if h // A_GROUP == 0:
            qz_ref[0, h] = jnp.concatenate([q, zeros], axis=0)
        else:
            qz_ref[0, h] = jnp.concatenate([zeros, q], axis=0)
    qnorm_ref[0] = qnorm
    ks = [_norm_rope_t(pt[_KA0 + g * HEAD_DIM:_KA0 + (g + 1) * HEAD_DIM], kg, cos, sin) for g in range(A_KV_HEADS)]
    knorm = jnp.zeros((A_HEADS, tm), F32)
    for g in range(A_KV_HEADS):
        knorm = jnp.where(head_row == g, l2(ks[g]), knorm)
    knorm_ref[0, 0] = knorm
    kt = jnp.concatenate(ks, axis=0)
    ka_ref[0, 0] = kt.T.astype(BF16)
    ones = jnp.ones((ONES_ROWS, tm), BF16)
    for g in range(A_KV_HEADS):
        v = pt[_VA0 + g * HEAD_DIM:_VA0 + (g + 1) * HEAD_DIM].astype(BF16)
        vat_ref[0, 0, g] = jnp.concatenate([v, ones], axis=0)
    nat = jnp.dot(xb, wn_ref[0], preferred_element_type=F32)
    qn_ref[0] = (nat[:, _QN0:_QN0 + B_WIDTH] * Q_SCALE).astype(BF16)
    kn_ref[0] = nat[:, _KN0:_KN0 + B_WIDTH].astype(BF16).reshape(tm // QBLK, QBLK, B_WIDTH)
    vn_ref[0] = nat[:, _VN0:_VN0 + B_WIDTH].astype(BF16).reshape(tm // QBLK, QBLK, B_WIDTH)


def _inproj(lidx, x, wt, wn, cos_t, sin_t, qg, kg):
    nb, s, _ = x.shape
    tm = TOKEN_TILE
    nc = s // tm
    out_shape = (
        jax.ShapeDtypeStruct((nb, A_HEADS, 2 * HEAD_DIM, s), BF16),
        jax.ShapeDtypeStruct((nb, nc, tm, KV_WIDTH), BF16),
        jax.ShapeDtypeStruct((nb, nc, A_KV_HEADS, VT_ROWS, tm), BF16),
        jax.ShapeDtypeStruct((nb, s, B_WIDTH), BF16),
        jax.ShapeDtypeStruct((nb, s // QBLK, QBLK, B_WIDTH), BF16),
        jax.ShapeDtypeStruct((nb, s // QBLK, QBLK, B_WIDTH), BF16),
        jax.ShapeDtypeStruct((nb, A_HEADS, s), F32),
        jax.ShapeDtypeStruct((nb, nc, A_HEADS, tm), F32),
    )
    grid_spec = pltpu.PrefetchScalarGridSpec(
        num_scalar_prefetch=1,
        grid=(nb, nc),
        in_specs=[
            pl.BlockSpec((1, tm, D_MODEL), lambda b, c, l: (b, c, 0)),
            pl.BlockSpec((1, _PT_ROWS, D_MODEL), lambda b, c, l: (l[0], 0, 0)),
            pl.BlockSpec((1, D_MODEL, _NAT_COLS), lambda b, c, l: (l[0], 0, 0)),
            pl.BlockSpec((HEAD_DIM // 2, tm), lambda b, c, l: (0, c)),
            pl.BlockSpec((HEAD_DIM // 2, tm), lambda b, c, l: (0, c)),
            pl.BlockSpec((1, HEAD_DIM, 1), lambda b, c, l: (l[0], 0, 0)),
            pl.BlockSpec((1, HEAD_DIM, 1), lambda b, c, l: (l[0], 0, 0)),
        ],
        out_specs=[
            pl.BlockSpec((1, A_HEADS, 2 * HEAD_DIM, tm), lambda b, c, l: (b, 0, 0, c)),
            pl.BlockSpec((1, 1, tm, KV_WIDTH), lambda b, c, l: (b, c, 0, 0)),
            pl.BlockSpec((1, 1, A_KV_HEADS, VT_ROWS, tm), lambda b, c, l: (b, c, 0, 0, 0)),
            pl.BlockSpec((1, tm, B_WIDTH), lambda b, c, l: (b, c, 0)),
            pl.BlockSpec((1, tm // QBLK, QBLK, B_WIDTH), lambda b, c, l: (b, c, 0, 0)),
            pl.BlockSpec((1, tm // QBLK, QBLK, B_WIDTH), lambda b, c, l: (b, c, 0, 0)),
            pl.BlockSpec((1, A_HEADS, tm), lambda b, c, l: (b, 0, c)),
            pl.BlockSpec((1, 1, A_HEADS, tm), lambda b, c, l: (b, c, 0, 0)),
        ],
    )
    return pl.pallas_call(
        _inproj_kernel, out_shape=out_shape, grid_spec=grid_spec,
        compiler_params=_params(("parallel", "parallel")), name="inproj",
    )(lidx, x, wt, wn, cos_t, sin_t, qg, kg)


def _col_max(st):
    slabs = [st[r:r + 128] for r in range(0, st.shape[0], 128)]
    while len(slabs) > 1:
        slabs = [jnp.maximum(slabs[i], slabs[i + 1]) for i in range(0, len(slabs), 2)]
    return jnp.max(slabs[0], axis=0, keepdims=True)


def _gattn_kernel(qz_ref, k_ref, vt_ref, qn_ref, kn_ref, o_ref, *scratch):
    m_refs, acc_refs = scratch[:A_HEADS], scratch[A_HEADS:]
    n_chunks = k_ref.shape[1]
    for h in range(A_HEADS):
        acc_refs[h][...] = jnp.zeros(acc_refs[h].shape, F32)

    k_norm_max = jnp.max(jnp.max(kn_ref[0], axis=0), axis=1, keepdims=True)
    row = lax.broadcasted_iota(jnp.int32, (A_HEADS, 1), 0)
    k_of_head = jnp.where(row < A_GROUP, k_norm_max[0:1], k_norm_max[1:2])
    bound = qn_ref[0] * k_of_head
    bounded = jnp.max(bound) <= SHIFT_BOUND_MAX

    def chunk_body(c, carry, online):
        kc = k_ref[0, c]
        scores = lambda h: jnp.dot(kc, qz_ref[0, h], preferred_element_type=F32)
        pending = [scores(h) for h in range(GATTN_AHEAD)]
        for h in range(A_HEADS):
            st = pending.pop(0)
            if h + GATTN_AHEAD < A_HEADS:
                pending.append(scores(h + GATTN_AHEAD))
            vt = vt_ref[0, c, h // A_GROUP]
            if online:
                m_old = m_refs[h][...]
                m_new = jnp.maximum(m_old, _col_max(st))
                alpha = jnp.exp2(m_old - m_new)
                p = jnp.exp2(st - m_new).astype(BF16)
                acc_refs[h][...] = alpha * acc_refs[h][...] + jnp.dot(vt, p, preferred_element_type=F32)
                m_refs[h][...] = m_new
            else:
                p = jnp.exp2(st - m_refs[h][...]).astype(BF16)
                acc_refs[h][...] += jnp.dot(vt, p, preferred_element_type=F32)
        return carry

    @pl.when(bounded)
    def _():
        for h in range(A_HEADS):
            m_refs[h][...] = bound[h:h + 1]
        lax.fori_loop(0, n_chunks, functools.partial(chunk_body, online=False), 0)

    @pl.when(jnp.logical_not(bounded))
    def _():
        for h in range(A_HEADS):
            m_refs[h][...] = jnp.full(m_refs[h].shape, -jnp.inf, F32)
        lax.fori_loop(0, n_chunks, functools.partial(chunk_body, online=True), 0)

    outs = []
    for h in range(A_HEADS):
        acc = acc_refs[h][...]
        outs.append(acc[0:HEAD_DIM] * (1.0 / acc[HEAD_DIM:HEAD_DIM + 1]))
    o_ref[0] = jnp.concatenate(outs, axis=0).T


def _gattn(qz, ka, vat, qnorm, knorm):
    nb, _, _, s = qz.shape
    nc, tk = ka.shape[1], ka.shape[2]
    tq = ATTN_Q_TILE
    return pl.pallas_call(
        _gattn_kernel,
        out_shape=jax.ShapeDtypeStruct((nb, s, A_WIDTH), F32),
        grid=(nb, s // tq),
        in_specs=[
            pl.BlockSpec((1, A_HEADS, 2 * HEAD_DIM, tq), lambda b, i: (b, 0, 0, i)),
            pl.BlockSpec((1, nc, tk, KV_WIDTH), lambda b, i: (b, 0, 0, 0)),
            pl.BlockSpec((1, nc, A_KV_HEADS, VT_ROWS, tk), lambda b, i: (b, 0, 0, 0, 0)),
            pl.BlockSpec((1, A_HEADS, tq), lambda b, i: (b, 0, i)),
            pl.BlockSpec((1, nc, A_HEADS, tk), lambda b, i: (b, 0, 0, 0)),
        ],
        out_specs=pl.BlockSpec((1, tq, A_WIDTH), lambda b, i: (b, i, 0)),
        scratch_shapes=[pltpu.VMEM((1, tq), F32)] * A_HEADS + [pltpu.VMEM((VT_ROWS, tq), F32)] * A_HEADS,
        compiler_params=_params(("parallel", "parallel")), name="gattn",
    )(qz, ka, vat, qnorm, knorm)


def _nattn_kernel(*refs):
    q_ref = refs[0]
    k_refs = refs[1:1 + BAND_BLKS]
    v_refs = refs[1 + BAND_BLKS:1 + 2 * BAND_BLKS]
    bias_ref = refs[1 + 2 * BAND_BLKS]
    o_ref = refs[2 + 2 * BAND_BLKS]
    first_head = lax.broadcasted_iota(jnp.int32, (QBLK, LANES), 1) < HEAD_DIM
    dn = (((1,), (1,)), ((), ()))
    def band(refs, c):
        return jnp.concatenate([refs[t][0, 0, :, c * LANES:(c + 1) * LANES] for t in range(BAND_BLKS)], axis=0)

    def scores(h):
        c, hh = h // 2, h % 2
        qp = q_ref[0, :, c * LANES:(c + 1) * LANES]
        qm = jnp.where(first_head if hh == 0 else jnp.logical_not(first_head), qp, jnp.zeros_like(qp))
        return lax.dot_general(qm, band(k_refs, c), dn, preferred_element_type=F32) + bias_ref[0, h]

    outs = []
    pending = [scores(h) for h in range(NATTN_AHEAD)]
    for h in range(B_HEADS):
        s = pending.pop(0)
        if h + NATTN_AHEAD < B_HEADS:
            pending.append(scores(h + NATTN_AHEAD))
        m = jnp.max(s, axis=-1, keepdims=True)
        p = jnp.exp(s - m)
        l = jnp.sum(p, axis=-1, keepdims=True)
        outs.append(jnp.dot(p.astype(BF16), band(v_refs, h // 2), preferred_element_type=F32) * (1.0 / l))
    pairs = [jnp.where(first_head, outs[2 * c], outs[2 * c + 1]) for c in range(B_HEADS // 2)]
    o_ref[0] = jnp.concatenate(pairs, axis=-1)


def _band_start_blk(j, nblk):
    rows_per_blk = QBLK // GRID_W
    rows = nblk * rows_per_blk
    bs = jnp.clip(j * rows_per_blk - WIN_R // 2, 0, rows - BAND_ROWS)
    return bs // rows_per_blk


def _bias_variant(j, nblk):
    return jnp.where(j < 2, j, jnp.where(j >= nblk - 2, j - nblk + N_BIAS_VARIANTS, 2))


def _nattn(lidx, qn, kn, vn, bias):
    nb, s, _ = qn.shape
    nblk = s // QBLK
    band_specs = [pl.BlockSpec((1, 1, QBLK, B_WIDTH),
                               functools.partial(lambda b, j, l, t: (b, _band_start_blk(j, nblk) + t, 0, 0), t=t))
                  for t in range(BAND_BLKS)]
    grid_spec = pltpu.PrefetchScalarGridSpec(
        num_scalar_prefetch=1,
        grid=(nb, nblk),
        in_specs=[pl.BlockSpec((1, QBLK, B_WIDTH), lambda b, j, l: (b, j, 0))]
        + band_specs + band_specs
        + [pl.BlockSpec((1, B_HEADS, QBLK, BAND_ROWS * GRID_W),
                        lambda b, j, l: (l[0] * N_BIAS_VARIANTS + _bias_variant(j, nblk), 0, 0, 0))],
        out_specs=pl.BlockSpec((1, QBLK, B_WIDTH), lambda b, j, l: (b, j, 0)),
    )

    def kernel(lidx_ref, *refs):
        del lidx_ref
        _nattn_kernel(*refs)

    return pl.pallas_call(
        kernel, out_shape=jax.ShapeDtypeStruct((nb, s, B_WIDTH), F32), grid_spec=grid_spec,
        compiler_params=_params(("parallel", "arbitrary")), name="nattn",
    )(lidx, qn, *([kn] * BAND_BLKS), *([vn] * BAND_BLKS), bias)


def _nattn_bias_tables(rpb):
    n_layers = rpb.shape[0]
    n_r, n_c = 2 * WIN_R - 1, 2 * WIN_C - 1
    qc = np.arange(GRID_W)[:, None]
    kc = np.arange(GRID_W)[None, :]
    ic = np.clip(kc - qc + WIN_C - 1, 0, n_c - 1)
    cs = np.clip(qc - WIN_C // 2, 0, GRID_W - WIN_C)
    col_ok = (kc >= cs) & (kc < cs + WIN_C)
    onehot = jnp.asarray((ic.reshape(-1)[None, :] == np.arange(n_c)[:, None]).astype(np.float32))
    cols = jnp.dot(rpb.astype(F32).reshape(-1, n_c), onehot, precision=lax.Precision.HIGHEST)
    cols = cols.reshape(n_layers, B_HEADS, n_r, GRID_W, GRID_W)
    rows_per_blk = QBLK // GRID_W
    rows = 16
    variants = []
    for r0 in (0, 2, 4, rows - 4, rows - 2):
        bs = min(max(r0 - WIN_R // 2, 0), rows - BAND_ROWS)
        q_rows = []
        for qo in range(rows_per_blk):
            qr = r0 + qo
            rs = min(max(qr - WIN_R // 2, 0), rows - WIN_R)
            k_rows = []
            for ko in range(BAND_ROWS):
                kr = bs + ko
                ir = min(max(kr - qr + WIN_R - 1, 0), n_r - 1)
                if rs <= kr < rs + WIN_R:
                    k_rows.append(jnp.where(jnp.asarray(col_ok)[None, None], cols[:, :, ir], NEG_INF))
                else:
                    k_rows.append(jnp.full((n_layers, B_HEADS, GRID_W, GRID_W), NEG_INF, F32))
            q_rows.append(jnp.concatenate(k_rows, axis=-1))
        variants.append(jnp.concatenate(q_rows, axis=-2))
    t = jnp.stack(variants, axis=1)
    return t.reshape(n_layers * N_BIAS_VARIANTS, B_HEADS, QBLK, BAND_ROWS * GRID_W)


def _group_member(x, d, wrapped):
    return jnp.where(wrapped, pltpu.roll(x, EXPERTS_PER_GROUP - d, 0), pltpu.roll(x, N_EXPERTS - d, 0))


def _route_t(logits_t):
    mx = jnp.max(logits_t, axis=0, keepdims=True)
    ex = jnp.exp(logits_t - mx)
    probs = ex / jnp.sum(ex, axis=0, keepdims=True)
    e_idx = lax.broadcasted_iota(jnp.int32, probs.shape, 0)
    pos = e_idx % EXPERTS_PER_GROUP
    rank = jnp.zeros(probs.shape, jnp.int32)
    for d in range(1, EXPERTS_PER_GROUP):
        wrapped = pos + d >= EXPERTS_PER_GROUP
        other = _group_member(probs, d, wrapped)
        beats = (other > probs) | ((other == probs) & wrapped)
        rank = rank + beats.astype(jnp.int32)
    top = rank < 2
    kept = jnp.where(top, probs, 0.0)
    score = kept
    for d in range(1, EXPERTS_PER_GROUP):
        score = score + _group_member(kept, d, pos + d >= EXPERTS_PER_GROUP)
    grp = e_idx // EXPERTS_PER_GROUP
    lost = jnp.zeros(probs.shape, jnp.bool_)
    for d in range(1, N_GROUPS):
        other = pltpu.roll(score, N_EXPERTS - d * EXPERTS_PER_GROUP, 0)
        earlier = grp + d >= N_GROUPS
        lost = lost | (other > score) | ((other == score) & earlier)
    won = jnp.logical_not(lost)
    gates = jnp.where(top & won, probs / score, 0.0)
    group_onehot = jnp.where(won & (pos == 0), 1.0, 0.0)
    return gates, group_onehot


def _post_attn_kernel(lidx_ref, oa_ref, ob_ref, x_ref, wo_ref, ga_ref, gb_ref, lng_ref, lnb_ref,
                      wrh_ref, wrl_ref, br_ref, h_ref, hb_ref, gates_ref, goh_ref):
    del lidx_ref
    tm = x_ref.shape[0]
    n_part = tm // POST_ROWS
    parts = [pl.ds(r * POST_ROWS, POST_ROWS) for r in range(n_part)]
    dn = (((1,), (1,)), ((), ()))

    def rms(ref, g_ref, rows):
        a = ref[rows, :]
        return a * lax.rsqrt(jnp.mean(a * a, axis=-1, keepdims=True) + LN_EPS) * g_ref[0]

    cats = [jnp.concatenate([rms(oa_ref, ga_ref, rows), rms(ob_ref, gb_ref, rows)], axis=-1).astype(BF16)
            for rows in parts]
    mixes = [jnp.dot(cat, wo_ref[0], preferred_element_type=F32) for cat in cats]
    hs = [_layer_norm(ALPHA * x_ref[rows, :] + mix, lng_ref[0, 0:1], lnb_ref[0, 0:1]) for rows, mix in zip(parts, mixes)]
    for rows, h in zip(parts, hs):
        h_ref[rows, :] = h
        h_hi = h.astype(BF16)
        hb_ref[rows, :] = h_hi
        h_lo = (h - h_hi.astype(F32)).astype(BF16)
        logits_t = (lax.dot_general(wrh_ref[...], h_hi, dn, preferred_element_type=F32)
                    + lax.dot_general(wrh_ref[...], h_lo, dn, preferred_element_type=F32)
                    + lax.dot_general(wrl_ref[...], h_hi, dn, preferred_element_type=F32)) + br_ref[...]
        gates_t, group_onehot = _route_t(logits_t)
        goh_ref[:, rows] = group_onehot.astype(BF16)
        gates_ref[:, rows] = gates_t


def _post_attn(lidx, oa, ob, x, wo, ga, gb, lng, lnb, wrh, wrl, br):
    t = x.shape[0]
    tm = ROW_TILE
    row = lambda i, l: (i, 0)
    lay = lambda i, l: (l[0], 0, 0)
    full = lambda i, l: (0, 0)
    grid_spec = pltpu.PrefetchScalarGridSpec(
        num_scalar_prefetch=1,
        grid=(t // tm,),
        in_specs=[
            pl.BlockSpec((tm, A_WIDTH), row),
            pl.BlockSpec((tm, B_WIDTH), row),
            pl.BlockSpec((tm, D_MODEL), row),
            pl.BlockSpec((1, D_MODEL, D_MODEL), lay),
            pl.BlockSpec((1, 1, A_WIDTH), lay),
            pl.BlockSpec((1, 1, B_WIDTH), lay),
            pl.BlockSpec((1, 3, D_MODEL), lay),
            pl.BlockSpec((1, 3, D_MODEL), lay),
            pl.BlockSpec((N_EXPERTS, D_MODEL), full),
            pl.BlockSpec((N_EXPERTS, D_MODEL), full),
            pl.BlockSpec((N_EXPERTS, 1), full),
        ],
        out_specs=[
            pl.BlockSpec((tm, D_MODEL), row),
            pl.BlockSpec((tm, D_MODEL), row),
            pl.BlockSpec((N_EXPERTS, tm), lambda i, l: (0, i)),
            pl.BlockSpec((N_EXPERTS, tm), lambda i, l: (0, i)),
        ],
    )
    return pl.pallas_call(
        _post_attn_kernel,
        out_shape=(jax.ShapeDtypeStruct((t, D_MODEL), F32),
                   jax.ShapeDtypeStruct((t, D_MODEL), BF16),
                   jax.ShapeDtypeStruct((N_EXPERTS, t), F32),
                   jax.ShapeDtypeStruct((N_EXPERTS, t), BF16)),
        grid_spec=grid_spec,
        compiler_params=_params(("parallel",)), name="post_attn",
    )(lidx, oa, ob, x, wo, ga, gb, lng, lnb, wrh, wrl, br)


def _moe_kernel(lidx_ref, bounds_ref, hb_ref, gates_ref, goh_ref, upper_ref, wg_ref, wu_ref, wd_ref, o_ref,
                perm_ref, xs_ref, gs_ref, ys_ref):
    del lidx_ref
    i = pl.program_id(0)
    e = pl.program_id(1)
    tm = hb_ref.shape[0]
    base = i * (2 * N_GROUPS)

    @pl.when(e == 0)
    def _sort():
        goh = goh_ref[...]
        ranks = jnp.dot(goh, upper_ref[...], preferred_element_type=F32)
        row = lax.broadcasted_iota(jnp.int32, (N_EXPERTS, 1), 0)
        start = jnp.zeros((N_EXPERTS, 1), F32)
        for g in range(N_GROUPS):
            start = jnp.where(row == g * EXPERTS_PER_GROUP, bounds_ref[base + g].astype(F32), start)
        pos = jnp.sum(goh.astype(F32) * (ranks + start), axis=0, keepdims=True)
        slot = lax.broadcasted_iota(jnp.int32, (tm, tm), 0)
        perm = jnp.where(slot == pos.astype(jnp.int32), 1.0, 0.0).astype(BF16)
        perm_ref[...] = perm
        xs_ref[...] = jnp.dot(perm, hb_ref[...], preferred_element_type=F32).astype(BF16)
        gates_t = gates_ref[...]
        g_hi = gates_t.astype(BF16)
        g_lo = (gates_t - g_hi.astype(F32)).astype(BF16)
        both = lax.dot_general(jnp.concatenate([g_hi, g_lo], axis=0), perm, (((1,), (1,)), ((), ())),
                               preferred_element_type=F32)
        sorted_t = both[0:N_EXPERTS] + both[N_EXPERTS:2 * N_EXPERTS]
        gs_ref[...] = jnp.concatenate([sorted_t, jnp.zeros((LANES - N_EXPERTS, tm), F32)], axis=0).T
        ys_ref[...] = jnp.zeros(ys_ref.shape, F32)

    grp = (e * EXPERTS_PER_STEP) // EXPERTS_PER_GROUP
    first = bounds_ref[base + grp] // MOE_SUB
    stop = (bounds_ref[base + N_GROUPS + grp] + MOE_SUB - 1) // MOE_SUB
    def experts(row0, n_rows):
        rows = pl.ds(pl.multiple_of(row0, MOE_SUB), n_rows)
        lane = lax.broadcasted_iota(jnp.int32, (n_rows, LANES), 1)
        x = xs_ref[rows, :]
        gs = gs_ref[rows, :]
        total = None
        for j in range(EXPERTS_PER_STEP):
            gate = jnp.dot(x, wg_ref[0, 0, j], preferred_element_type=F32)
            up = jnp.dot(x, wu_ref[0, 0, j], preferred_element_type=F32)
            act = (gate * jax.nn.sigmoid(gate) * up).astype(BF16)
            y = jnp.dot(act, wd_ref[0, 0, j], preferred_element_type=F32)
            g_col = jnp.sum(jnp.where(lane == e * EXPERTS_PER_STEP + j, gs, 0.0), axis=1, keepdims=True)
            total = g_col * y if total is None else total + g_col * y
        ys_ref[rows, :] += total

    n_pairs = (stop - first) // 2

    def pair_body(s, carry):
        experts((first + 2 * s) * MOE_SUB, 2 * MOE_SUB)
        return carry

    lax.fori_loop(0, n_pairs, pair_body, 0)

    @pl.when(first + 2 * n_pairs < stop)
    def _():
        experts((stop - 1) * MOE_SUB, MOE_SUB)

    @pl.when(e == N_EXPERTS // EXPERTS_PER_STEP - 1)
    def _unsort():
        o_ref[...] = lax.dot_general(perm_ref[...], ys_ref[...].astype(BF16), (((0,), (0,)), ((), ())),
                                     preferred_element_type=F32)


def _tile_bounds(goh, tm):
    nt = goh.shape[1] // tm
    cnt = goh[::EXPERTS_PER_GROUP].astype(F32).reshape(N_GROUPS, nt, tm).sum(-1).T.astype(jnp.int32)
    end = jnp.cumsum(cnt, axis=1)
    return jnp.concatenate([end - cnt, end], axis=1).reshape(-1)


def _moe(lidx, hb, gates, goh, wg, wu, wd):
    t = hb.shape[0]
    tm = min(MOE_TILE, t)
    bounds = _tile_bounds(goh, tm)
    n_steps = N_EXPERTS // EXPERTS_PER_STEP
    wg, wu, wd = (w.reshape(w.shape[0], n_steps, EXPERTS_PER_STEP, *w.shape[2:]) for w in (wg, wu, wd))
    upper =(jnp.arange(tm)[:, None] < jnp.arange(tm)[None, :]).astype(BF16)
    grid_spec = pltpu.PrefetchScalarGridSpec(
        num_scalar_prefetch=2,
        grid=(t // tm, n_steps),
        in_specs=[
            pl.BlockSpec((tm, D_MODEL), lambda i, e, l, b: (i, 0)),
            pl.BlockSpec((N_EXPERTS, tm), lambda i, e, l, b: (0, i)),
            pl.BlockSpec((N_EXPERTS, tm), lambda i, e, l, b: (0, i)),
            pl.BlockSpec((tm, tm), lambda i, e, l, b: (0, 0)),
            pl.BlockSpec((1, 1, EXPERTS_PER_STEP, D_MODEL, D_FF_EXPERT), lambda i, e, l, b: (l[0], e, 0, 0, 0)),
            pl.BlockSpec((1, 1, EXPERTS_PER_STEP, D_MODEL, D_FF_EXPERT), lambda i, e, l, b: (l[0], e, 0, 0, 0)),
            pl.BlockSpec((1, 1, EXPERTS_PER_STEP, D_FF_EXPERT, D_MODEL), lambda i, e, l, b: (l[0], e, 0, 0, 0)),
        ],
        out_specs=pl.BlockSpec((tm, D_MODEL), lambda i, e, l, b: (i, 0)),
        scratch_shapes=[pltpu.VMEM((tm, tm), BF16), pltpu.VMEM((tm, D_MODEL), BF16),
                        pltpu.VMEM((tm, LANES), F32), pltpu.VMEM((tm, D_MODEL), F32)],
    )
    return pl.pallas_call(
        _moe_kernel, out_shape=jax.ShapeDtypeStruct((t, D_MODEL), F32), grid_spec=grid_spec,
        compiler_params=_params(("parallel", "arbitrary")), name="moe",
    )(lidx, bounds, hb, gates, goh, upper, wg, wu, wd)


def _post_moe_kernel(lidx_ref, h_ref, f_ref, p_ref, wpg_ref, bpg_ref, wpp_ref, lng_ref, lnb_ref, x_ref):
    del lidx_ref
    h2 = _layer_norm(ALPHA * h_ref[...] + f_ref[...], lng_ref[0, 1:2], lnb_ref[0, 1:2])
    z = jnp.dot(h2.astype(BF16), wpg_ref[0], preferred_element_type=F32) + bpg_ref[0]
    proj = jnp.dot(p_ref[0].astype(BF16), wpp_ref[0], preferred_element_type=F32)
    pe = jax.nn.sigmoid(z) * proj
    x_ref[...] = _layer_norm(ALPHA * h2 + pe, lng_ref[0, 2:3], lnb_ref[0, 2:3])


def _post_moe(lidx, h, ffn, p, wpg, bpg, wpp, lng, lnb):
    t = h.shape[0]
    tm = ROW_TILE
    row = lambda i, l: (i, 0)
    lay = lambda i, l: (l[0], 0, 0)
    grid_spec = pltpu.PrefetchScalarGridSpec(
        num_scalar_prefetch=1,
        grid=(t // tm,),
        in_specs=[
            pl.BlockSpec((tm, D_MODEL), row),
            pl.BlockSpec((tm, D_MODEL), row),
            pl.BlockSpec((1, tm, PLE_DIM), lambda i, l: (l[0], i, 0)),
            pl.BlockSpec((1, D_MODEL, D_MODEL), lay),
            pl.BlockSpec((1, 1, D_MODEL), lay),
            pl.BlockSpec((1, PLE_DIM, D_MODEL), lay),
            pl.BlockSpec((1, 3, D_MODEL), lay),
            pl.BlockSpec((1, 3, D_MODEL), lay),
        ],
        out_specs=pl.BlockSpec((tm, D_MODEL), row),
    )
    return pl.pallas_call(
        _post_moe_kernel, out_shape=jax.ShapeDtypeStruct((t, D_MODEL), F32), grid_spec=grid_spec,
        compiler_params=_params(("parallel",)), name="post_moe",
    )(lidx, h, ffn, p, wpg, bpg, wpp, lng, lnb)


def _rope_tables_t(s):
    t = jnp.arange(s, dtype=jnp.int32)
    row = (t // GRID_W).astype(F32)
    col = (t % GRID_W).astype(F32)
    n_freq = HEAD_DIM // 4
    inv_freq = ROPE_THETA ** (-jnp.arange(n_freq, dtype=F32) / n_freq)
    ang = jnp.concatenate([inv_freq[:, None] * row[None, :], inv_freq[:, None] * col[None, :]], axis=0)
    return jnp.cos(ang), jnp.sin(ang)


def _prepare_weights(w_in, q_norm_a, k_norm_a, rpb_b, g_out_a, g_out_b, w_out, w_router, b_router,
                     w_gate, w_up, w_down, w_ple_gate, b_ple_gate, w_ple_proj):
    n_layers = w_in.shape[0]
    qa, ka, va, qn, kn, vn = jnp.split(w_in, [512, 640, 768, 1280, 1792], axis=-1)
    wt = jnp.swapaxes(jnp.concatenate([qa, ka, va], axis=-1), 1, 2).astype(BF16)
    wn = jnp.concatenate([qn, kn, vn], axis=-1).astype(BF16)
    wr_t = w_router.T
    wr_hi = wr_t.astype(BF16)
    wr_lo = (wr_t - wr_hi.astype(F32)).astype(BF16)
    return dict(
        wt=wt, wn=wn,
        qg=q_norm_a.reshape(n_layers, HEAD_DIM, 1), kg=k_norm_a.reshape(n_layers, HEAD_DIM, 1),
        bias=_nattn_bias_tables(rpb_b),
        ga=g_out_a.reshape(n_layers, 1, A_WIDTH), gb=g_out_b.reshape(n_layers, 1, B_WIDTH),
        wo=w_out.astype(BF16), wr_hi=wr_hi, wr_lo=wr_lo, br=b_router.reshape(N_EXPERTS, 1),
        wg=w_gate.astype(BF16), wu=w_up.astype(BF16), wd=w_down.astype(BF16),
        wpg=w_ple_gate.astype(BF16), bpg=b_ple_gate.reshape(n_layers, 1, D_MODEL), wpp=w_ple_proj.astype(BF16),
    )


def _layer(lidx, x, p_flat, w, ln_g, ln_b, rope):
    nb, s, _ = x.shape
    qz, ka, vat, qn, kn, vn, qnorm, knorm = _inproj(lidx, x, w["wt"], w["wn"], rope[0], rope[1], w["qg"], w["kg"])
    oa = _gattn(qz, ka, vat, qnorm, knorm)
    ob = _nattn(lidx, qn, kn, vn, w["bias"])
    t = nb * s
    h, hb, gates, goh = _post_attn(lidx, oa.reshape(t, A_WIDTH), ob.reshape(t, B_WIDTH), x.reshape(t, D_MODEL),
                              w["wo"], w["ga"], w["gb"], ln_g, ln_b, w["wr_hi"], w["wr_lo"], w["br"])
    ffn = _moe(lidx, hb, gates, goh, w["wg"], w["wu"], w["wd"])
    xo = _post_moe(lidx, h, ffn, p_flat, w["wpg"], w["bpg"], w["wpp"], ln_g, ln_b)
    return xo.reshape(nb, s, D_MODEL)


def kernel(x_prompt, x_sample, p_prompt, p_sample, w_in, q_norm_a, k_norm_a, rpb_b, g_out_a, g_out_b, w_out, ln_g, ln_b, w_router, b_router, w_gate, w_up, w_down, w_ple_gate, b_ple_gate, w_ple_proj):
    n_layers = w_in.shape[0]
    w = _prepare_weights(w_in, q_norm_a, k_norm_a, rpb_b, g_out_a, g_out_b, w_out, w_router, b_router,
                         w_gate, w_up, w_down, w_ple_gate, b_ple_gate, w_ple_proj)
    trunks = []
    for x, p in ((x_prompt, p_prompt), (x_sample, p_sample)):
        nb, s, _ = x.shape
        trunks.append((x, p.reshape(n_layers, nb * s, PLE_DIM), _rope_tables_t(s)))

    def body(xs, layer):
        lidx = jnp.reshape(layer, (1,)).astype(jnp.int32)
        new = tuple(_layer(lidx, x, trunks[i][1], w, ln_g, ln_b, trunks[i][2]) for i, x in enumerate(xs))
        return new, None

    (y_prompt, y_sample), _ = lax.scan(body, (trunks[0][0], trunks[1][0]), jnp.arange(n_layers, dtype=jnp.int32))
    return (y_prompt, y_sample)
```

```python
import functools
import math

import jax
import jax.numpy as jnp
import numpy as np
from jax import lax
from jax.experimental import pallas as pl
from jax.experimental.pallas import tpu as pltpu

F32 = jnp.float32
BF16 = jnp.bfloat16

D_MODEL = 1024
N_LAYERS = 4
HEAD_DIM = 64
A_HEADS = 8
A_KV_HEADS = 2
A_GROUP = A_HEADS // A_KV_HEADS
B_HEADS = 8
A_WIDTH = A_HEADS * HEAD_DIM
KV_WIDTH = A_KV_HEADS * HEAD_DIM
B_WIDTH = B_HEADS * HEAD_DIM
GRID_W = 64
QBLK = 128
WIN_R = 8
WIN_C = 16
ROPE_THETA = 10000.0
N_EXPERTS = 16
N_GROUPS = 4
EXPERTS_PER_GROUP = N_EXPERTS // N_GROUPS
D_FF_EXPERT = 512
PLE_DIM = 256
LN_EPS = 1e-5
QK_EPS = 1e-6
NEG_INF = -1e30
ALPHA = (2 * N_LAYERS) ** 0.25
Q_SCALE = HEAD_DIM ** -0.5
LOG2_E = math.log2(math.e)

LANES = 128

TOKEN_TILE = 1024
ATTN_Q_TILE = 512
ROW_TILE = 1024
POST_ROWS = 128
MOE_TILE = 1024
MOE_SUB = 128
EXPERTS_PER_STEP = 4
BAND_ROWS = 10
BAND_BLKS = BAND_ROWS * GRID_W // QBLK
N_BIAS_VARIANTS = 5
NATTN_AHEAD = 3
GATTN_AHEAD = 1
VMEM_LIMIT = 56 * 1024 * 1024

_QA0, _KA0, _VA0, _PT_ROWS = 0, 512, 640, 768
_QN0, _KN0, _VN0, _NAT_COLS = 0, 512, 1024, 1536
ONES_ROWS = 16
VT_ROWS = HEAD_DIM + ONES_ROWS
SHIFT_BOUND_MAX = 40.0


def _params(sem):
    return pltpu.CompilerParams(dimension_semantics=sem, vmem_limit_bytes=VMEM_LIMIT)


def _layer_norm(y, g, b):
    mu = jnp.mean(y, axis=-1, keepdims=True)
    d = y - mu
    var = jnp.mean(d * d, axis=-1, keepdims=True)
    return d * lax.rsqrt(var + LN_EPS) * g + b


def _rope_t(y, cos, sin):
    cr, cc = cos[0:16], cos[16:32]
    sr, sc = sin[0:16], sin[16:32]
    a1, a2, b1, b2 = y[0:16], y[16:32], y[32:48], y[48:64]
    return jnp.concatenate([a1 * cr - a2 * sr, a1 * sr + a2 * cr,
                            b1 * cc - b2 * sc, b1 * sc + b2 * cc], axis=0)


def _norm_rope_t(y, gain, cos, sin):
    ms = jnp.sum(y * y, axis=0, keepdims=True) * (1.0 / HEAD_DIM)
    return _rope_t(y * lax.rsqrt(ms + QK_EPS) * gain, cos, sin)


def _inproj_kernel(lidx_ref, x_ref, wt_ref, wn_ref, cos_ref, sin_ref, qg_ref, kg_ref,
                   qz_ref, ka_ref, vat_ref, qn_ref, kn_ref, vn_ref, qnorm_ref, knorm_ref):
    del lidx_ref
    xb = x_ref[0].astype(BF16)
    pt = lax.dot_general(wt_ref[0], xb, (((1,), (1,)), ((), ())),
                         preferred_element_type=F32)
    tm = xb.shape[0]
    cos = cos_ref[...]
    sin = sin_ref[...]
    qg = qg_ref[0] * (Q_SCALE * LOG2_E)
    kg = kg_ref[0]
    zeros = jnp.zeros((HEAD_DIM, tm), BF16)
    head_row = lax.broadcasted_iota(jnp.int32, (A_HEADS, 1), 0)
    l2 = lambda y: jnp.sqrt(jnp.sum(y * y, axis=0, keepdims=True))
    qnorm = jnp.zeros((A_HEADS, tm), F32)
    for h in range(A_HEADS):
        qf = _norm_rope_t(pt[_QA0 + h * HEAD_DIM:_QA0 + (h + 1) * HEAD_DIM], qg, cos, sin)
        qnorm = jnp.where(head_row == h, l2(qf), qnorm)
        q = qf.astype(BF16)
        if h // A_GROUP == 0:
            qz_ref[0, h] = jnp.concatenate([q, zeros], axis=0)
        else:
            qz_ref[0, h] = jnp.concatenate([zeros, q], axis=0)
    qnorm_ref[0] = qnorm
    ks = [_norm_rope_t(pt[_KA0 + g * HEAD_DIM:_KA0 + (g + 1) * HEAD_DIM], kg, cos, sin) for g in range(A_KV_HEADS)]
    knorm = jnp.zeros((A_HEADS, tm), F32)
    for g in range(A_KV_HEADS):
        knorm = jnp.where(head_row == g, l2(ks[g]), knorm)
    knorm_ref[0, 0] = knorm
    kt = jnp.concatenate(ks, axis=0)
    ka_ref[0, 0] = kt.T.astype(BF16)
    ones = jnp.ones((ONES_ROWS, tm), BF16)
    for g in range(A_KV_HEADS):
        v = pt[_VA0 + g * HEAD_DIM:_VA0 + (g + 1) * HEAD_DIM].astype(BF16)
        vat_ref[0, 0, g] = jnp.concatenate([v, ones], axis=0)
    nat = jnp.dot(xb, wn_ref[0], preferred_element_type=F32)
    qn_ref[0] = (nat[:, _QN0:_QN0 + B_WIDTH] * (Q_SCALE * LOG2_E)).astype(BF16)
    kn_ref[0] = nat[:, _KN0:_KN0 + B_WIDTH].astype(BF16).reshape(tm // QBLK, QBLK, B_WIDTH)
    vn_ref[0] = nat[:, _VN0:_VN0 + B_WIDTH].astype(BF16).reshape(tm // QBLK, QBLK, B_WIDTH)


def _inproj(lidx, x, wt, wn, cos_t, sin_t, qg, kg):
    nb, s, _ = x.shape
    tm = TOKEN_TILE
    nc = s // tm
    out_shape = (
        jax.ShapeDtypeStruct((nb, A_HEADS, 2 * HEAD_DIM, s), BF16),
        jax.ShapeDtypeStruct((nb, nc, tm, KV_WIDTH), BF16),
        jax.ShapeDtypeStruct((nb, nc, A_KV_HEADS, VT_ROWS, tm), BF16),
        jax.ShapeDtypeStruct((nb, s, B_WIDTH), BF16),
        jax.ShapeDtypeStruct((nb, s // QBLK, QBLK, B_WIDTH), BF16),
        jax.ShapeDtypeStruct((nb, s // QBLK, QBLK, B_WIDTH), BF16),
        jax.ShapeDtypeStruct((nb, A_HEADS, s), F32),
        jax.ShapeDtypeStruct((nb, nc, A_HEADS, tm), F32),
    )
    grid_spec = pltpu.PrefetchScalarGridSpec(
        num_scalar_prefetch=1,
        grid=(nb, nc),
        in_specs=[
            pl.BlockSpec((1, tm, D_MODEL), lambda b, c, l: (b, c, 0)),
            pl.BlockSpec((1, _PT_ROWS, D_MODEL), lambda b, c, l: (l[0], 0, 0)),
            pl.BlockSpec((1, D_MODEL, _NAT_COLS), lambda b, c, l: (l[0], 0, 0)),
            pl.BlockSpec((HEAD_DIM // 2, tm), lambda b, c, l: (0, c)),
            pl.BlockSpec((HEAD_DIM // 2, tm), lambda b, c, l: (0, c)),
            pl.BlockSpec((1, HEAD_DIM, 1), lambda b, c, l: (l[0], 0, 0)),
            pl.BlockSpec((1, HEAD_DIM, 1), lambda b, c, l: (l[0], 0, 0)),
        ],
        out_specs=[
            pl.BlockSpec((1, A_HEADS, 2 * HEAD_DIM, tm), lambda b, c, l: (b, 0, 0, c)),
            pl.BlockSpec((1, 1, tm, KV_WIDTH), lambda b, c, l: (b, c, 0, 0)),
            pl.BlockSpec((1, 1, A_KV_HEADS, VT_ROWS, tm), lambda b, c, l: (b, c, 0, 0, 0)),
            pl.BlockSpec((1, tm, B_WIDTH), lambda b, c, l: (b, c, 0)),
            pl.BlockSpec((1, tm // QBLK, QBLK, B_WIDTH), lambda b, c, l: (b, c, 0, 0)),
            pl.BlockSpec((1, tm // QBLK, QBLK, B_WIDTH), lambda b, c, l: (b, c, 0, 0)),
            pl.BlockSpec((1, A_HEADS, tm), lambda b, c, l: (b, 0, c)),
            pl.BlockSpec((1, 1, A_HEADS, tm), lambda b, c, l: (b, c, 0, 0)),
        ],
    )
    return pl.pallas_call(
        _inproj_kernel, out_shape=out_shape, grid_spec=grid_spec,
        compiler_params=_params(("parallel", "parallel")), name="inproj",
    )(lidx, x, wt, wn, cos_t, sin_t, qg, kg)


def _col_max(st):
    slabs = [st[r:r + 128] for r in range(0, st.shape[0], 128)]
    while len(slabs) > 1:
        slabs = [jnp.maximum(slabs[i], slabs[i + 1]) for i in range(0, len(slabs), 2)]
    return jnp.max(slabs[0], axis=0, keepdims=True)


def _gattn_kernel(qz_ref, k_ref, vt_ref, qn_ref, kn_ref, o_ref, *scratch):
    m_refs, acc_refs = scratch[:A_HEADS], scratch[A_HEADS:]
    n_chunks = k_ref.shape[1]
    for h in range(A_HEADS):
        acc_refs[h][...] = jnp.zeros(acc_refs[h].shape, F32)

    k_norm_max = jnp.max(jnp.max(kn_ref[0], axis=0), axis=1, keepdims=True)
    row = lax.broadcasted_iota(jnp.int32, (A_HEADS, 1), 0)
    k_of_head = jnp.where(row < A_GROUP, k_norm_max[0:1], k_norm_max[1:2])
    bound = qn_ref[0] * k_of_head
    bounded = jnp.max(bound) <= SHIFT_BOUND_MAX

    def chunk_body(c, carry, online):
        kc = k_ref[0, c]
        scores = lambda h: jnp.dot(kc, qz_ref[0, h], preferred_element_type=F32)
        pending = [scores(h) for h in range(GATTN_AHEAD)]
        for h in range(A_HEADS):
            st = pending.pop(0)
            if h + GATTN_AHEAD < A_HEADS:
                pending.append(scores(h + GATTN_AHEAD))
            vt = vt_ref[0, c, h // A_GROUP]
            if online:
                m_old = m_refs[h][...]
                m_new = jnp.maximum(m_old, _col_max(st))
                alpha = jnp.exp2(m_old - m_new)
                p = jnp.exp2(st - m_new).astype(BF16)
                acc_refs[h][...] = alpha * acc_refs[h][...] + jnp.dot(vt, p, preferred_element_type=F32)
                m_refs[h][...] = m_new
            else:
                p = jnp.exp2(st - m_refs[h][...]).astype(BF16)
                acc_refs[h][...] += jnp.dot(vt, p, preferred_element_type=F32)
        return carry

    @pl.when(bounded)
    def _():
        for h in range(A_HEADS):
            m_refs[h][...] = bound[h:h + 1]
        lax.fori_loop(0, n_chunks, functools.partial(chunk_body, online=False), 0)

    @pl.when(jnp.logical_not(bounded))
    def _():
        for h in range(A_HEADS):
            m_refs[h][...] = jnp.full(m_refs[h].shape, -jnp.inf, F32)
        lax.fori_loop(0, n_chunks, functools.partial(chunk_body, online=True), 0)

    outs = []
    for h in range(A_HEADS):
        acc = acc_refs[h][...]
        outs.append(acc[0:HEAD_DIM] * (1.0 / acc[HEAD_DIM:HEAD_DIM + 1]))
    o_ref[0] = jnp.concatenate(outs, axis=0).T


def _gattn(qz, ka, vat, qnorm, knorm):
    nb, _, _, s = qz.shape
    nc, tk = ka.shape[1], ka.shape[2]
    tq = ATTN_Q_TILE
    return pl.pallas_call(
        _gattn_kernel,
        out_shape=jax.ShapeDtypeStruct((nb, s, A_WIDTH), F32),
        grid=(nb, s // tq),
        in_specs=[
            pl.BlockSpec((1, A_HEADS, 2 * HEAD_DIM, tq), lambda b, i: (b, 0, 0, i)),
            pl.BlockSpec((1, nc, tk, KV_WIDTH), lambda b, i: (b, 0, 0, 0)),
            pl.BlockSpec((1, nc, A_KV_HEADS, VT_ROWS, tk), lambda b, i: (b, 0, 0, 0, 0)),
            pl.BlockSpec((1, A_HEADS, tq), lambda b, i: (b, 0, i)),
            pl.BlockSpec((1, nc, A_HEADS, tk), lambda b, i: (b, 0, 0, 0)),
        ],
        out_specs=pl.BlockSpec((1, tq, A_WIDTH), lambda b, i: (b, i, 0)),
        scratch_shapes=[pltpu.VMEM((1, tq), F32)] * A_HEADS + [pltpu.VMEM((VT_ROWS, tq), F32)] * A_HEADS,
        compiler_params=_params(("parallel", "parallel")), name="gattn",
    )(qz, ka, vat, qnorm, knorm)


def _nattn_kernel(*refs):
    q_ref = refs[0]
    k_refs = refs[1:1 + BAND_BLKS]
    v_refs = refs[1 + BAND_BLKS:1 + 2 * BAND_BLKS]
    bias_ref = refs[1 + 2 * BAND_BLKS]
    o_ref = refs[2 + 2 * BAND_BLKS]
    first_head = lax.broadcasted_iota(jnp.int32, (QBLK, LANES), 1) < HEAD_DIM
    dn = (((1,), (1,)), ((), ()))
    def band(refs, c):
        return jnp.concatenate([refs[t][0, 0, :, c * LANES:(c + 1) * LANES] for t in range(BAND_BLKS)], axis=0)

    def scores(h):
        c, hh = h // 2, h % 2
        qp = q_ref[0, :, c * LANES:(c + 1) * LANES]
        qm = jnp.where(first_head if hh == 0 else jnp.logical_not(first_head), qp, jnp.zeros_like(qp))
        return lax.dot_general(qm, band(k_refs, c), dn, preferred_element_type=F32) + bias_ref[0, h]

    outs = []
    pending = [scores(h) for h in range(NATTN_AHEAD)]
    for h in range(B_HEADS):
        s = pending.pop(0)
        if h + NATTN_AHEAD < B_HEADS:
            pending.append(scores(h + NATTN_AHEAD))
        m = jnp.max(s, axis=-1, keepdims=True)
        p = jnp.exp2(s - m)
        l = jnp.sum(p, axis=-1, keepdims=True)
        outs.append(jnp.dot(p.astype(BF16), band(v_refs, h // 2), preferred_element_type=F32) * (1.0 / l))
    pairs = [jnp.where(first_head, outs[2 * c], outs[2 * c + 1]) for c in range(B_HEADS // 2)]
    o_ref[0] = jnp.concatenate(pairs, axis=-1)


def _band_start_blk(j, nblk):
    rows_per_blk = QBLK // GRID_W
    rows = nblk * rows_per_blk
    bs = jnp.clip(j * rows_per_blk - WIN_R // 2, 0, rows - BAND_ROWS)
    return bs // rows_per_blk


def _bias_variant(j, nblk):
    return jnp.where(j < 2, j, jnp.where(j >= nblk - 2, j - nblk + N_BIAS_VARIANTS, 2))


def _nattn(lidx, qn, kn, vn, bias):
    nb, s, _ = qn.shape
    nblk = s // QBLK
    band_specs = [pl.BlockSpec((1, 1, QBLK, B_WIDTH),
                               functools.partial(lambda b, j, l, t: (b, _band_start_blk(j, nblk) + t, 0, 0), t=t))
                  for t in range(BAND_BLKS)]
    grid_spec = pltpu.PrefetchScalarGridSpec(
        num_scalar_prefetch=1,
        grid=(nb, nblk),
        in_specs=[pl.BlockSpec((1, QBLK, B_WIDTH), lambda b, j, l: (b, j, 0))]
        + band_specs + band_specs
        + [pl.BlockSpec((1, B_HEADS, QBLK, BAND_ROWS * GRID_W),
                        lambda b, j, l: (l[0] * N_BIAS_VARIANTS + _bias_variant(j, nblk), 0, 0, 0))],
        out_specs=pl.BlockSpec((1, QBLK, B_WIDTH), lambda b, j, l: (b, j, 0)),
    )

    def kernel(lidx_ref, *refs):
        del lidx_ref
        _nattn_kernel(*refs)

    return pl.pallas_call(
        kernel, out_shape=jax.ShapeDtypeStruct((nb, s, B_WIDTH), F32), grid_spec=grid_spec,
        compiler_params=_params(("parallel", "arbitrary")), name="nattn",
    )(lidx, qn, *([kn] * BAND_BLKS), *([vn] * BAND_BLKS), bias)


def _nattn_bias_tables(rpb):
    n_layers = rpb.shape[0]
    n_r, n_c = 2 * WIN_R - 1, 2 * WIN_C - 1
    qc = np.arange(GRID_W)[:, None]
    kc = np.arange(GRID_W)[None, :]
    ic = np.clip(kc - qc + WIN_C - 1, 0, n_c - 1)
    cs = np.clip(qc - WIN_C // 2, 0, GRID_W - WIN_C)
    col_ok = (kc >= cs) & (kc < cs + WIN_C)
    onehot = jnp.asarray((ic.reshape(-1)[None, :] == np.arange(n_c)[:, None]).astype(np.float32))
    cols = jnp.dot(rpb.astype(F32).reshape(-1, n_c), onehot, precision=lax.Precision.HIGHEST)
    cols = cols.reshape(n_layers, B_HEADS, n_r, GRID_W, GRID_W) * LOG2_E
    rows_per_blk = QBLK // GRID_W
    rows = 16
    variants = []
    for r0 in (0, 2, 4, rows - 4, rows - 2):
        bs = min(max(r0 - WIN_R // 2, 0), rows - BAND_ROWS)
        q_rows = []
        for qo in range(rows_per_blk):
            qr = r0 + qo
            rs = min(max(qr - WIN_R // 2, 0), rows - WIN_R)
            k_rows = []
            for ko in range(BAND_ROWS):
                kr = bs + ko
                ir = min(max(kr - qr + WIN_R - 1, 0), n_r - 1)
                if rs <= kr < rs + WIN_R:
                    k_rows.append(jnp.where(jnp.asarray(col_ok)[None, None], cols[:, :, ir], NEG_INF))
                else:
                    k_rows.append(jnp.full((n_layers, B_HEADS, GRID_W, GRID_W), NEG_INF, F32))
            q_rows.append(jnp.concatenate(k_rows, axis=-1))
        variants.append(jnp.concatenate(q_rows, axis=-2))
    t = jnp.stack(variants, axis=1)
    return t.reshape(n_layers * N_BIAS_VARIANTS, B_HEADS, QBLK, BAND_ROWS * GRID_W)


def _group_member(x, d, wrapped):
    return jnp.where(wrapped, pltpu.roll(x, EXPERTS_PER_GROUP - d, 0), pltpu.roll(x, N_EXPERTS - d, 0))


def _route_t(logits_t):
    mx = jnp.max(logits_t, axis=0, keepdims=True)
    ex = jnp.exp(logits_t - mx)
    probs = ex / jnp.sum(ex, axis=0, keepdims=True)
    e_idx = lax.broadcasted_iota(jnp.int32, probs.shape, 0)
    pos = e_idx % EXPERTS_PER_GROUP
    rank = jnp.zeros(probs.shape, jnp.int32)
    for d in range(1, EXPERTS_PER_GROUP):
        wrapped = pos + d >= EXPERTS_PER_GROUP
        other = _group_member(probs, d, wrapped)
        beats = (other > probs) | ((other == probs) & wrapped)
        rank = rank + beats.astype(jnp.int32)
    top = rank < 2
    kept = jnp.where(top, probs, 0.0)
    score = kept
    for d in range(1, EXPERTS_PER_GROUP):
        score = score + _group_member(kept, d, pos + d >= EXPERTS_PER_GROUP)
    grp = e_idx // EXPERTS_PER_GROUP
    lost = jnp.zeros(probs.shape, jnp.bool_)
    for d in range(1, N_GROUPS):
        other = pltpu.roll(score, N_EXPERTS - d * EXPERTS_PER_GROUP, 0)
        earlier = grp + d >= N_GROUPS
        lost = lost | (other > score) | ((other == score) & earlier)
    won = jnp.logical_not(lost)
    gates = jnp.where(top & won, probs / score, 0.0)
    group_onehot = jnp.where(won & (pos == 0), 1.0, 0.0)
    return gates, group_onehot


def _post_attn_kernel(lidx_ref, oa_ref, ob_ref, x_ref, wo_ref, ga_ref, gb_ref, lng_ref, lnb_ref,
                      wrh_ref, wrl_ref, br_ref, h_ref, hb_ref, gates_ref, goh_ref):
    del lidx_ref
    tm = x_ref.shape[0]
    n_part = tm // POST_ROWS
    parts = [pl.ds(r * POST_ROWS, POST_ROWS) for r in range(n_part)]
    dn = (((1,), (1,)), ((), ()))

    def rms(ref, g_ref, rows):
        a = ref[rows, :]
        return a * lax.rsqrt(jnp.mean(a * a, axis=-1, keepdims=True) + LN_EPS) * g_ref[0]

    cats = [jnp.concatenate([rms(oa_ref, ga_ref, rows), rms(ob_ref, gb_ref, rows)], axis=-1).astype(BF16)
            for rows in parts]
    mixes = [jnp.dot(cat, wo_ref[0], preferred_element_type=F32) for cat in cats]
    hs = [_layer_norm(ALPHA * x_ref[rows, :] + mix, lng_ref[0, 0:1], lnb_ref[0, 0:1]) for rows, mix in zip(parts, mixes)]
    for rows, h in zip(parts, hs):
        h_ref[rows, :] = h
        h_hi = h.astype(BF16)
        hb_ref[rows, :] = h_hi
        h_lo = (h - h_hi.astype(F32)).astype(BF16)
        logits_t = (lax.dot_general(wrh_ref[...], h_hi, dn, preferred_element_type=F32)
                    + lax.dot_general(wrh_ref[...], h_lo, dn, preferred_element_type=F32)
                    + lax.dot_general(wrl_ref[...], h_hi, dn, preferred_element_type=F32)) + br_ref[...]
        gates_t, group_onehot = _route_t(logits_t)
        goh_ref[:, rows] = group_onehot.astype(BF16)
        gates_ref[:, rows] = gates_t


def _post_attn(lidx, oa, ob, x, wo, ga, gb, lng, lnb, wrh, wrl, br):
    t = x.shape[0]
    tm = ROW_TILE
    row = lambda i, l: (i, 0)
    lay = lambda i, l: (l[0], 0, 0)
    full = lambda i, l: (0, 0)
    grid_spec = pltpu.PrefetchScalarGridSpec(
        num_scalar_prefetch=1,
        grid=(t // tm,),
        in_specs=[
            pl.BlockSpec((tm, A_WIDTH), row),
            pl.BlockSpec((tm, B_WIDTH), row),
            pl.BlockSpec((tm, D_MODEL), row),
            pl.BlockSpec((1, D_MODEL, D_MODEL), lay),
            pl.BlockSpec((1, 1, A_WIDTH), lay),
            pl.BlockSpec((1, 1, B_WIDTH), lay),
            pl.BlockSpec((1, 3, D_MODEL), lay),
            pl.BlockSpec((1, 3, D_MODEL), lay),
            pl.BlockSpec((N_EXPERTS, D_MODEL), full),
            pl.BlockSpec((N_EXPERTS, D_MODEL), full),
            pl.BlockSpec((N_EXPERTS, 1), full),
        ],
        out_specs=[
            pl.BlockSpec((tm, D_MODEL), row),
            pl.BlockSpec((tm, D_MODEL), row),
            pl.BlockSpec((N_EXPERTS, tm), lambda i, l: (0, i)),
            pl.BlockSpec((N_EXPERTS, tm), lambda i, l: (0, i)),
        ],
    )
    return pl.pallas_call(
        _post_attn_kernel,
        out_shape=(jax.ShapeDtypeStruct((t, D_MODEL), F32),
                   jax.ShapeDtypeStruct((t, D_MODEL), BF16),
                   jax.ShapeDtypeStruct((N_EXPERTS, t), F32),
                   jax.ShapeDtypeStruct((N_EXPERTS, t), BF16)),
        grid_spec=grid_spec,
        compiler_params=_params(("parallel",)), name="post_attn",
    )(lidx, oa, ob, x, wo, ga, gb, lng, lnb, wrh, wrl, br)


def _moe_kernel(lidx_ref, bounds_ref, hb_ref, gates_ref, goh_ref, upper_ref, wg_ref, wu_ref, wd_ref, o_ref,
                perm_ref, xs_ref, gs_ref, ys_ref):
    del lidx_ref
    i = pl.program_id(0)
    e = pl.program_id(1)
    tm = hb_ref.shape[0]
    base = i * (2 * N_GROUPS)

    @pl.when(e == 0)
    def _sort():
        goh = goh_ref[...]
        ranks = jnp.dot(goh, upper_ref[...], preferred_element_type=F32)
        row = lax.broadcasted_iota(jnp.int32, (N_EXPERTS, 1), 0)
        start = jnp.zeros((N_EXPERTS, 1), F32)
        for g in range(N_GROUPS):
            start = jnp.where(row == g * EXPERTS_PER_GROUP, bounds_ref[base + g].astype(F32), start)
        pos = jnp.sum(goh.astype(F32) * (ranks + start), axis=0, keepdims=True)
        slot = lax.broadcasted_iota(jnp.int32, (tm, tm), 0)
        perm = jnp.where(slot == pos.astype(jnp.int32), 1.0, 0.0).astype(BF16)
        perm_ref[...] = perm
        xs_ref[...] = jnp.dot(perm, hb_ref[...], preferred_element_type=F32).astype(BF16)
        gates_t = gates_ref[...]
        g_hi = gates_t.astype(BF16)
        g_lo = (gates_t - g_hi.astype(F32)).astype(BF16)
        both = lax.dot_general(jnp.concatenate([g_hi, g_lo], axis=0), perm, (((1,), (1,)), ((), ())),
                               preferred_element_type=F32)
        sorted_t = both[0:N_EXPERTS] + both[N_EXPERTS:2 * N_EXPERTS]
        gs_ref[...] = jnp.concatenate([sorted_t, jnp.zeros((LANES - N_EXPERTS, tm), F32)], axis=0).T
        ys_ref[...] = jnp.zeros(ys_ref.shape, F32)

    grp = (e * EXPERTS_PER_STEP) // EXPERTS_PER_GROUP
    first = bounds_ref[base + grp] // MOE_SUB
    stop = (bounds_ref[base + N_GROUPS + grp] + MOE_SUB - 1) // MOE_SUB
    def experts(row0, n_rows):
        rows = pl.ds(pl.multiple_of(row0, MOE_SUB), n_rows)
        lane = lax.broadcasted_iota(jnp.int32, (n_rows, LANES), 1)
        x = xs_ref[rows, :]
        gs = gs_ref[rows, :]
        total = None
        for j in range(EXPERTS_PER_STEP):
            gate = jnp.dot(x, wg_ref[0, 0, j], preferred_element_type=F32)
            up = jnp.dot(x, wu_ref[0, 0, j], preferred_element_type=F32)
            act = (gate * jax.nn.sigmoid(gate) * up).astype(BF16)
            y = jnp.dot(act, wd_ref[0, 0, j], preferred_element_type=F32)
            g_col = jnp.sum(jnp.where(lane == e * EXPERTS_PER_STEP + j, gs, 0.0), axis=1, keepdims=True)
            total = g_col * y if total is None else total + g_col * y
        ys_ref[rows, :] += total

    n_pairs = (stop - first) // 2

    def pair_body(s, carry):
        experts((first + 2 * s) * MOE_SUB, 2 * MOE_SUB)
        return carry

    lax.fori_loop(0, n_pairs, pair_body, 0)

    @pl.when(first + 2 * n_pairs < stop)
    def _():
        experts((stop - 1) * MOE_SUB, MOE_SUB)

    @pl.when(e == N_EXPERTS // EXPERTS_PER_STEP - 1)
    def _unsort():
        o_ref[...] = lax.dot_general(perm_ref[...], ys_ref[...].astype(BF16), (((0,), (0,)), ((), ())),
                                     preferred_element_type=F32)


def _tile_bounds(goh, tm):
    nt = goh.shape[1] // tm
    cnt = goh[::EXPERTS_PER_GROUP].astype(F32).reshape(N_GROUPS, nt, tm).sum(-1).T.astype(jnp.int32)
    end = jnp.cumsum(cnt, axis=1)
    return jnp.concatenate([end - cnt, end], axis=1).reshape(-1)


def _moe(lidx, hb, gates, goh, wg, wu, wd):
    t = hb.shape[0]
    tm = min(MOE_TILE, t)
    bounds = _tile_bounds(goh, tm)
    n_steps = N_EXPERTS // EXPERTS_PER_STEP
    wg, wu, wd = (w.reshape(w.shape[0], n_steps, EXPERTS_PER_STEP, *w.shape[2:]) for w in (wg, wu, wd))
    upper =(jnp.arange(tm)[:, None] < jnp.arange(tm)[None, :]).astype(BF16)
    grid_spec = pltpu.PrefetchScalarGridSpec(
        num_scalar_prefetch=2,
        grid=(t // tm, n_steps),
        in_specs=[
            pl.BlockSpec((tm, D_MODEL), lambda i, e, l, b: (i, 0)),
            pl.BlockSpec((N_EXPERTS, tm), lambda i, e, l, b: (0, i)),
            pl.BlockSpec((N_EXPERTS, tm), lambda i, e, l, b: (0, i)),
            pl.BlockSpec((tm, tm), lambda i, e, l, b: (0, 0)),
            pl.BlockSpec((1, 1, EXPERTS_PER_STEP, D_MODEL, D_FF_EXPERT), lambda i, e, l, b: (l[0], e, 0, 0, 0)),
            pl.BlockSpec((1, 1, EXPERTS_PER_STEP, D_MODEL, D_FF_EXPERT), lambda i, e, l, b: (l[0], e, 0, 0, 0)),
            pl.BlockSpec((1, 1, EXPERTS_PER_STEP, D_FF_EXPERT, D_MODEL), lambda i, e, l, b: (l[0], e, 0, 0, 0)),
        ],
        out_specs=pl.BlockSpec((tm, D_MODEL), lambda i, e, l, b: (i, 0)),
        scratch_shapes=[pltpu.VMEM((tm, tm), BF16), pltpu.VMEM((tm, D_MODEL), BF16),
                        pltpu.VMEM((tm, LANES), F32), pltpu.VMEM((tm, D_MODEL), F32)],
    )
    return pl.pallas_call(
        _moe_kernel, out_shape=jax.ShapeDtypeStruct((t, D_MODEL), F32), grid_spec=grid_spec,
        compiler_params=_params(("parallel", "arbitrary")), name="moe",
    )(lidx, bounds, hb, gates, goh, upper, wg, wu, wd)


def _post_moe_kernel(lidx_ref, h_ref, f_ref, p_ref, wpg_ref, bpg_ref, wpp_ref, lng_ref, lnb_ref, x_ref):
    del lidx_ref
    parts = [pl.ds(r * POST_ROWS, POST_ROWS) for r in range(h_ref.shape[0] // POST_ROWS)]
    h2s = [_layer_norm(ALPHA * h_ref[rows, :] + f_ref[rows, :], lng_ref[0, 1:2], lnb_ref[0, 1:2]) for rows in parts]
    zs = [jnp.dot(h2.astype(BF16), wpg_ref[0], preferred_element_type=F32) + bpg_ref[0] for h2 in h2s]
    projs = [jnp.dot(p_ref[0, rows, :].astype(BF16), wpp_ref[0], preferred_element_type=F32) for rows in parts]
    for rows, h2, z, proj in zip(parts, h2s, zs, projs):
        pe = jax.nn.sigmoid(z) * proj
        x_ref[rows, :] = _layer_norm(ALPHA * h2 + pe, lng_ref[0, 2:3], lnb_ref[0, 2:3])


def _post_moe(lidx, h, ffn, p, wpg, bpg, wpp, lng, lnb):
    t = h.shape[0]
    tm = ROW_TILE
    row = lambda i, l: (i, 0)
    lay = lambda i, l: (l[0], 0, 0)
    grid_spec = pltpu.PrefetchScalarGridSpec(
        num_scalar_prefetch=1,
        grid=(t // tm,),
        in_specs=[
            pl.BlockSpec((tm, D_MODEL), row),
            pl.BlockSpec((tm, D_MODEL), row),
            pl.BlockSpec((1, tm, PLE_DIM), lambda i, l: (l[0], i, 0)),
            pl.BlockSpec((1, D_MODEL, D_MODEL), lay),
            pl.BlockSpec((1, 1, D_MODEL), lay),
            pl.BlockSpec((1, PLE_DIM, D_MODEL), lay),
            pl.BlockSpec((1, 3, D_MODEL), lay),
            pl.BlockSpec((1, 3, D_MODEL), lay),
        ],
        out_specs=pl.BlockSpec((tm, D_MODEL), row),
    )
    return pl.pallas_call(
        _post_moe_kernel, out_shape=jax.ShapeDtypeStruct((t, D_MODEL), F32), grid_spec=grid_spec,
        compiler_params=_params(("parallel",)), name="post_moe",
    )(lidx, h, ffn, p, wpg, bpg, wpp, lng, lnb)


def _rope_tables_t(s):
    t = jnp.arange(s, dtype=jnp.int32)
    row = (t // GRID_W).astype(F32)
    col = (t % GRID_W).astype(F32)
    n_freq = HEAD_DIM // 4
    inv_freq = ROPE_THETA ** (-jnp.arange(n_freq, dtype=F32) / n_freq)
    ang = jnp.concatenate([inv_freq[:, None] * row[None, :], inv_freq[:, None] * col[None, :]], axis=0)
    return jnp.cos(ang), jnp.sin(ang)


def _prepare_weights(w_in, q_norm_a, k_norm_a, rpb_b, g_out_a, g_out_b, w_out, w_router, b_router,
                     w_gate, w_up, w_down, w_ple_gate, b_ple_gate, w_ple_proj):
    n_layers = w_in.shape[0]
    qa, ka, va, qn, kn, vn = jnp.split(w_in, [512, 640, 768, 1280, 1792], axis=-1)
    wt = jnp.swapaxes(jnp.concatenate([qa, ka, va], axis=-1), 1, 2).astype(BF16)
    wn = jnp.concatenate([qn, kn, vn], axis=-1).astype(BF16)
    wr_t = w_router.T
    wr_hi = wr_t.astype(BF16)
    wr_lo = (wr_t - wr_hi.astype(F32)).astype(BF16)
    return dict(
        wt=wt, wn=wn,
        qg=q_norm_a.reshape(n_layers, HEAD_DIM, 1), kg=k_norm_a.reshape(n_layers, HEAD_DIM, 1),
        bias=_nattn_bias_tables(rpb_b),
        ga=g_out_a.reshape(n_layers, 1, A_WIDTH), gb=g_out_b.reshape(n_layers, 1, B_WIDTH),
        wo=w_out.astype(BF16), wr_hi=wr_hi, wr_lo=wr_lo, br=b_router.reshape(N_EXPERTS, 1),
        wg=w_gate.astype(BF16), wu=w_up.astype(BF16), wd=w_down.astype(BF16),
        wpg=w_ple_gate.astype(BF16), bpg=b_ple_gate.reshape(n_layers, 1, D_MODEL), wpp=w_ple_proj.astype(BF16),
    )


def _layer(lidx, x, p_flat, w, ln_g, ln_b, rope):
    nb, s, _ = x.shape
    qz, ka, vat, qn, kn, vn, qnorm, knorm = _inproj(lidx, x, w["wt"], w["wn"], rope[0], rope[1], w["qg"], w["kg"])
    oa = _gattn(qz, ka, vat, qnorm, knorm)
    ob = _nattn(lidx, qn, kn, vn, w["bias"])
    t = nb * s
    h, hb, gates, goh = _post_attn(lidx, oa.reshape(t, A_WIDTH), ob.reshape(t, B_WIDTH), x.reshape(t, D_MODEL),
                              w["wo"], w["ga"], w["gb"], ln_g, ln_b, w["wr_hi"], w["wr_lo"], w["br"])
    ffn = _moe(lidx, hb, gates, goh, w["wg"], w["wu"], w["wd"])
    xo = _post_moe(lidx, h, ffn, p_flat, w["wpg"], w["bpg"], w["wpp"], ln_g, ln_b)
    return xo.reshape(nb, s, D_MODEL)


def kernel(x_prompt, x_sample, p_prompt, p_sample, w_in, q_norm_a, k_norm_a, rpb_b, g_out_a, g_out_b, w_out, ln_g, ln_b, w_router, b_router, w_gate, w_up, w_down, w_ple_gate, b_ple_gate, w_ple_proj):
    n_layers = w_in.shape[0]
    w = _prepare_weights(w_in, q_norm_a, k_norm_a, rpb_b, g_out_a, g_out_b, w_out, w_router, b_router,
                         w_gate, w_up, w_down, w_ple_gate, b_ple_gate, w_ple_proj)
    trunks = []
    for x, p in ((x_prompt, p_prompt), (x_sample, p_sample)):
        nb, s, _ = x.shape
        trunks.append((x, p.reshape(n_layers, nb * s, PLE_DIM), _rope_tables_t(s)))

    def body(xs, layer):
        lidx = jnp.reshape(layer, (1,)).astype(jnp.int32)
        new = tuple(_layer(lidx, x, trunks[i][1], w, ln_g, ln_b, trunks[i][2]) for i, x in enumerate(xs))
        return new, None

    (y_prompt, y_sample), _ = lax.scan(body, (trunks[0][0], trunks[1][0]), jnp.arange(n_layers, dtype=jnp.int32))
    return (y_prompt, y_sample)
```

```python
import functools
import math

import jax
import jax.numpy as jnp
import numpy as np
from jax import lax
from jax.experimental import pallas as pl
from jax.experimental.pallas import tpu as pltpu

F32 = jnp.float32
BF16 = jnp.bfloat16

D_MODEL = 1024
N_LAYERS = 4
HEAD_DIM = 64
A_HEADS = 8
A_KV_HEADS = 2
A_GROUP = A_HEADS // A_KV_HEADS
B_HEADS = 8
A_WIDTH = A_HEADS * HEAD_DIM
KV_WIDTH = A_KV_HEADS * HEAD_DIM
B_WIDTH = B_HEADS * HEAD_DIM
GRID_W = 64
QBLK = 128
WIN_R = 8
WIN_C = 16
ROPE_THETA = 10000.0
N_EXPERTS = 16
N_GROUPS = 4
EXPERTS_PER_GROUP = N_EXPERTS // N_GROUPS
D_FF_EXPERT = 512
PLE_DIM = 256
LN_EPS = 1e-5
QK_EPS = 1e-6
NEG_INF = -1e30
ALPHA = (2 * N_LAYERS) ** 0.25
Q_SCALE = HEAD_DIM ** -0.5
LOG2_E = math.log2(math.e)

LANES = 128

TOKEN_TILE = 1024
ATTN_Q_TILE = 512
ROW_TILE = 1024
POST_ROWS = 128
MOE_TILE = 1024
MOE_SUB = 128
EXPERTS_PER_STEP = 4
BAND_ROWS = 10
BAND_BLKS = BAND_ROWS * GRID_W // QBLK
N_BIAS_VARIANTS = 5
NATTN_AHEAD = 3
GATTN_AHEAD = 1
VMEM_LIMIT = 56 * 1024 * 1024

_QA0, _KA0, _VA0, _PT_ROWS = 0, 512, 640, 768
_QN0, _KN0, _VN0, _NAT_COLS = 0, 512, 1024, 1536
ONES_ROWS = 16
VT_ROWS = HEAD_DIM + ONES_ROWS
SHIFT_BOUND_MAX = 40.0


def _params(sem):
    return pltpu.CompilerParams(dimension_semantics=sem, vmem_limit_bytes=VMEM_LIMIT)


def _layer_norm(y, g, b):
    mu = jnp.mean(y, axis=-1, keepdims=True)
    d = y - mu
    var = jnp.mean(d * d, axis=-1, keepdims=True)
    return d * lax.rsqrt(var + LN_EPS) * g + b


def _rope_t(y, cos, sin):
    cr, cc = cos[0:16], cos[16:32]
    sr, sc = sin[0:16], sin[16:32]
    a1, a2, b1, b2 = y[0:16], y[16:32], y[32:48], y[48:64]
    return jnp.concatenate([a1 * cr - a2 * sr, a1 * sr + a2 * cr,
                            b1 * cc - b2 * sc, b1 * sc + b2 * cc], axis=0)


def _norm_rope_t(y, gain, cos, sin):
    ms = jnp.sum(y * y, axis=0, keepdims=True) * (1.0 / HEAD_DIM)
    return _rope_t(y * lax.rsqrt(ms + QK_EPS) * gain, cos, sin)


def _inproj_kernel(lidx_ref, x_ref, wt_ref, wn_ref, cos_ref, sin_ref, qg_ref, kg_ref,
                   qz_ref, ka_ref, vat_ref, qn_ref, kn_ref, vn_ref, qnorm_ref, knorm_ref):
    del lidx_ref
    xb = x_ref[0].astype(BF16)
    pt = lax.dot_general(wt_ref[0], xb, (((1,), (1,)), ((), ())),
                         preferred_element_type=F32)
    tm = xb.shape[0]
    cos = cos_ref[...]
    sin = sin_ref[...]
    qg = qg_ref[0] * (Q_SCALE * LOG2_E)
    kg = kg_ref[0]
    zeros = jnp.zeros((HEAD_DIM, tm), BF16)
    head_row = lax.broadcasted_iota(jnp.int32, (A_HEADS, 1), 0)
    l2 = lambda y: jnp.sqrt(jnp.sum(y * y, axis=0, keepdims=True))
    qnorm = jnp.zeros((A_HEADS, tm), F32)
    for h in range(A_HEADS):
        qf = _norm_rope_t(pt[_QA0 + h * HEAD_DIM:_QA0 + (h + 1) * HEAD_DIM], qg, cos, sin)
        qnorm = jnp.where(head_row == h, l2(qf), qnorm)
        q = qf.astype(BF16)
        if h // A_GROUP == 0:
            qz_ref[0, h] = jnp.concatenate([q, zeros], axis=0)
        else:
            qz_ref[0, h] = jnp.concatenate([zeros, q], axis=0)
    qnorm_ref[0] = qnorm
    ks = [_norm_rope_t(pt[_KA0 + g * HEAD_DIM:_KA0 + (g + 1) * HEAD_DIM], kg, cos, sin) for g in range(A_KV_HEADS)]
    knorm = jnp.zeros((A_HEADS, tm), F32)
    for g in range(A_KV_HEADS):
        knorm = jnp.where(head_row == g, l2(ks[g]), knorm)
    knorm_ref[0, 0] = knorm
    kt = jnp.concatenate(ks, axis=0)
    ka_ref[0, 0] = kt.T.astype(BF16)
    ones = jnp.ones((ONES_ROWS, tm), BF16)
    for g in range(A_KV_HEADS):
        v = pt[_VA0 + g * HEAD_DIM:_VA0 + (g + 1) * HEAD_DIM].astype(BF16)
        vat_ref[0, 0, g] = jnp.concatenate([v, ones], axis=0)
    nat = jnp.dot(xb, wn_ref[0], preferred_element_type=F32)
    qn_ref[0] = (nat[:, _QN0:_QN0 + B_WIDTH] * (Q_SCALE * LOG2_E)).astype(BF16)
    kn_ref[0] = nat[:, _KN0:_KN0 + B_WIDTH].astype(BF16).reshape(tm // QBLK, QBLK, B_WIDTH)
    vn_ref[0] = nat[:, _VN0:_VN0 + B_WIDTH].astype(BF16).reshape(tm // QBLK, QBLK, B_WIDTH)


def _inproj(lidx, x, wt, wn, cos_t, sin_t, qg, kg):
    nb, s, _ = x.shape
    tm = TOKEN_TILE
    nc = s // tm
    out_shape = (
        jax.ShapeDtypeStruct((nb, A_HEADS, 2 * HEAD_DIM, s), BF16),
        jax.ShapeDtypeStruct((nb, nc, tm, KV_WIDTH), BF16),
        jax.ShapeDtypeStruct((nb, nc, A_KV_HEADS, VT_ROWS, tm), BF16),
        jax.ShapeDtypeStruct((nb, s, B_WIDTH), BF16),
        jax.ShapeDtypeStruct((nb, s // QBLK, QBLK, B_WIDTH), BF16),
        jax.ShapeDtypeStruct((nb, s // QBLK, QBLK, B_WIDTH), BF16),
        jax.ShapeDtypeStruct((nb, A_HEADS, s), F32),
        jax.ShapeDtypeStruct((nb, nc, A_HEADS, tm), F32),
    )
    grid_spec = pltpu.PrefetchScalarGridSpec(
        num_scalar_prefetch=1,
        grid=(nb, nc),
        in_specs=[
            pl.BlockSpec((1, tm, D_MODEL), lambda b, c, l: (b, c, 0)),
            pl.BlockSpec((1, _PT_ROWS, D_MODEL), lambda b, c, l: (l[0], 0, 0)),
            pl.BlockSpec((1, D_MODEL, _NAT_COLS), lambda b, c, l: (l[0], 0, 0)),
            pl.BlockSpec((HEAD_DIM // 2, tm), lambda b, c, l: (0, c)),
            pl.BlockSpec((HEAD_DIM // 2, tm), lambda b, c, l: (0, c)),
            pl.BlockSpec((1, HEAD_DIM, 1), lambda b, c, l: (l[0], 0, 0)),
            pl.BlockSpec((1, HEAD_DIM, 1), lambda b, c, l: (l[0], 0, 0)),
        ],
        out_specs=[
            pl.BlockSpec((1, A_HEADS, 2 * HEAD_DIM, tm), lambda b, c, l: (b, 0, 0, c)),
            pl.BlockSpec((1, 1, tm, KV_WIDTH), lambda b, c, l: (b, c, 0, 0)),
            pl.BlockSpec((1, 1, A_KV_HEADS, VT_ROWS, tm), lambda b, c, l: (b, c, 0, 0, 0)),
            pl.BlockSpec((1, tm, B_WIDTH), lambda b, c, l: (b, c, 0)),
            pl.BlockSpec((1, tm // QBLK, QBLK, B_WIDTH), lambda b, c, l: (b, c, 0, 0)),
            pl.BlockSpec((1, tm // QBLK, QBLK, B_WIDTH), lambda b, c, l: (b, c, 0, 0)),
            pl.BlockSpec((1, A_HEADS, tm), lambda b, c, l: (b, 0, c)),
            pl.BlockSpec((1, 1, A_HEADS, tm), lambda b, c, l: (b, c, 0, 0)),
        ],
    )
    return pl.pallas_call(
        _inproj_kernel, out_shape=out_shape, grid_spec=grid_spec,
        compiler_params=_params(("parallel", "parallel")), name="inproj",
    )(lidx, x, wt, wn, cos_t, sin_t, qg, kg)


def _col_max(st):
    slabs = [st[r:r + 128] for r in range(0, st.shape[0], 128)]
    while len(slabs) > 1:
        slabs = [jnp.maximum(slabs[i], slabs[i + 1]) for i in range(0, len(slabs), 2)]
    return jnp.max(slabs[0], axis=0, keepdims=True)


def _gattn_kernel(lidx_ref, qz_ref, k_ref, vt_ref, qn_ref, kn_ref, ga_ref, o_ref, *scratch):
    del lidx_ref
    m_refs, acc_refs = scratch[:A_HEADS], scratch[A_HEADS:]
    n_chunks = k_ref.shape[1]
    for h in range(A_HEADS):
        acc_refs[h][...] = jnp.zeros(acc_refs[h].shape, F32)

    k_norm_max = jnp.max(jnp.max(kn_ref[0], axis=0), axis=1, keepdims=True)
    row = lax.broadcasted_iota(jnp.int32, (A_HEADS, 1), 0)
    k_of_head = jnp.where(row < A_GROUP, k_norm_max[0:1], k_norm_max[1:2])
    bound = qn_ref[0] * k_of_head
    bounded = jnp.max(bound) <= SHIFT_BOUND_MAX

    def chunk_body(c, carry, online):
        kc = k_ref[0, c]
        scores = lambda h: jnp.dot(kc, qz_ref[0, h], preferred_element_type=F32)
        pending = [scores(h) for h in range(GATTN_AHEAD)]
        for h in range(A_HEADS):
            st = pending.pop(0)
            if h + GATTN_AHEAD < A_HEADS:
                pending.append(scores(h + GATTN_AHEAD))
            vt = vt_ref[0, c, h // A_GROUP]
            if online:
                m_old = m_refs[h][...]
                m_new = jnp.maximum(m_old, _col_max(st))
                alpha = jnp.exp2(m_old - m_new)
                p = jnp.exp2(st - m_new).astype(BF16)
                acc_refs[h][...] = alpha * acc_refs[h][...] + jnp.dot(vt, p, preferred_element_type=F32)
                m_refs[h][...] = m_new
            else:
                p = jnp.exp2(st - m_refs[h][...]).astype(BF16)
                acc_refs[h][...] += jnp.dot(vt, p, preferred_element_type=F32)
        return carry

    @pl.when(bounded)
    def _():
        for h in range(A_HEADS):
            m_refs[h][...] = bound[h:h + 1]
        lax.fori_loop(0, n_chunks, functools.partial(chunk_body, online=False), 0)

    @pl.when(jnp.logical_not(bounded))
    def _():
        for h in range(A_HEADS):
            m_refs[h][...] = jnp.full(m_refs[h].shape, -jnp.inf, F32)
        lax.fori_loop(0, n_chunks, functools.partial(chunk_body, online=True), 0)

    outs = []
    for h in range(A_HEADS):
        acc = acc_refs[h][...]
        outs.append(acc[0:HEAD_DIM] * (1.0 / acc[HEAD_DIM:HEAD_DIM + 1]))
    ot = jnp.concatenate(outs, axis=0)
    ms = jnp.sum(ot * ot, axis=0, keepdims=True) * (1.0 / A_WIDTH)
    o_ref[0] = (ot * lax.rsqrt(ms + LN_EPS) * ga_ref[0]).astype(BF16)


def _gattn(lidx, qz, ka, vat, qnorm, knorm, ga_col):
    nb, _, _, s = qz.shape
    nc, tk = ka.shape[1], ka.shape[2]
    tq = ATTN_Q_TILE
    grid_spec = pltpu.PrefetchScalarGridSpec(
        num_scalar_prefetch=1,
        grid=(nb, s // tq),
        in_specs=[
            pl.BlockSpec((1, A_HEADS, 2 * HEAD_DIM, tq), lambda b, i, l: (b, 0, 0, i)),
            pl.BlockSpec((1, nc, tk, KV_WIDTH), lambda b, i, l: (b, 0, 0, 0)),
            pl.BlockSpec((1, nc, A_KV_HEADS, VT_ROWS, tk), lambda b, i, l: (b, 0, 0, 0, 0)),
            pl.BlockSpec((1, A_HEADS, tq), lambda b, i, l: (b, 0, i)),
            pl.BlockSpec((1, nc, A_HEADS, tk), lambda b, i, l: (b, 0, 0, 0)),
            pl.BlockSpec((1, A_WIDTH, 1), lambda b, i, l: (l[0], 0, 0)),
        ],
        out_specs=pl.BlockSpec((1, A_WIDTH, tq), lambda b, i, l: (b, 0, i)),
        scratch_shapes=[pltpu.VMEM((1, tq), F32)] * A_HEADS + [pltpu.VMEM((VT_ROWS, tq), F32)] * A_HEADS,
    )
    return pl.pallas_call(
        _gattn_kernel, out_shape=jax.ShapeDtypeStruct((nb, A_WIDTH, s), BF16), grid_spec=grid_spec,
        compiler_params=_params(("parallel", "parallel")), name="gattn",
    )(lidx, qz, ka, vat, qnorm, knorm, ga_col)


def _nattn_kernel(*refs):
    q_ref = refs[0]
    k_refs = refs[1:1 + BAND_BLKS]
    v_refs = refs[1 + BAND_BLKS:1 + 2 * BAND_BLKS]
    bias_ref = refs[1 + 2 * BAND_BLKS]
    o_ref = refs[2 + 2 * BAND_BLKS]
    first_head = lax.broadcasted_iota(jnp.int32, (QBLK, LANES), 1) < HEAD_DIM
    dn = (((1,), (1,)), ((), ()))

    def band(refs, c):
        return jnp.concatenate([refs[t][0, 0, :, c * LANES:(c + 1) * LANES] for t in range(BAND_BLKS)], axis=0)

    def scores(h):
        c, hh = h // 2, h % 2
        qp = q_ref[0, :, c * LANES:(c + 1) * LANES]
        qm = jnp.where(first_head if hh == 0 else jnp.logical_not(first_head), qp, jnp.zeros_like(qp))
        return lax.dot_general(qm, band(k_refs, c), dn, preferred_element_type=F32) + bias_ref[0, h]

    outs = []
    pending = [scores(h) for h in range(NATTN_AHEAD)]
    for h in range(B_HEADS):
        s = pending.pop(0)
        if h + NATTN_AHEAD < B_HEADS:
            pending.append(scores(h + NATTN_AHEAD))
        m = jnp.max(s, axis=-1, keepdims=True)
        p = jnp.exp2(s - m)
        l = jnp.sum(p, axis=-1, keepdims=True)
        outs.append(jnp.dot(p.astype(BF16), band(v_refs, h // 2), preferred_element_type=F32) * (1.0 / l))
    pairs = [jnp.where(first_head, outs[2 * c], outs[2 * c + 1]) for c in range(B_HEADS // 2)]
    o_ref[0] = jnp.concatenate(pairs, axis=-1)


def _band_start_blk(j, nblk):
    rows_per_blk = QBLK // GRID_W
    rows = nblk * rows_per_blk
    bs = jnp.clip(j * rows_per_blk - WIN_R // 2, 0, rows - BAND_ROWS)
    return bs // rows_per_blk


def _bias_variant(j, nblk):
    return jnp.where(j < 2, j, jnp.where(j >= nblk - 2, j - nblk + N_BIAS_VARIANTS, 2))


def _nattn(lidx, qn, kn, vn, bias):
    nb, s, _ = qn.shape
    nblk = s // QBLK
    band_specs = [pl.BlockSpec((1, 1, QBLK, B_WIDTH),
                               functools.partial(lambda b, j, l, t: (b, _band_start_blk(j, nblk) + t, 0, 0), t=t))
                  for t in range(BAND_BLKS)]
    grid_spec = pltpu.PrefetchScalarGridSpec(
        num_scalar_prefetch=1,
        grid=(nb, nblk),
        in_specs=[pl.BlockSpec((1, QBLK, B_WIDTH), lambda b, j, l: (b, j, 0))]
        + band_specs + band_specs
        + [pl.BlockSpec((1, B_HEADS, QBLK, BAND_ROWS * GRID_W),
                        lambda b, j, l: (l[0] * N_BIAS_VARIANTS + _bias_variant(j, nblk), 0, 0, 0))],
        out_specs=pl.BlockSpec((1, QBLK, B_WIDTH), lambda b, j, l: (b, j, 0)),
    )

    def kernel(lidx_ref, *refs):
        del lidx_ref
        _nattn_kernel(*refs)

    return pl.pallas_call(
        kernel, out_shape=jax.ShapeDtypeStruct((nb, s, B_WIDTH), F32), grid_spec=grid_spec,
        compiler_params=_params(("parallel", "arbitrary")), name="nattn",
    )(lidx, qn, *([kn] * BAND_BLKS), *([vn] * BAND_BLKS), bias)


def _nattn_bias_tables(rpb):
    n_layers = rpb.shape[0]
    n_r, n_c = 2 * WIN_R - 1, 2 * WIN_C - 1
    qc = np.arange(GRID_W)[:, None]
    kc = np.arange(GRID_W)[None, :]
    ic = np.clip(kc - qc + WIN_C - 1, 0, n_c - 1)
    cs = np.clip(qc - WIN_C // 2, 0, GRID_W - WIN_C)
    col_ok = (kc >= cs) & (kc < cs + WIN_C)
    onehot = jnp.asarray((ic.reshape(-1)[None, :] == np.arange(n_c)[:, None]).astype(np.float32))
    cols = jnp.dot(rpb.astype(F32).reshape(-1, n_c), onehot, precision=lax.Precision.HIGHEST)
    cols = cols.reshape(n_layers, B_HEADS, n_r, GRID_W, GRID_W) * LOG2_E
    rows_per_blk = QBLK // GRID_W
    rows = 16
    variants = []
    for r0 in (0, 2, 4, rows - 4, rows - 2):
        bs = min(max(r0 - WIN_R // 2, 0), rows - BAND_ROWS)
        q_rows = []
        for qo in range(rows_per_blk):
            qr = r0 + qo
            rs = min(max(qr - WIN_R // 2, 0), rows - WIN_R)
            k_rows = []
            for ko in range(BAND_ROWS):
                kr = bs + ko
                ir = min(max(kr - qr + WIN_R - 1, 0), n_r - 1)
                if rs <= kr < rs + WIN_R:
                    k_rows.append(jnp.where(jnp.asarray(col_ok)[None, None], cols[:, :, ir], NEG_INF))
                else:
                    k_rows.append(jnp.full((n_layers, B_HEADS, GRID_W, GRID_W), NEG_INF, F32))
            q_rows.append(jnp.concatenate(k_rows, axis=-1))
        variants.append(jnp.concatenate(q_rows, axis=-2))
    t = jnp.stack(variants, axis=1)
    return t.reshape(n_layers * N_BIAS_VARIANTS, B_HEADS, QBLK, BAND_ROWS * GRID_W)


def _group_member(x, d, wrapped):
    return jnp.where(wrapped, pltpu.roll(x, EXPERTS_PER_GROUP - d, 0), pltpu.roll(x, N_EXPERTS - d, 0))


def _route_t(logits_t):
    mx = jnp.max(logits_t, axis=0, keepdims=True)
    ex = jnp.exp(logits_t - mx)
    probs = ex / jnp.sum(ex, axis=0, keepdims=True)
    e_idx = lax.broadcasted_iota(jnp.int32, probs.shape, 0)
    pos = e_idx % EXPERTS_PER_GROUP
    rank = jnp.zeros(probs.shape, jnp.int32)
    for d in range(1, EXPERTS_PER_GROUP):
        wrapped = pos + d >= EXPERTS_PER_GROUP
        other = _group_member(probs, d, wrapped)
        beats = (other > probs) | ((other == probs) & wrapped)
        rank = rank + beats.astype(jnp.int32)
    top = rank < 2
    kept = jnp.where(top, probs, 0.0)
    score = kept
    for d in range(1, EXPERTS_PER_GROUP):
        score = score + _group_member(kept, d, pos + d >= EXPERTS_PER_GROUP)
    grp = e_idx // EXPERTS_PER_GROUP
    lost = jnp.zeros(probs.shape, jnp.bool_)
    for d in range(1, N_GROUPS):
        other = pltpu.roll(score, N_EXPERTS - d * EXPERTS_PER_GROUP, 0)
        earlier = grp + d >= N_GROUPS
        lost = lost | (other > score) | ((other == score) & earlier)
    won = jnp.logical_not(lost)
    gates = jnp.where(top & won, probs / score, 0.0)
    group_onehot = jnp.where(won & (pos == 0), 1.0, 0.0)
    return gates, group_onehot


def _post_attn_kernel(lidx_ref, oat_ref, ob_ref, x_ref, wo_ref, gb_ref, lng_ref, lnb_ref,
                      wrh_ref, wrl_ref, br_ref, h_ref, hb_ref, gates_ref, goh_ref):
    del lidx_ref
    tm = x_ref.shape[0]
    n_part = tm // POST_ROWS
    parts = [pl.ds(r * POST_ROWS, POST_ROWS) for r in range(n_part)]
    dn = (((1,), (1,)), ((), ()))
    dn_t = (((0,), (0,)), ((), ()))

    def rms_b(rows):
        b = ob_ref[rows, :]
        return (b * lax.rsqrt(jnp.mean(b * b, axis=-1, keepdims=True) + LN_EPS) * gb_ref[0]).astype(BF16)

    nbs = [rms_b(rows) for rows in parts]
    mixes = [lax.dot_general(oat_ref[0, :, rows], wo_ref[0, 0:A_WIDTH, :], dn_t, preferred_element_type=F32)
             + jnp.dot(nb_, wo_ref[0, A_WIDTH:, :], preferred_element_type=F32) for rows, nb_ in zip(parts, nbs)]
    hs = [_layer_norm(ALPHA * x_ref[rows, :] + mix, lng_ref[0, 0:1], lnb_ref[0, 0:1]) for rows, mix in zip(parts, mixes)]
    for rows, h in zip(parts, hs):
        h_ref[rows, :] = h
        h_hi = h.astype(BF16)
        hb_ref[rows, :] = h_hi
        h_lo = (h - h_hi.astype(F32)).astype(BF16)
        logits_t = (lax.dot_general(wrh_ref[...], h_hi, dn, preferred_element_type=F32)
                    + lax.dot_general(wrh_ref[...], h_lo, dn, preferred_element_type=F32)
                    + lax.dot_general(wrl_ref[...], h_hi, dn, preferred_element_type=F32)) + br_ref[...]
        gates_t, group_onehot = _route_t(logits_t)
        goh_ref[:, rows] = group_onehot.astype(BF16)
        gates_ref[:, rows] = gates_t


def _post_attn(lidx, oat, ob, x, wo, gb, lng, lnb, wrh, wrl, br):
    t = x.shape[0]
    tm = ROW_TILE
    tiles_per_seq = oat.shape[2] // tm
    row = lambda i, l: (i, 0)
    lay = lambda i, l: (l[0], 0, 0)
    full = lambda i, l: (0, 0)
    grid_spec = pltpu.PrefetchScalarGridSpec(
        num_scalar_prefetch=1,
        grid=(t // tm,),
        in_specs=[
            pl.BlockSpec((1, A_WIDTH, tm), lambda i, l: (i // tiles_per_seq, 0, i % tiles_per_seq)),
            pl.BlockSpec((tm, B_WIDTH), row),
            pl.BlockSpec((tm, D_MODEL), row),
            pl.BlockSpec((1, D_MODEL, D_MODEL), lay),
            pl.BlockSpec((1, 1, B_WIDTH), lay),
            pl.BlockSpec((1, 3, D_MODEL), lay),
            pl.BlockSpec((1, 3, D_MODEL), lay),
            pl.BlockSpec((N_EXPERTS, D_MODEL), full),
            pl.BlockSpec((N_EXPERTS, D_MODEL), full),
            pl.BlockSpec((N_EXPERTS, 1), full),
        ],
        out_specs=[
            pl.BlockSpec((tm, D_MODEL), row),
            pl.BlockSpec((tm, D_MODEL), row),
            pl.BlockSpec((N_EXPERTS, tm), lambda i, l: (0, i)),
            pl.BlockSpec((N_EXPERTS, tm), lambda i, l: (0, i)),
        ],
    )
    return pl.pallas_call(
        _post_attn_kernel,
        out_shape=(jax.ShapeDtypeStruct((t, D_MODEL), F32),
                   jax.ShapeDtypeStruct((t, D_MODEL), BF16),
                   jax.ShapeDtypeStruct((N_EXPERTS, t), F32),
                   jax.ShapeDtypeStruct((N_EXPERTS, t), BF16)),
        grid_spec=grid_spec,
        compiler_params=_params(("parallel",)), name="post_attn",
    )(lidx, oat, ob, x, wo, gb, lng, lnb, wrh, wrl, br)


def _moe_kernel(lidx_ref, bounds_ref, hb_ref, gates_ref, goh_ref, upper_ref, wg_ref, wu_ref, wd_ref, o_ref,
                perm_ref, xs_ref, gs_ref, ys_ref):
    del lidx_ref
    i = pl.program_id(0)
    e = pl.program_id(1)
    tm = hb_ref.shape[0]
    base = i * (2 * N_GROUPS)

    @pl.when(e == 0)
    def _sort():
        goh = goh_ref[...]
        ranks = jnp.dot(goh, upper_ref[...], preferred_element_type=F32)
        row = lax.broadcasted_iota(jnp.int32, (N_EXPERTS, 1), 0)
        start = jnp.zeros((N_EXPERTS, 1), F32)
        for g in range(N_GROUPS):
            start = jnp.where(row == g * EXPERTS_PER_GROUP, bounds_ref[base + g].astype(F32), start)
        pos = jnp.sum(goh.astype(F32) * (ranks + start), axis=0, keepdims=True)
        slot = lax.broadcasted_iota(jnp.int32, (tm, tm), 0)
        perm = jnp.where(slot == pos.astype(jnp.int32), 1.0, 0.0).astype(BF16)
        perm_ref[...] = perm
        xs_ref[...] = jnp.dot(perm, hb_ref[...], preferred_element_type=F32).astype(BF16)
        gates_t = gates_ref[...]
        g_hi = gates_t.astype(BF16)
        g_lo = (gates_t - g_hi.astype(F32)).astype(BF16)
        both = lax.dot_general(jnp.concatenate([g_hi, g_lo], axis=0), perm, (((1,), (1,)), ((), ())),
                               preferred_element_type=F32)
        sorted_t = both[0:N_EXPERTS] + both[N_EXPERTS:2 * N_EXPERTS]
        gs_ref[...] = jnp.concatenate([sorted_t, jnp.zeros((LANES - N_EXPERTS, tm), F32)], axis=0).T
        ys_ref[...] = jnp.zeros(ys_ref.shape, F32)

    grp = (e * EXPERTS_PER_STEP) // EXPERTS_PER_GROUP
    first = bounds_ref[base + grp] // MOE_SUB
    stop = (bounds_ref[base + N_GROUPS + grp] + MOE_SUB - 1) // MOE_SUB

    def experts(row0, n_rows):
        rows = pl.ds(pl.multiple_of(row0, MOE_SUB), n_rows)
        lane = lax.broadcasted_iota(jnp.int32, (n_rows, LANES), 1)
        x = xs_ref[rows, :]
        gs = gs_ref[rows, :]
        total = None
        for j in range(EXPERTS_PER_STEP):
            gate = jnp.dot(x, wg_ref[0, 0, j], preferred_element_type=F32)
            up = jnp.dot(x, wu_ref[0, 0, j], preferred_element_type=F32)
            act = (gate * jax.nn.sigmoid(gate) * up).astype(BF16)
            y = jnp.dot(act, wd_ref[0, 0, j], preferred_element_type=F32)
            g_col = jnp.sum(jnp.where(lane == e * EXPERTS_PER_STEP + j, gs, 0.0), axis=1, keepdims=True)
            total = g_col * y if total is None else total + g_col * y
        ys_ref[rows, :] += total

    n_pairs = (stop - first) // 2

    def pair_body(s, carry):
        experts((first + 2 * s) * MOE_SUB, 2 * MOE_SUB)
        return carry

    lax.fori_loop(0, n_pairs, pair_body, 0)

    @pl.when(first + 2 * n_pairs < stop)
    def _():
        experts((stop - 1) * MOE_SUB, MOE_SUB)

    @pl.when(e == N_EXPERTS // EXPERTS_PER_STEP - 1)
    def _unsort():
        o_ref[...] = lax.dot_general(perm_ref[...], ys_ref[...].astype(BF16), (((0,), (0,)), ((), ())),
                                     preferred_element_type=F32)


def _tile_bounds(goh, tm):
    nt = goh.shape[1] // tm
    cnt = goh[::EXPERTS_PER_GROUP].astype(F32).reshape(N_GROUPS, nt, tm).sum(-1).T.astype(jnp.int32)
    end = jnp.cumsum(cnt, axis=1)
    return jnp.concatenate([end - cnt, end], axis=1).reshape(-1)


def _moe(lidx, hb, gates, goh, wg, wu, wd):
    t = hb.shape[0]
    tm = min(MOE_TILE, t)
    bounds = _tile_bounds(goh, tm)
    n_steps = N_EXPERTS // EXPERTS_PER_STEP
    wg, wu, wd = (w.reshape(w.shape[0], n_steps, EXPERTS_PER_STEP, *w.shape[2:]) for w in (wg, wu, wd))
    upper = (jnp.arange(tm)[:, None] < jnp.arange(tm)[None, :]).astype(BF16)
    grid_spec = pltpu.PrefetchScalarGridSpec(
        num_scalar_prefetch=2,
        grid=(t // tm, n_steps),
        in_specs=[
            pl.BlockSpec((tm, D_MODEL), lambda i, e, l, b: (i, 0)),
            pl.BlockSpec((N_EXPERTS, tm), lambda i, e, l, b: (0, i)),
            pl.BlockSpec((N_EXPERTS, tm), lambda i, e, l, b: (0, i)),
            pl.BlockSpec((tm, tm), lambda i, e, l, b: (0, 0)),
            pl.BlockSpec((1, 1, EXPERTS_PER_STEP, D_MODEL, D_FF_EXPERT), lambda i, e, l, b: (l[0], e, 0, 0, 0)),
            pl.BlockSpec((1, 1, EXPERTS_PER_STEP, D_MODEL, D_FF_EXPERT), lambda i, e, l, b: (l[0], e, 0, 0, 0)),
            pl.BlockSpec((1, 1, EXPERTS_PER_STEP, D_FF_EXPERT, D_MODEL), lambda i, e, l, b: (l[0], e, 0, 0, 0)),
        ],
        out_specs=pl.BlockSpec((tm, D_MODEL), lambda i, e, l, b: (i, 0)),
        scratch_shapes=[pltpu.VMEM((tm, tm), BF16), pltpu.VMEM((tm, D_MODEL), BF16),
                        pltpu.VMEM((tm, LANES), F32), pltpu.VMEM((tm, D_MODEL), F32)],
    )
    return pl.pallas_call(
        _moe_kernel, out_shape=jax.ShapeDtypeStruct((t, D_MODEL), F32), grid_spec=grid_spec,
        compiler_params=_params(("parallel", "arbitrary")), name="moe",
    )(lidx, bounds, hb, gates, goh, upper, wg, wu, wd)


def _post_moe_kernel(lidx_ref, h_ref, f_ref, p_ref, wpg_ref, bpg_ref, wpp_ref, lng_ref, lnb_ref, x_ref):
    del lidx_ref
    parts = [pl.ds(r * POST_ROWS, POST_ROWS) for r in range(h_ref.shape[0] // POST_ROWS)]
    h2s = [_layer_norm(ALPHA * h_ref[rows, :] + f_ref[rows, :], lng_ref[0, 1:2], lnb_ref[0, 1:2]) for rows in parts]
    zs = [jnp.dot(h2.astype(BF16), wpg_ref[0], preferred_element_type=F32) + bpg_ref[0] for h2 in h2s]
    projs = [jnp.dot(p_ref[0, rows, :].astype(BF16), wpp_ref[0], preferred_element_type=F32) for rows in parts]
    for rows, h2, z, proj in zip(parts, h2s, zs, projs):
        pe = jax.nn.sigmoid(z) * proj
        x_ref[rows, :] = _layer_norm(ALPHA * h2 + pe, lng_ref[0, 2:3], lnb_ref[0, 2:3])


def _post_moe(lidx, h, ffn, p, wpg, bpg, wpp, lng, lnb):
    t = h.shape[0]
    tm = ROW_TILE
    row = lambda i, l: (i, 0)
    lay = lambda i, l: (l[0], 0, 0)
    grid_spec = pltpu.PrefetchScalarGridSpec(
        num_scalar_prefetch=1,
        grid=(t // tm,),
        in_specs=[
            pl.BlockSpec((tm, D_MODEL), row),
            pl.BlockSpec((tm, D_MODEL), row),
            pl.BlockSpec((1, tm, PLE_DIM), lambda i, l: (l[0], i, 0)),
            pl.BlockSpec((1, D_MODEL, D_MODEL), lay),
            pl.BlockSpec((1, 1, D_MODEL), lay),
            pl.BlockSpec((1, PLE_DIM, D_MODEL), lay),
            pl.BlockSpec((1, 3, D_MODEL), lay),
            pl.BlockSpec((1, 3, D_MODEL), lay),
        ],
        out_specs=pl.BlockSpec((tm, D_MODEL), row),
    )
    return pl.pallas_call(
        _post_moe_kernel, out_shape=jax.ShapeDtypeStruct((t, D_MODEL), F32), grid_spec=grid_spec,
        compiler_params=_params(("parallel",)), name="post_moe",
    )(lidx, h, ffn, p, wpg, bpg, wpp, lng, lnb)


def _rope_tables_t(s):
    t = jnp.arange(s, dtype=jnp.int32)
    row = (t // GRID_W).astype(F32)
    col = (t % GRID_W).astype(F32)
    n_freq = HEAD_DIM // 4
    inv_freq = ROPE_THETA ** (-jnp.arange(n_freq, dtype=F32) / n_freq)
    ang = jnp.concatenate([inv_freq[:, None] * row[None, :], inv_freq[:, None] * col[None, :]], axis=0)
    return jnp.cos(ang), jnp.sin(ang)


def _prepare_weights(w_in, q_norm_a, k_norm_a, rpb_b, g_out_a, g_out_b, w_out, w_router, b_router,
                     w_gate, w_up, w_down, w_ple_gate, b_ple_gate, w_ple_proj):
    n_layers = w_in.shape[0]
    qa, ka, va, qn, kn, vn = jnp.split(w_in, [512, 640, 768, 1280, 1792], axis=-1)
    wt = jnp.swapaxes(jnp.concatenate([qa, ka, va], axis=-1), 1, 2).astype(BF16)
    wn = jnp.concatenate([qn, kn, vn], axis=-1).astype(BF16)
    wr_t = w_router.T
    wr_hi = wr_t.astype(BF16)
    wr_lo = (wr_t - wr_hi.astype(F32)).astype(BF16)
    return dict(
        wt=wt, wn=wn,
        qg=q_norm_a.reshape(n_layers, HEAD_DIM, 1), kg=k_norm_a.reshape(n_layers, HEAD_DIM, 1),
        bias=_nattn_bias_tables(rpb_b),
        ga=g_out_a.reshape(n_layers, A_WIDTH, 1), gb=g_out_b.reshape(n_layers, 1, B_WIDTH),
        wo=w_out.astype(BF16), wr_hi=wr_hi, wr_lo=wr_lo, br=b_router.reshape(N_EXPERTS, 1),
        wg=w_gate.astype(BF16), wu=w_up.astype(BF16), wd=w_down.astype(BF16),
        wpg=w_ple_gate.astype(BF16), bpg=b_ple_gate.reshape(n_layers, 1, D_MODEL), wpp=w_ple_proj.astype(BF16),
    )


def _layer(lidx, x, p_flat, w, ln_g, ln_b, rope):
    nb, s, _ = x.shape
    qz, ka, vat, qn, kn, vn, qnorm, knorm = _inproj(lidx, x, w["wt"], w["wn"], rope[0], rope[1], w["qg"], w["kg"])
    oat = _gattn(lidx, qz, ka, vat, qnorm, knorm, w["ga"])
    ob = _nattn(lidx, qn, kn, vn, w["bias"])
    t = nb * s
    h, hb, gates, goh = _post_attn(lidx, oat, ob.reshape(t, B_WIDTH), x.reshape(t, D_MODEL),
                                   w["wo"], w["gb"], ln_g, ln_b, w["wr_hi"], w["wr_lo"], w["br"])
    ffn = _moe(lidx, hb, gates, goh, w["wg"], w["wu"], w["wd"])
    xo = _post_moe(lidx, h, ffn, p_flat, w["wpg"], w["bpg"], w["wpp"], ln_g, ln_b)
    return xo.reshape(nb, s, D_MODEL)


def kernel(x_prompt, x_sample, p_prompt, p_sample, w_in, q_norm_a, k_norm_a, rpb_b, g_out_a, g_out_b, w_out, ln_g, ln_b, w_router, b_router, w_gate, w_up, w_down, w_ple_gate, b_ple_gate, w_ple_proj):
    n_layers = w_in.shape[0]
    w = _prepare_weights(w_in, q_norm_a, k_norm_a, rpb_b, g_out_a, g_out_b, w_out, w_router, b_router,
                         w_gate, w_up, w_down, w_ple_gate, b_ple_gate, w_ple_proj)
    trunks = []
    for x, p in ((x_prompt, p_prompt), (x_sample, p_sample)):
        nb, s, _ = x.shape
        trunks.append((x, p.reshape(n_layers, nb * s, PLE_DIM), _rope_tables_t(s)))

    outs = []
    for x, p_flat, rope in trunks:
        for layer in range(n_layers):
            x = _layer(jnp.full((1,), layer, jnp.int32), x, p_flat, w, ln_g, ln_b, rope)
        outs.append(x)
    return tuple(outs)
```

```python
import functools
import math

import jax
import jax.numpy as jnp
import numpy as np
from jax import lax
from jax.experimental import pallas as pl
from jax.experimental.pallas import tpu as pltpu

F32 = jnp.float32
BF16 = jnp.bfloat16

D_MODEL = 1024
N_LAYERS = 4
HEAD_DIM = 64
A_HEADS = 8
A_KV_HEADS = 2
A_GROUP = A_HEADS // A_KV_HEADS
B_HEADS = 8
A_WIDTH = A_HEADS * HEAD_DIM
KV_WIDTH = A_KV_HEADS * HEAD_DIM
B_WIDTH = B_HEADS * HEAD_DIM
GRID_W = 64
QBLK = 128
WIN_R = 8
WIN_C = 16
ROPE_THETA = 10000.0
N_EXPERTS = 16
N_GROUPS = 4
EXPERTS_PER_GROUP = N_EXPERTS // N_GROUPS
D_FF_EXPERT = 512
PLE_DIM = 256
LN_EPS = 1e-5
QK_EPS = 1e-6
NEG_INF = -1e30
ALPHA = (2 * N_LAYERS) ** 0.25
Q_SCALE = HEAD_DIM ** -0.5
LOG2_E = math.log2(math.e)

LANES = 128

TOKEN_TILE = 1024
ATTN_Q_TILE = 512
ROW_TILE = 1024
POST_ROWS = 128
MOE_TILE = 1024
MOE_SUB = 64
MOE_MAX_UNITS = 6
MOE_TAIL_WINDOWS = ((0, 2), (2, 4), (4, 5))
MOE_PAD_ROWS = MOE_SUB
EXPERTS_PER_STEP = 4
BAND_ROWS = 10
BAND_BLKS = BAND_ROWS * GRID_W // QBLK
N_BIAS_VARIANTS = 5
NATTN_AHEAD = 3
GATTN_AHEAD = 1
VMEM_LIMIT = 56 * 1024 * 1024

_QA0, _KA0, _VA0, _PT_ROWS = 0, 512, 640, 768
_QN0, _KN0, _VN0, _NAT_COLS = 0, 512, 1024, 1536
ONES_ROWS = 16
VT_ROWS = HEAD_DIM + ONES_ROWS
SHIFT_BOUND_MAX = 40.0


def _params(sem):
    return pltpu.CompilerParams(dimension_semantics=sem, vmem_limit_bytes=VMEM_LIMIT)


def _layer_norm(y, g, b):
    mu = jnp.mean(y, axis=-1, keepdims=True)
    d = y - mu
    var = jnp.mean(d * d, axis=-1, keepdims=True)
    return d * lax.rsqrt(var + LN_EPS) * g + b


def _rope_t(y, cos, sin):
    cr, cc = cos[0:16], cos[16:32]
    sr, sc = sin[0:16], sin[16:32]
    a1, a2, b1, b2 = y[0:16], y[16:32], y[32:48], y[48:64]
    return jnp.concatenate([a1 * cr - a2 * sr, a1 * sr + a2 * cr,
                            b1 * cc - b2 * sc, b1 * sc + b2 * cc], axis=0)


def _norm_rope_t(y, gain, cos, sin):
    ms = jnp.sum(y * y, axis=0, keepdims=True) * (1.0 / HEAD_DIM)
    return _rope_t(y * lax.rsqrt(ms + QK_EPS) * gain, cos, sin)


def _inproj_kernel(lidx_ref, x_ref, wt_ref, wn_ref, cos_ref, sin_ref, qg_ref, kg_ref,
                   qz_ref, ka_ref, vat_ref, qn_ref, kn_ref, vn_ref, qnorm_ref, knorm_ref):
    del lidx_ref
    xb = x_ref[0].astype(BF16)
    pt = lax.dot_general(wt_ref[0], xb, (((1,), (1,)), ((), ())),
                         preferred_element_type=F32)
    tm = xb.shape[0]
    cos = cos_ref[...]
    sin = sin_ref[...]
    qg = qg_ref[0] * (Q_SCALE * LOG2_E)
    kg = kg_ref[0]
    zeros = jnp.zeros((HEAD_DIM, tm), BF16)
    head_row = lax.broadcasted_iota(jnp.int32, (A_HEADS, 1), 0)
    l2 = lambda y: jnp.sqrt(jnp.sum(y * y, axis=0, keepdims=True))
    qnorm = jnp.zeros((A_HEADS, tm), F32)
    for h in range(A_HEADS):
        qf = _norm_rope_t(pt[_QA0 + h * HEAD_DIM:_QA0 + (h + 1) * HEAD_DIM], qg, cos, sin)
        qnorm = jnp.where(head_row == h, l2(qf), qnorm)
        q = qf.astype(BF16)
        if h // A_GROUP == 0:
            qz_ref[0, h] = jnp.concatenate([q, zeros], axis=0)
        else:
            qz_ref[0, h] = jnp.concatenate([zeros, q], axis=0)
    qnorm_ref[0] = qnorm
    ks = [_norm_rope_t(pt[_KA0 + g * HEAD_DIM:_KA0 + (g + 1) * HEAD_DIM], kg, cos, sin) for g in range(A_KV_HEADS)]
    knorm = jnp.zeros((A_HEADS, tm), F32)
    for g in range(A_KV_HEADS):
        knorm = jnp.where(head_row == g, l2(ks[g]), knorm)
    knorm_ref[0, 0] = knorm
    kt = jnp.concatenate(ks, axis=0)
    ka_ref[0, 0] = kt.T.astype(BF16)
    ones = jnp.ones((ONES_ROWS, tm), BF16)
    for g in range(A_KV_HEADS):
        v = pt[_VA0 + g * HEAD_DIM:_VA0 + (g + 1) * HEAD_DIM].astype(BF16)
        vat_ref[0, 0, g] = jnp.concatenate([v, ones], axis=0)
    nat = jnp.dot(xb, wn_ref[0], preferred_element_type=F32)
    qn_ref[0] = (nat[:, _QN0:_QN0 + B_WIDTH] * (Q_SCALE * LOG2_E)).astype(BF16)
    kn_ref[0] = nat[:, _KN0:_KN0 + B_WIDTH].astype(BF16).reshape(tm // QBLK, QBLK, B_WIDTH)
    vn_ref[0] = nat[:, _VN0:_VN0 + B_WIDTH].astype(BF16).reshape(tm // QBLK, QBLK, B_WIDTH)


def _inproj(lidx, x, wt, wn, cos_t, sin_t, qg, kg):
    nb, s, _ = x.shape
    tm = TOKEN_TILE
    nc = s // tm
    out_shape = (
        jax.ShapeDtypeStruct((nb, A_HEADS, 2 * HEAD_DIM, s), BF16),
        jax.ShapeDtypeStruct((nb, nc, tm, KV_WIDTH), BF16),
        jax.ShapeDtypeStruct((nb, nc, A_KV_HEADS, VT_ROWS, tm), BF16),
        jax.ShapeDtypeStruct((nb, s, B_WIDTH), BF16),
        jax.ShapeDtypeStruct((nb, s // QBLK, QBLK, B_WIDTH), BF16),
        jax.ShapeDtypeStruct((nb, s // QBLK, QBLK, B_WIDTH), BF16),
        jax.ShapeDtypeStruct((nb, A_HEADS, s), F32),
        jax.ShapeDtypeStruct((nb, nc, A_HEADS, tm), F32),
    )
    grid_spec = pltpu.PrefetchScalarGridSpec(
        num_scalar_prefetch=1,
        grid=(nb, nc),
        in_specs=[
            pl.BlockSpec((1, tm, D_MODEL), lambda b, c, l: (b, c, 0)),
            pl.BlockSpec((1, _PT_ROWS, D_MODEL), lambda b, c, l: (l[0], 0, 0)),
            pl.BlockSpec((1, D_MODEL, _NAT_COLS), lambda b, c, l: (l[0], 0, 0)),
            pl.BlockSpec((HEAD_DIM // 2, tm), lambda b, c, l: (0, c)),
            pl.BlockSpec((HEAD_DIM // 2, tm), lambda b, c, l: (0, c)),
            pl.BlockSpec((1, HEAD_DIM, 1), lambda b, c, l: (l[0], 0, 0)),
            pl.BlockSpec((1, HEAD_DIM, 1), lambda b, c, l: (l[0], 0, 0)),
        ],
        out_specs=[
            pl.BlockSpec((1, A_HEADS, 2 * HEAD_DIM, tm), lambda b, c, l: (b, 0, 0, c)),
            pl.BlockSpec((1, 1, tm, KV_WIDTH), lambda b, c, l: (b, c, 0, 0)),
            pl.BlockSpec((1, 1, A_KV_HEADS, VT_ROWS, tm), lambda b, c, l: (b, c, 0, 0, 0)),
            pl.BlockSpec((1, tm, B_WIDTH), lambda b, c, l: (b, c, 0)),
            pl.BlockSpec((1, tm // QBLK, QBLK, B_WIDTH), lambda b, c, l: (b, c, 0, 0)),
            pl.BlockSpec((1, tm // QBLK, QBLK, B_WIDTH), lambda b, c, l: (b, c, 0, 0)),
            pl.BlockSpec((1, A_HEADS, tm), lambda b, c, l: (b, 0, c)),
            pl.BlockSpec((1, 1, A_HEADS, tm), lambda b, c, l: (b, c, 0, 0)),
        ],
    )
    return pl.pallas_call(
        _inproj_kernel, out_shape=out_shape, grid_spec=grid_spec,
        compiler_params=_params(("parallel", "parallel")), name="inproj",
    )(lidx, x, wt, wn, cos_t, sin_t, qg, kg)


def _col_max(st):
    slabs = [st[r:r + 128] for r in range(0, st.shape[0], 128)]
    while len(slabs) > 1:
        slabs = [jnp.maximum(slabs[i], slabs[i + 1]) for i in range(0, len(slabs), 2)]
    return jnp.max(slabs[0], axis=0, keepdims=True)


def _gattn_kernel(lidx_ref, qz_ref, k_ref, vt_ref, qn_ref, kn_ref, ga_ref, o_ref, *scratch):
    del lidx_ref
    m_refs, acc_refs = scratch[:A_HEADS], scratch[A_HEADS:]
    n_chunks = k_ref.shape[1]
    for h in range(A_HEADS):
        acc_refs[h][...] = jnp.zeros(acc_refs[h].shape, F32)

    k_norm_max = jnp.max(jnp.max(kn_ref[0], axis=0), axis=1, keepdims=True)
    row = lax.broadcasted_iota(jnp.int32, (A_HEADS, 1), 0)
    k_of_head = jnp.where(row < A_GROUP, k_norm_max[0:1], k_norm_max[1:2])
    bound = qn_ref[0] * k_of_head
    bounded = jnp.max(bound) <= SHIFT_BOUND_MAX

    def chunk_body(c, carry, online):
        kc = k_ref[0, c]
        scores = lambda h: jnp.dot(kc, qz_ref[0, h], preferred_element_type=F32)
        pending = [scores(h) for h in range(GATTN_AHEAD)]
        for h in range(A_HEADS):
            st = pending.pop(0)
            if h + GATTN_AHEAD < A_HEADS:
                pending.append(scores(h + GATTN_AHEAD))
            vt = vt_ref[0, c, h // A_GROUP]
            if online:
                m_old = m_refs[h][...]
                m_new = jnp.maximum(m_old, _col_max(st))
                alpha = jnp.exp2(m_old - m_new)
                p = jnp.exp2(st - m_new).astype(BF16)
                acc_refs[h][...] = alpha * acc_refs[h][...] + jnp.dot(vt, p, preferred_element_type=F32)
                m_refs[h][...] = m_new
            else:
                p = jnp.exp2(st - m_refs[h][...]).astype(BF16)
                acc_refs[h][...] += jnp.dot(vt, p, preferred_element_type=F32)
        return carry

    @pl.when(bounded)
    def _():
        for h in range(A_HEADS):
            m_refs[h][...] = bound[h:h + 1]
        lax.fori_loop(0, n_chunks, functools.partial(chunk_body, online=False), 0)

    @pl.when(jnp.logical_not(bounded))
    def _():
        for h in range(A_HEADS):
            m_refs[h][...] = jnp.full(m_refs[h].shape, -jnp.inf, F32)
        lax.fori_loop(0, n_chunks, functools.partial(chunk_body, online=True), 0)

    outs = []
    for h in range(A_HEADS):
        acc = acc_refs[h][...]
        outs.append(acc[0:HEAD_DIM] * (1.0 / acc[HEAD_DIM:HEAD_DIM + 1]))
    ot = jnp.concatenate(outs, axis=0)
    ms = jnp.sum(ot * ot, axis=0, keepdims=True) * (1.0 / A_WIDTH)
    o_ref[0] = (ot * lax.rsqrt(ms + LN_EPS) * ga_ref[0]).astype(BF16)


def _gattn(lidx, qz, ka, vat, qnorm, knorm, ga_col):
    nb, _, _, s = qz.shape
    nc, tk = ka.shape[1], ka.shape[2]
    tq = ATTN_Q_TILE
    grid_spec = pltpu.PrefetchScalarGridSpec(
        num_scalar_prefetch=1,
        grid=(nb, s // tq),
        in_specs=[
            pl.BlockSpec((1, A_HEADS, 2 * HEAD_DIM, tq), lambda b, i, l: (b, 0, 0, i)),
            pl.BlockSpec((1, nc, tk, KV_WIDTH), lambda b, i, l: (b, 0, 0, 0)),
            pl.BlockSpec((1, nc, A_KV_HEADS, VT_ROWS, tk), lambda b, i, l: (b, 0, 0, 0, 0)),
            pl.BlockSpec((1, A_HEADS, tq), lambda b, i, l: (b, 0, i)),
            pl.BlockSpec((1, nc, A_HEADS, tk), lambda b, i, l: (b, 0, 0, 0)),
            pl.BlockSpec((1, A_WIDTH, 1), lambda b, i, l: (l[0], 0, 0)),
        ],
        out_specs=pl.BlockSpec((1, A_WIDTH, tq), lambda b, i, l: (b, 0, i)),
        scratch_shapes=[pltpu.VMEM((1, tq), F32)] * A_HEADS + [pltpu.VMEM((VT_ROWS, tq), F32)] * A_HEADS,
    )
    return pl.pallas_call(
        _gattn_kernel, out_shape=jax.ShapeDtypeStruct((nb, A_WIDTH, s), BF16), grid_spec=grid_spec,
        compiler_params=_params(("parallel", "parallel")), name="gattn",
    )(lidx, qz, ka, vat, qnorm, knorm, ga_col)


def _nattn_kernel(*refs):
    q_ref = refs[0]
    k_refs = refs[1:1 + BAND_BLKS]
    v_refs = refs[1 + BAND_BLKS:1 + 2 * BAND_BLKS]
    bias_ref = refs[1 + 2 * BAND_BLKS]
    o_ref = refs[2 + 2 * BAND_BLKS]
    first_head = lax.broadcasted_iota(jnp.int32, (QBLK, LANES), 1) < HEAD_DIM
    dn = (((1,), (1,)), ((), ()))

    def band(refs, c):
        return jnp.concatenate([refs[t][0, 0, :, c * LANES:(c + 1) * LANES] for t in range(BAND_BLKS)], axis=0)

    def scores(h):
        c, hh = h // 2, h % 2
        qp = q_ref[0, :, c * LANES:(c + 1) * LANES]
        qm = jnp.where(first_head if hh == 0 else jnp.logical_not(first_head), qp, jnp.zeros_like(qp))
        return lax.dot_general(qm, band(k_refs, c), dn, preferred_element_type=F32) + bias_ref[0, h]

    outs = []
    pending = [scores(h) for h in range(NATTN_AHEAD)]
    for h in range(B_HEADS):
        s = pending.pop(0)
        if h + NATTN_AHEAD < B_HEADS:
            pending.append(scores(h + NATTN_AHEAD))
        m = jnp.max(s, axis=-1, keepdims=True)
        p = jnp.exp2(s - m)
        l = jnp.sum(p, axis=-1, keepdims=True)
        outs.append(jnp.dot(p.astype(BF16), band(v_refs, h // 2), preferred_element_type=F32) * (1.0 / l))
    pairs = [jnp.where(first_head, outs[2 * c], outs[2 * c + 1]) for c in range(B_HEADS // 2)]
    o_ref[0] = jnp.concatenate(pairs, axis=-1)


def _band_start_blk(j, nblk):
    rows_per_blk = QBLK // GRID_W
    rows = nblk * rows_per_blk
    bs = jnp.clip(j * rows_per_blk - WIN_R // 2, 0, rows - BAND_ROWS)
    return bs // rows_per_blk


def _bias_variant(j, nblk):
    return jnp.where(j < 2, j, jnp.where(j >= nblk - 2, j - nblk + N_BIAS_VARIANTS, 2))


def _nattn(lidx, qn, kn, vn, bias):
    nb, s, _ = qn.shape
    nblk = s // QBLK
    band_specs = [pl.BlockSpec((1, 1, QBLK, B_WIDTH),
                               functools.partial(lambda b, j, l, t: (b, _band_start_blk(j, nblk) + t, 0, 0), t=t))
                  for t in range(BAND_BLKS)]
    grid_spec = pltpu.PrefetchScalarGridSpec(
        num_scalar_prefetch=1,
        grid=(nb, nblk),
        in_specs=[pl.BlockSpec((1, QBLK, B_WIDTH), lambda b, j, l: (b, j, 0))]
        + band_specs + band_specs
        + [pl.BlockSpec((1, B_HEADS, QBLK, BAND_ROWS * GRID_W),
                        lambda b, j, l: (l[0] * N_BIAS_VARIANTS + _bias_variant(j, nblk), 0, 0, 0))],
        out_specs=pl.BlockSpec((1, QBLK, B_WIDTH), lambda b, j, l: (b, j, 0)),
    )

    def kernel(lidx_ref, *refs):
        del lidx_ref
        _nattn_kernel(*refs)

    return pl.pallas_call(
        kernel, out_shape=jax.ShapeDtypeStruct((nb, s, B_WIDTH), F32), grid_spec=grid_spec,
        compiler_params=_params(("parallel", "arbitrary")), name="nattn",
    )(lidx, qn, *([kn] * BAND_BLKS), *([vn] * BAND_BLKS), bias)


def _nattn_bias_tables(rpb):
    n_layers = rpb.shape[0]
    n_r, n_c = 2 * WIN_R - 1, 2 * WIN_C - 1
    qc = np.arange(GRID_W)[:, None]
    kc = np.arange(GRID_W)[None, :]
    ic = np.clip(kc - qc + WIN_C - 1, 0, n_c - 1)
    cs = np.clip(qc - WIN_C // 2, 0, GRID_W - WIN_C)
    col_ok = (kc >= cs) & (kc < cs + WIN_C)
    onehot = jnp.asarray((ic.reshape(-1)[None, :] == np.arange(n_c)[:, None]).astype(np.float32))
    cols = jnp.dot(rpb.astype(F32).reshape(-1, n_c), onehot, precision=lax.Precision.HIGHEST)
    cols = cols.reshape(n_layers, B_HEADS, n_r, GRID_W, GRID_W) * LOG2_E
    rows_per_blk = QBLK // GRID_W
    rows = 16
    variants = []
    for r0 in (0, 2, 4, rows - 4, rows - 2):
        bs = min(max(r0 - WIN_R // 2, 0), rows - BAND_ROWS)
        q_rows = []
        for qo in range(rows_per_blk):
            qr = r0 + qo
            rs = min(max(qr - WIN_R // 2, 0), rows - WIN_R)
            k_rows = []
            for ko in range(BAND_ROWS):
                kr = bs + ko
                ir = min(max(kr - qr + WIN_R - 1, 0), n_r - 1)
                if rs <= kr < rs + WIN_R:
                    k_rows.append(jnp.where(jnp.asarray(col_ok)[None, None], cols[:, :, ir], NEG_INF))
                else:
                    k_rows.append(jnp.full((n_layers, B_HEADS, GRID_W, GRID_W), NEG_INF, F32))
            q_rows.append(jnp.concatenate(k_rows, axis=-1))
        variants.append(jnp.concatenate(q_rows, axis=-2))
    t = jnp.stack(variants, axis=1)
    return t.reshape(n_layers * N_BIAS_VARIANTS, B_HEADS, QBLK, BAND_ROWS * GRID_W)


def _group_member(x, d, wrapped):
    return jnp.where(wrapped, pltpu.roll(x, EXPERTS_PER_GROUP - d, 0), pltpu.roll(x, N_EXPERTS - d, 0))


def _route_t(logits_t):
    mx = jnp.max(logits_t, axis=0, keepdims=True)
    ex = jnp.exp(logits_t - mx)
    probs = ex / jnp.sum(ex, axis=0, keepdims=True)
    e_idx = lax.broadcasted_iota(jnp.int32, probs.shape, 0)
    pos = e_idx % EXPERTS_PER_GROUP
    rank = jnp.zeros(probs.shape, jnp.int32)
    for d in range(1, EXPERTS_PER_GROUP):
        wrapped = pos + d >= EXPERTS_PER_GROUP
        other = _group_member(probs, d, wrapped)
        beats = (other > probs) | ((other == probs) & wrapped)
        rank = rank + beats.astype(jnp.int32)
    top = rank < 2
    kept = jnp.where(top, probs, 0.0)
    score = kept
    for d in range(1, EXPERTS_PER_GROUP):
        score = score + _group_member(kept, d, pos + d >= EXPERTS_PER_GROUP)
    grp = e_idx // EXPERTS_PER_GROUP
    lost = jnp.zeros(probs.shape, jnp.bool_)
    for d in range(1, N_GROUPS):
        other = pltpu.roll(score, N_EXPERTS - d * EXPERTS_PER_GROUP, 0)
        earlier = grp + d >= N_GROUPS
        lost = lost | (other > score) | ((other == score) & earlier)
    won = jnp.logical_not(lost)
    gates = jnp.where(top & won, probs / score, 0.0)
    group_onehot = jnp.where(won & (pos == 0), 1.0, 0.0)
    return gates, group_onehot


def _post_attn_kernel(lidx_ref, oat_ref, ob_ref, x_ref, wo_ref, gb_ref, lng_ref, lnb_ref,
                      wrh_ref, wrl_ref, br_ref, h_ref, hb_ref, gates_ref, goh_ref):
    del lidx_ref
    tm = x_ref.shape[0]
    n_part = tm // POST_ROWS
    parts = [pl.ds(r * POST_ROWS, POST_ROWS) for r in range(n_part)]
    dn = (((1,), (1,)), ((), ()))
    dn_t = (((0,), (0,)), ((), ()))

    def rms_b(rows):
        b = ob_ref[rows, :]
        return (b * lax.rsqrt(jnp.mean(b * b, axis=-1, keepdims=True) + LN_EPS) * gb_ref[0]).astype(BF16)

    nbs = [rms_b(rows) for rows in parts]
    mixes = [lax.dot_general(oat_ref[0, :, rows], wo_ref[0, 0:A_WIDTH, :], dn_t, preferred_element_type=F32)
             + jnp.dot(nb_, wo_ref[0, A_WIDTH:, :], preferred_element_type=F32) for rows, nb_ in zip(parts, nbs)]
    hs = [_layer_norm(ALPHA * x_ref[rows, :] + mix, lng_ref[0, 0:1], lnb_ref[0, 0:1]) for rows, mix in zip(parts, mixes)]
    for rows, h in zip(parts, hs):
        h_ref[rows, :] = h
        h_hi = h.astype(BF16)
        hb_ref[rows, :] = h_hi
        h_lo = (h - h_hi.astype(F32)).astype(BF16)
        logits_t = (lax.dot_general(wrh_ref[...], h_hi, dn, preferred_element_type=F32)
                    + lax.dot_general(wrh_ref[...], h_lo, dn, preferred_element_type=F32)
                    + lax.dot_general(wrl_ref[...], h_hi, dn, preferred_element_type=F32)) + br_ref[...]
        gates_t, group_onehot = _route_t(logits_t)
        goh_ref[:, rows] = group_onehot.astype(BF16)
        gates_ref[:, rows] = gates_t


def _post_attn(lidx, oat, ob, x, wo, gb, lng, lnb, wrh, wrl, br):
    t = x.shape[0]
    tm = ROW_TILE
    tiles_per_seq = oat.shape[2] // tm
    row = lambda i, l: (i, 0)
    lay = lambda i, l: (l[0], 0, 0)
    full = lambda i, l: (0, 0)
    grid_spec = pltpu.PrefetchScalarGridSpec(
        num_scalar_prefetch=1,
        grid=(t // tm,),
        in_specs=[
            pl.BlockSpec((1, A_WIDTH, tm), lambda i, l: (i // tiles_per_seq, 0, i % tiles_per_seq)),
            pl.BlockSpec((tm, B_WIDTH), row),
            pl.BlockSpec((tm, D_MODEL), row),
            pl.BlockSpec((1, D_MODEL, D_MODEL), lay),
            pl.BlockSpec((1, 1, B_WIDTH), lay),
            pl.BlockSpec((1, 3, D_MODEL), lay),
            pl.BlockSpec((1, 3, D_MODEL), lay),
            pl.BlockSpec((N_EXPERTS, D_MODEL), full),
            pl.BlockSpec((N_EXPERTS, D_MODEL), full),
            pl.BlockSpec((N_EXPERTS, 1), full),
        ],
        out_specs=[
            pl.BlockSpec((tm, D_MODEL), row),
            pl.BlockSpec((tm, D_MODEL), row),
            pl.BlockSpec((N_EXPERTS, tm), lambda i, l: (0, i)),
            pl.BlockSpec((N_EXPERTS, tm), lambda i, l: (0, i)),
        ],
    )
    return pl.pallas_call(
        _post_attn_kernel,
        out_shape=(jax.ShapeDtypeStruct((t, D_MODEL), F32),
                   jax.ShapeDtypeStruct((t, D_MODEL), BF16),
                   jax.ShapeDtypeStruct((N_EXPERTS, t), F32),
                   jax.ShapeDtypeStruct((N_EXPERTS, t), BF16)),
        grid_spec=grid_spec,
        compiler_params=_params(("parallel",)), name="post_attn",
    )(lidx, oat, ob, x, wo, gb, lng, lnb, wrh, wrl, br)


def _moe_kernel(lidx_ref, bounds_ref, hb_ref, gates_ref, goh_ref, upper_ref, wg_ref, wu_ref, wd_ref, o_ref,
                perm_ref, xs_ref, gs_ref, ys_ref):
    del lidx_ref
    i = pl.program_id(0)
    e = pl.program_id(1)
    tm = hb_ref.shape[0]
    base = i * (2 * N_GROUPS)

    @pl.when(e == 0)
    def _sort():
        goh = goh_ref[...]
        ranks = jnp.dot(goh, upper_ref[...], preferred_element_type=F32)
        row = lax.broadcasted_iota(jnp.int32, (N_EXPERTS, 1), 0)
        start = jnp.zeros((N_EXPERTS, 1), F32)
        for g in range(N_GROUPS):
            start = jnp.where(row == g * EXPERTS_PER_GROUP, bounds_ref[base + g].astype(F32), start)
        pos = jnp.sum(goh.astype(F32) * (ranks + start), axis=0, keepdims=True)
        slot = lax.broadcasted_iota(jnp.int32, (tm, tm), 0)
        perm = jnp.where(slot == pos.astype(jnp.int32), 1.0, 0.0).astype(BF16)
        perm_ref[...] = perm
        xs_ref[0:tm, :] = jnp.dot(perm, hb_ref[...], preferred_element_type=F32).astype(BF16)
        xs_ref[tm:, :] = jnp.zeros((MOE_PAD_ROWS, D_MODEL), BF16)
        gs_ref[tm:, :] = jnp.zeros((MOE_PAD_ROWS, LANES), F32)
        gates_t = gates_ref[...]
        g_hi = gates_t.astype(BF16)
        g_lo = (gates_t - g_hi.astype(F32)).astype(BF16)
        both = lax.dot_general(jnp.concatenate([g_hi, g_lo], axis=0), perm, (((1,), (1,)), ((), ())),
                               preferred_element_type=F32)
        sorted_t = both[0:N_EXPERTS] + both[N_EXPERTS:2 * N_EXPERTS]
        gs_ref[0:tm, :] = jnp.concatenate([sorted_t, jnp.zeros((LANES - N_EXPERTS, tm), F32)], axis=0).T
        ys_ref[...] = jnp.zeros(ys_ref.shape, F32)

    grp = (e * EXPERTS_PER_STEP) // EXPERTS_PER_GROUP
    first = bounds_ref[base + grp] // MOE_SUB
    stop = (bounds_ref[base + N_GROUPS + grp] + MOE_SUB - 1) // MOE_SUB

    def experts(row0, n_rows):
        rows = pl.ds(pl.multiple_of(row0, MOE_SUB), n_rows)
        lane = lax.broadcasted_iota(jnp.int32, (n_rows, LANES), 1)
        x = xs_ref[rows, :]
        gs = gs_ref[rows, :]
        total = None
        for j in range(EXPERTS_PER_STEP):
            gate = jnp.dot(x, wg_ref[0, 0, j], preferred_element_type=F32)
            up = jnp.dot(x, wu_ref[0, 0, j], preferred_element_type=F32)
            act = (gate * jax.nn.sigmoid(gate) * up).astype(BF16)
            y = jnp.dot(act, wd_ref[0, 0, j], preferred_element_type=F32)
            g_col = jnp.sum(jnp.where(lane == e * EXPERTS_PER_STEP + j, gs, 0.0), axis=1, keepdims=True)
            total = g_col * y if total is None else total + g_col * y
        ys_ref[rows, :] += total

    n_full = (stop - first) // MOE_MAX_UNITS
    rest = stop - first - MOE_MAX_UNITS * n_full
    tail = (first + MOE_MAX_UNITS * n_full) * MOE_SUB

    def full_body(s, carry):
        experts((first + MOE_MAX_UNITS * s) * MOE_SUB, MOE_MAX_UNITS * MOE_SUB)
        return carry

    lax.fori_loop(0, n_full, full_body, 0)
    for lo, units in MOE_TAIL_WINDOWS:
        @pl.when((rest > lo) & (rest <= units))
        def _(units=units):
            experts(tail, units * MOE_SUB)

    @pl.when(e == N_EXPERTS // EXPERTS_PER_STEP - 1)
    def _unsort():
        o_ref[...] = lax.dot_general(perm_ref[...], ys_ref[0:tm, :].astype(BF16), (((0,), (0,)), ((), ())),
                                     preferred_element_type=F32)


def _tile_bounds(goh, tm):
    nt = goh.shape[1] // tm
    cnt = goh[::EXPERTS_PER_GROUP].astype(F32).reshape(N_GROUPS, nt, tm).sum(-1).T.astype(jnp.int32)
    end = jnp.cumsum(cnt, axis=1)
    return jnp.concatenate([end - cnt, end], axis=1).reshape(-1)


def _moe(lidx, hb, gates, goh, wg, wu, wd):
    t = hb.shape[0]
    tm = min(MOE_TILE, t)
    bounds = _tile_bounds(goh, tm)
    n_steps = N_EXPERTS // EXPERTS_PER_STEP
    wg, wu, wd = (w.reshape(w.shape[0], n_steps, EXPERTS_PER_STEP, *w.shape[2:]) for w in (wg, wu, wd))
    upper = (jnp.arange(tm)[:, None] < jnp.arange(tm)[None, :]).astype(BF16)
    grid_spec = pltpu.PrefetchScalarGridSpec(
        num_scalar_prefetch=2,
        grid=(t // tm, n_steps),
        in_specs=[
            pl.BlockSpec((tm, D_MODEL), lambda i, e, l, b: (i, 0)),
            pl.BlockSpec((N_EXPERTS, tm), lambda i, e, l, b: (0, i)),
            pl.BlockSpec((N_EXPERTS, tm), lambda i, e, l, b: (0, i)),
            pl.BlockSpec((tm, tm), lambda i, e, l, b: (0, 0)),
            pl.BlockSpec((1, 1, EXPERTS_PER_STEP, D_MODEL, D_FF_EXPERT), lambda i, e, l, b: (l[0], e, 0, 0, 0)),
            pl.BlockSpec((1, 1, EXPERTS_PER_STEP, D_MODEL, D_FF_EXPERT), lambda i, e, l, b: (l[0], e, 0, 0, 0)),
            pl.BlockSpec((1, 1, EXPERTS_PER_STEP, D_FF_EXPERT, D_MODEL), lambda i, e, l, b: (l[0], e, 0, 0, 0)),
        ],
        out_specs=pl.BlockSpec((tm, D_MODEL), lambda i, e, l, b: (i, 0)),
        scratch_shapes=[pltpu.VMEM((tm, tm), BF16), pltpu.VMEM((tm + MOE_PAD_ROWS, D_MODEL), BF16),
                        pltpu.VMEM((tm + MOE_PAD_ROWS, LANES), F32), pltpu.VMEM((tm + MOE_PAD_ROWS, D_MODEL), F32)],
    )
    return pl.pallas_call(
        _moe_kernel, out_shape=jax.ShapeDtypeStruct((t, D_MODEL), F32), grid_spec=grid_spec,
        compiler_params=_params(("parallel", "arbitrary")), name="moe",
    )(lidx, bounds, hb, gates, goh, upper, wg, wu, wd)


def _post_moe_kernel(lidx_ref, h_ref, f_ref, p_ref, wpg_ref, bpg_ref, wpp_ref, lng_ref, lnb_ref, x_ref):
    del lidx_ref
    parts = [pl.ds(r * POST_ROWS, POST_ROWS) for r in range(h_ref.shape[0] // POST_ROWS)]
    h2s = [_layer_norm(ALPHA * h_ref[rows, :] + f_ref[rows, :], lng_ref[0, 1:2], lnb_ref[0, 1:2]) for rows in parts]
    zs = [jnp.dot(h2.astype(BF16), wpg_ref[0], preferred_element_type=F32) + bpg_ref[0] for h2 in h2s]
    projs = [jnp.dot(p_ref[0, rows, :].astype(BF16), wpp_ref[0], preferred_element_type=F32) for rows in parts]
    for rows, h2, z, proj in zip(parts, h2s, zs, projs):
        pe = jax.nn.sigmoid(z) * proj
        x_ref[rows, :] = _layer_norm(ALPHA * h2 + pe, lng_ref[0, 2:3], lnb_ref[0, 2:3])


def _post_moe(lidx, h, ffn, p, wpg, bpg, wpp, lng, lnb):
    t = h.shape[0]
    tm = ROW_TILE
    row = lambda i, l: (i, 0)
    lay = lambda i, l: (l[0], 0, 0)
    grid_spec = pltpu.PrefetchScalarGridSpec(
        num_scalar_prefetch=1,
        grid=(t // tm,),
        in_specs=[
            pl.BlockSpec((tm, D_MODEL), row),
            pl.BlockSpec((tm, D_MODEL), row),
            pl.BlockSpec((1, tm, PLE_DIM), lambda i, l: (l[0], i, 0)),
            pl.BlockSpec((1, D_MODEL, D_MODEL), lay),
            pl.BlockSpec((1, 1, D_MODEL), lay),
            pl.BlockSpec((1, PLE_DIM, D_MODEL), lay),
            pl.BlockSpec((1, 3, D_MODEL), lay),
            pl.BlockSpec((1, 3, D_MODEL), lay),
        ],
        out_specs=pl.BlockSpec((tm, D_MODEL), row),
    )
    return pl.pallas_call(
        _post_moe_kernel, out_shape=jax.ShapeDtypeStruct((t, D_MODEL), F32), grid_spec=grid_spec,
        compiler_params=_params(("parallel",)), name="post_moe",
    )(lidx, h, ffn, p, wpg, bpg, wpp, lng, lnb)


def _rope_tables_t(s):
    t = jnp.arange(s, dtype=jnp.int32)
    row = (t // GRID_W).astype(F32)
    col = (t % GRID_W).astype(F32)
    n_freq = HEAD_DIM // 4
    inv_freq = ROPE_THETA ** (-jnp.arange(n_freq, dtype=F32) / n_freq)
    ang = jnp.concatenate([inv_freq[:, None] * row[None, :], inv_freq[:, None] * col[None, :]], axis=0)
    return jnp.cos(ang), jnp.sin(ang)


def _prepare_weights(w_in, q_norm_a, k_norm_a, rpb_b, g_out_a, g_out_b, w_out, w_router, b_router,
                     w_gate, w_up, w_down, w_ple_gate, b_ple_gate, w_ple_proj):
    n_layers = w_in.shape[0]
    qa, ka, va, qn, kn, vn = jnp.split(w_in, [512, 640, 768, 1280, 1792], axis=-1)
    wt = jnp.swapaxes(jnp.concatenate([qa, ka, va], axis=-1), 1, 2).astype(BF16)
    wn = jnp.concatenate([qn, kn, vn], axis=-1).astype(BF16)
    wr_t = w_router.T
    wr_hi = wr_t.astype(BF16)
    wr_lo = (wr_t - wr_hi.astype(F32)).astype(BF16)
    return dict(
        wt=wt, wn=wn,
        qg=q_norm_a.reshape(n_layers, HEAD_DIM, 1), kg=k_norm_a.reshape(n_layers, HEAD_DIM, 1),
        bias=_nattn_bias_tables(rpb_b),
        ga=g_out_a.reshape(n_layers, A_WIDTH, 1), gb=g_out_b.reshape(n_layers, 1, B_WIDTH),
        wo=w_out.astype(BF16), wr_hi=wr_hi, wr_lo=wr_lo, br=b_router.reshape(N_EXPERTS, 1),
        wg=w_gate.astype(BF16), wu=w_up.astype(BF16), wd=w_down.astype(BF16),
        wpg=w_ple_gate.astype(BF16), bpg=b_ple_gate.reshape(n_layers, 1, D_MODEL), wpp=w_ple_proj.astype(BF16),
    )


def _layer(lidx, x, p_flat, w, ln_g, ln_b, rope):
    nb, s, _ = x.shape
    qz, ka, vat, qn, kn, vn, qnorm, knorm = _inproj(lidx, x, w["wt"], w["wn"], rope[0], rope[1], w["qg"], w["kg"])
    oat = _gattn(lidx, qz, ka, vat, qnorm, knorm, w["ga"])
    ob = _nattn(lidx, qn, kn, vn, w["bias"])
    t = nb * s
    h, hb, gates, goh = _post_attn(lidx, oat, ob.reshape(t, B_WIDTH), x.reshape(t, D_MODEL),
                                   w["wo"], w["gb"], ln_g, ln_b, w["wr_hi"], w["wr_lo"], w["br"])
    ffn = _moe(lidx, hb, gates, goh, w["wg"], w["wu"], w["wd"])
    xo = _post_moe(lidx, h, ffn, p_flat, w["wpg"], w["bpg"], w["wpp"], ln_g, ln_b)
    return xo.reshape(nb, s, D_MODEL)


def kernel(x_prompt, x_sample, p_prompt, p_sample, w_in, q_norm_a, k_norm_a, rpb_b, g_out_a, g_out_b, w_out, ln_g, ln_b, w_router, b_router, w_gate, w_up, w_down, w_ple_gate, b_ple_gate, w_ple_proj):
    n_layers = w_in.shape[0]
    w = _prepare_weights(w_in, q_norm_a, k_norm_a, rpb_b, g_out_a, g_out_b, w_out, w_router, b_router,
                         w_gate, w_up, w_down, w_ple_gate, b_ple_gate, w_ple_proj)
    trunks = []
    for x, p in ((x_prompt, p_prompt), (x_sample, p_sample)):
        nb, s, _ = x.shape
        trunks.append((x, p.reshape(n_layers, nb * s, PLE_DIM), _rope_tables_t(s)))

    outs = []
    for x, p_flat, rope in trunks:
        for layer in range(n_layers):
            x = _layer(jnp.full((1,), layer, jnp.int32), x, p_flat, w, ln_g, ln_b, rope)
        outs.append(x)
    return tuple(outs)
```

```python
import functools
import math

import jax
import jax.numpy as jnp
import numpy as np
from jax import lax
from jax.experimental import pallas as pl
from jax.experimental.pallas import tpu as pltpu

F32 = jnp.float32
BF16 = jnp.bfloat16

D_MODEL = 1024
N_LAYERS = 4
HEAD_DIM = 64
A_HEADS = 8
A_KV_HEADS = 2
A_GROUP = A_HEADS // A_KV_HEADS
B_HEADS = 8
A_WIDTH = A_HEADS * HEAD_DIM
KV_WIDTH = A_KV_HEADS * HEAD_DIM
B_WIDTH = B_HEADS * HEAD_DIM
GRID_W = 64
QBLK = 128
WIN_R = 8
WIN_C = 16
ROPE_THETA = 10000.0
N_EXPERTS = 16
N_GROUPS = 4
EXPERTS_PER_GROUP = N_EXPERTS // N_GROUPS
D_FF_EXPERT = 512
PLE_DIM = 256
LN_EPS = 1e-5
QK_EPS = 1e-6
NEG_INF = -1e30
ALPHA = (2 * N_LAYERS) ** 0.25
Q_SCALE = HEAD_DIM ** -0.5
LOG2_E = math.log2(math.e)

LANES = 128

TOKEN_TILE = 1024
ATTN_Q_TILE = 512
ROW_TILE = 1024
POST_ROWS = 128
MOE_TILE = 1024
MOE_SUB = 32
MOE_MAX_UNITS = 12
MOE_TAIL_WINDOWS = ((0, 4), (4, 8), (8, 9), (9, 10), (10, 11))
MOE_PAD_ROWS = max(units - lo - 1 for lo, units in MOE_TAIL_WINDOWS) * MOE_SUB
EXPERTS_PER_STEP = 4
BAND_ROWS = 10
BAND_BLKS = BAND_ROWS * GRID_W // QBLK
N_BIAS_VARIANTS = 5
NATTN_AHEAD = 3
GATTN_AHEAD = 1
VMEM_LIMIT = 56 * 1024 * 1024

_QA0, _KA0, _VA0, _PT_ROWS = 0, 512, 640, 768
_QN0, _KN0, _VN0, _NAT_COLS = 0, 512, 1024, 1536
ONES_ROWS = 16
VT_ROWS = HEAD_DIM + ONES_ROWS
SHIFT_BOUND_MAX = 40.0


def _params(sem):
    return pltpu.CompilerParams(dimension_semantics=sem, vmem_limit_bytes=VMEM_LIMIT)


def _layer_norm(y, g, b):
    mu = jnp.mean(y, axis=-1, keepdims=True)
    d = y - mu
    var = jnp.mean(d * d, axis=-1, keepdims=True)
    return d * lax.rsqrt(var + LN_EPS) * g + b


def _rope_t(y, cos, sin):
    cr, cc = cos[0:16], cos[16:32]
    sr, sc = sin[0:16], sin[16:32]
    a1, a2, b1, b2 = y[0:16], y[16:32], y[32:48], y[48:64]
    return jnp.concatenate([a1 * cr - a2 * sr, a1 * sr + a2 * cr,
                            b1 * cc - b2 * sc, b1 * sc + b2 * cc], axis=0)


def _norm_rope_t(y, gain, cos, sin):
    ms = jnp.sum(y * y, axis=0, keepdims=True) * (1.0 / HEAD_DIM)
    return _rope_t(y * lax.rsqrt(ms + QK_EPS) * gain, cos, sin)


def _inproj_kernel(lidx_ref, x_ref, wt_ref, wn_ref, cos_ref, sin_ref, qg_ref, kg_ref,
                   qz_ref, ka_ref, vat_ref, qn_ref, kn_ref, vn_ref, qnorm_ref, knorm_ref):
    del lidx_ref
    xb = x_ref[0].astype(BF16)
    pt = lax.dot_general(wt_ref[0], xb, (((1,), (1,)), ((), ())),
                         preferred_element_type=F32)
    tm = xb.shape[0]
    cos = cos_ref[...]
    sin = sin_ref[...]
    qg = qg_ref[0] * (Q_SCALE * LOG2_E)
    kg = kg_ref[0]
    zeros = jnp.zeros((HEAD_DIM, tm), BF16)
    head_row = lax.broadcasted_iota(jnp.int32, (A_HEADS, 1), 0)
    l2 = lambda y: jnp.sqrt(jnp.sum(y * y, axis=0, keepdims=True))
    qnorm = jnp.zeros((A_HEADS, tm), F32)
    for h in range(A_HEADS):
        qf = _norm_rope_t(pt[_QA0 + h * HEAD_DIM:_QA0 + (h + 1) * HEAD_DIM], qg, cos, sin)
        qnorm = jnp.where(head_row == h, l2(qf), qnorm)
        q = qf.astype(BF16)
        if h // A_GROUP == 0:
            qz_ref[0, h] = jnp.concatenate([q, zeros], axis=0)
        else:
            qz_ref[0, h] = jnp.concatenate([zeros, q], axis=0)
    qnorm_ref[0] = qnorm
    ks = [_norm_rope_t(pt[_KA0 + g * HEAD_DIM:_KA0 + (g + 1) * HEAD_DIM], kg, cos, sin) for g in range(A_KV_HEADS)]
    knorm = jnp.zeros((A_HEADS, tm), F32)
    for g in range(A_KV_HEADS):
        knorm = jnp.where(head_row == g, l2(ks[g]), knorm)
    knorm_ref[0, 0] = knorm
    kt = jnp.concatenate(ks, axis=0)
    ka_ref[0, 0] = kt.T.astype(BF16)
    ones = jnp.ones((ONES_ROWS, tm), BF16)
    for g in range(A_KV_HEADS):
        v = pt[_VA0 + g * HEAD_DIM:_VA0 + (g + 1) * HEAD_DIM].astype(BF16)
        vat_ref[0, 0, g] = jnp.concatenate([v, ones], axis=0)
    nat = jnp.dot(xb, wn_ref[0], preferred_element_type=F32)
    qn_ref[0] = (nat[:, _QN0:_QN0 + B_WIDTH] * (Q_SCALE * LOG2_E)).astype(BF16)
    kn_ref[0] = nat[:, _KN0:_KN0 + B_WIDTH].astype(BF16).reshape(tm // QBLK, QBLK, B_WIDTH)
    vn_ref[0] = nat[:, _VN0:_VN0 + B_WIDTH].astype(BF16).reshape(tm // QBLK, QBLK, B_WIDTH)


def _inproj(lidx, x, wt, wn, cos_t, sin_t, qg, kg):
    nb, s, _ = x.shape
    tm = TOKEN_TILE
    nc = s // tm
    out_shape = (
        jax.ShapeDtypeStruct((nb, A_HEADS, 2 * HEAD_DIM, s), BF16),
        jax.ShapeDtypeStruct((nb, nc, tm, KV_WIDTH), BF16),
        jax.ShapeDtypeStruct((nb, nc, A_KV_HEADS, VT_ROWS, tm), BF16),
        jax.ShapeDtypeStruct((nb, s, B_WIDTH), BF16),
        jax.ShapeDtypeStruct((nb, s // QBLK, QBLK, B_WIDTH), BF16),
        jax.ShapeDtypeStruct((nb, s // QBLK, QBLK, B_WIDTH), BF16),
        jax.ShapeDtypeStruct((nb, A_HEADS, s), F32),
        jax.ShapeDtypeStruct((nb, nc, A_HEADS, tm), F32),
    )
    grid_spec = pltpu.PrefetchScalarGridSpec(
        num_scalar_prefetch=1,
        grid=(nb, nc),
        in_specs=[
            pl.BlockSpec((1, tm, D_MODEL), lambda b, c, l: (b, c, 0)),
            pl.BlockSpec((1, _PT_ROWS, D_MODEL), lambda b, c, l: (l[0], 0, 0)),
            pl.BlockSpec((1, D_MODEL, _NAT_COLS), lambda b, c, l: (l[0], 0, 0)),
            pl.BlockSpec((HEAD_DIM // 2, tm), lambda b, c, l: (0, c)),
            pl.BlockSpec((HEAD_DIM // 2, tm), lambda b, c, l: (0, c)),
            pl.BlockSpec((1, HEAD_DIM, 1), lambda b, c, l: (l[0], 0, 0)),
            pl.BlockSpec((1, HEAD_DIM, 1), lambda b, c, l: (l[0], 0, 0)),
        ],
        out_specs=[
            pl.BlockSpec((1, A_HEADS, 2 * HEAD_DIM, tm), lambda b, c, l: (b, 0, 0, c)),
            pl.BlockSpec((1, 1, tm, KV_WIDTH), lambda b, c, l: (b, c, 0, 0)),
            pl.BlockSpec((1, 1, A_KV_HEADS, VT_ROWS, tm), lambda b, c, l: (b, c, 0, 0, 0)),
            pl.BlockSpec((1, tm, B_WIDTH), lambda b, c, l: (b, c, 0)),
            pl.BlockSpec((1, tm // QBLK, QBLK, B_WIDTH), lambda b, c, l: (b, c, 0, 0)),
            pl.BlockSpec((1, tm // QBLK, QBLK, B_WIDTH), lambda b, c, l: (b, c, 0, 0)),
            pl.BlockSpec((1, A_HEADS, tm), lambda b, c, l: (b, 0, c)),
            pl.BlockSpec((1, 1, A_HEADS, tm), lambda b, c, l: (b, c, 0, 0)),
        ],
    )
    return pl.pallas_call(
        _inproj_kernel, out_shape=out_shape, grid_spec=grid_spec,
        compiler_params=_params(("parallel", "parallel")), name="inproj",
    )(lidx, x, wt, wn, cos_t, sin_t, qg, kg)


def _col_max(st):
    slabs = [st[r:r + 128] for r in range(0, st.shape[0], 128)]
    while len(slabs) > 1:
        slabs = [jnp.maximum(slabs[i], slabs[i + 1]) for i in range(0, len(slabs), 2)]
    return jnp.max(slabs[0], axis=0, keepdims=True)


def _gattn_kernel(lidx_ref, qz_ref, k_ref, vt_ref, qn_ref, kn_ref, ga_ref, o_ref, *scratch):
    del lidx_ref
    m_refs, acc_refs = scratch[:A_HEADS], scratch[A_HEADS:]
    n_chunks = k_ref.shape[1]
    for h in range(A_HEADS):
        acc_refs[h][...] = jnp.zeros(acc_refs[h].shape, F32)

    k_norm_max = jnp.max(jnp.max(kn_ref[0], axis=0), axis=1, keepdims=True)
    row = lax.broadcasted_iota(jnp.int32, (A_HEADS, 1), 0)
    k_of_head = jnp.where(row < A_GROUP, k_norm_max[0:1], k_norm_max[1:2])
    bound = qn_ref[0] * k_of_head
    bounded = jnp.max(bound) <= SHIFT_BOUND_MAX

    def chunk_body(c, carry, online):
        kc = k_ref[0, c]
        scores = lambda h: jnp.dot(kc, qz_ref[0, h], preferred_element_type=F32)
        pending = [scores(h) for h in range(GATTN_AHEAD)]
        for h in range(A_HEADS):
            st = pending.pop(0)
            if h + GATTN_AHEAD < A_HEADS:
                pending.append(scores(h + GATTN_AHEAD))
            vt = vt_ref[0, c, h // A_GROUP]
            if online:
                m_old = m_refs[h][...]
                m_new = jnp.maximum(m_old, _col_max(st))
                alpha = jnp.exp2(m_old - m_new)
                p = jnp.exp2(st - m_new).astype(BF16)
                acc_refs[h][...] = alpha * acc_refs[h][...] + jnp.dot(vt, p, preferred_element_type=F32)
                m_refs[h][...] = m_new
            else:
                p = jnp.exp2(st - m_refs[h][...]).astype(BF16)
                acc_refs[h][...] += jnp.dot(vt, p, preferred_element_type=F32)
        return carry

    @pl.when(bounded)
    def _():
        for h in range(A_HEADS):
            m_refs[h][...] = bound[h:h + 1]
        lax.fori_loop(0, n_chunks, functools.partial(chunk_body, online=False), 0)

    @pl.when(jnp.logical_not(bounded))
    def _():
        for h in range(A_HEADS):
            m_refs[h][...] = jnp.full(m_refs[h].shape, -jnp.inf, F32)
        lax.fori_loop(0, n_chunks, functools.partial(chunk_body, online=True), 0)

    outs = []
    for h in range(A_HEADS):
        acc = acc_refs[h][...]
        outs.append(acc[0:HEAD_DIM] * (1.0 / acc[HEAD_DIM:HEAD_DIM + 1]))
    ot = jnp.concatenate(outs, axis=0)
    ms = jnp.sum(ot * ot, axis=0, keepdims=True) * (1.0 / A_WIDTH)
    o_ref[0] = (ot * lax.rsqrt(ms + LN_EPS) * ga_ref[0]).astype(BF16)


def _gattn(lidx, qz, ka, vat, qnorm, knorm, ga_col):
    nb, _, _, s = qz.shape
    nc, tk = ka.shape[1], ka.shape[2]
    tq = ATTN_Q_TILE
    grid_spec = pltpu.PrefetchScalarGridSpec(
        num_scalar_prefetch=1,
        grid=(nb, s // tq),
        in_specs=[
            pl.BlockSpec((1, A_HEADS, 2 * HEAD_DIM, tq), lambda b, i, l: (b, 0, 0, i)),
            pl.BlockSpec((1, nc, tk, KV_WIDTH), lambda b, i, l: (b, 0, 0, 0)),
            pl.BlockSpec((1, nc, A_KV_HEADS, VT_ROWS, tk), lambda b, i, l: (b, 0, 0, 0, 0)),
            pl.BlockSpec((1, A_HEADS, tq), lambda b, i, l: (b, 0, i)),
            pl.BlockSpec((1, nc, A_HEADS, tk), lambda b, i, l: (b, 0, 0, 0)),
            pl.BlockSpec((1, A_WIDTH, 1), lambda b, i, l: (l[0], 0, 0)),
        ],
        out_specs=pl.BlockSpec((1, A_WIDTH, tq), lambda b, i, l: (b, 0, i)),
        scratch_shapes=[pltpu.VMEM((1, tq), F32)] * A_HEADS + [pltpu.VMEM((VT_ROWS, tq), F32)] * A_HEADS,
    )
    return pl.pallas_call(
        _gattn_kernel, out_shape=jax.ShapeDtypeStruct((nb, A_WIDTH, s), BF16), grid_spec=grid_spec,
        compiler_params=_params(("parallel", "parallel")), name="gattn",
    )(lidx, qz, ka, vat, qnorm, knorm, ga_col)


def _nattn_kernel(*refs):
    q_ref = refs[0]
    k_refs = refs[1:1 + BAND_BLKS]
    v_refs = refs[1 + BAND_BLKS:1 + 2 * BAND_BLKS]
    bias_ref = refs[1 + 2 * BAND_BLKS]
    o_ref = refs[2 + 2 * BAND_BLKS]
    first_head = lax.broadcasted_iota(jnp.int32, (QBLK, LANES), 1) < HEAD_DIM
    dn = (((1,), (1,)), ((), ()))

    def band(refs, c):
        return jnp.concatenate([refs[t][0, 0, :, c * LANES:(c + 1) * LANES] for t in range(BAND_BLKS)], axis=0)

    def scores(h):
        c, hh = h // 2, h % 2
        qp = q_ref[0, :, c * LANES:(c + 1) * LANES]
        qm = jnp.where(first_head if hh == 0 else jnp.logical_not(first_head), qp, jnp.zeros_like(qp))
        return lax.dot_general(qm, band(k_refs, c), dn, preferred_element_type=F32) + bias_ref[0, h]

    outs = []
    pending = [scores(h) for h in range(NATTN_AHEAD)]
    for h in range(B_HEADS):
        s = pending.pop(0)
        if h + NATTN_AHEAD < B_HEADS:
            pending.append(scores(h + NATTN_AHEAD))
        m = jnp.max(s, axis=-1, keepdims=True)
        p = jnp.exp2(s - m)
        l = jnp.sum(p, axis=-1, keepdims=True)
        outs.append(jnp.dot(p.astype(BF16), band(v_refs, h // 2), preferred_element_type=F32) * (1.0 / l))
    pairs = [jnp.where(first_head, outs[2 * c], outs[2 * c + 1]) for c in range(B_HEADS // 2)]
    o_ref[0] = jnp.concatenate(pairs, axis=-1)


def _band_start_blk(j, nblk):
    rows_per_blk = QBLK // GRID_W
    rows = nblk * rows_per_blk
    bs = jnp.clip(j * rows_per_blk - WIN_R // 2, 0, rows - BAND_ROWS)
    return bs // rows_per_blk


def _bias_variant(j, nblk):
    return jnp.where(j < 2, j, jnp.where(j >= nblk - 2, j - nblk + N_BIAS_VARIANTS, 2))


def _nattn(lidx, qn, kn, vn, bias):
    nb, s, _ = qn.shape
    nblk = s // QBLK
    band_specs = [pl.BlockSpec((1, 1, QBLK, B_WIDTH),
                               functools.partial(lambda b, j, l, t: (b, _band_start_blk(j, nblk) + t, 0, 0), t=t))
                  for t in range(BAND_BLKS)]
    grid_spec = pltpu.PrefetchScalarGridSpec(
        num_scalar_prefetch=1,
        grid=(nb, nblk),
        in_specs=[pl.BlockSpec((1, QBLK, B_WIDTH), lambda b, j, l: (b, j, 0))]
        + band_specs + band_specs
        + [pl.BlockSpec((1, B_HEADS, QBLK, BAND_ROWS * GRID_W),
                        lambda b, j, l: (l[0] * N_BIAS_VARIANTS + _bias_variant(j, nblk), 0, 0, 0))],
        out_specs=pl.BlockSpec((1, QBLK, B_WIDTH), lambda b, j, l: (b, j, 0)),
    )

    def kernel(lidx_ref, *refs):
        del lidx_ref
        _nattn_kernel(*refs)

    return pl.pallas_call(
        kernel, out_shape=jax.ShapeDtypeStruct((nb, s, B_WIDTH), F32), grid_spec=grid_spec,
        compiler_params=_params(("parallel", "arbitrary")), name="nattn",
    )(lidx, qn, *([kn] * BAND_BLKS), *([vn] * BAND_BLKS), bias)


def _nattn_bias_tables(rpb):
    n_layers = rpb.shape[0]
    n_r, n_c = 2 * WIN_R - 1, 2 * WIN_C - 1
    qc = np.arange(GRID_W)[:, None]
    kc = np.arange(GRID_W)[None, :]
    ic = np.clip(kc - qc + WIN_C - 1, 0, n_c - 1)
    cs = np.clip(qc - WIN_C // 2, 0, GRID_W - WIN_C)
    col_ok = (kc >= cs) & (kc < cs + WIN_C)
    onehot = jnp.asarray((ic.reshape(-1)[None, :] == np.arange(n_c)[:, None]).astype(np.float32))
    cols = jnp.dot(rpb.astype(F32).reshape(-1, n_c), onehot, precision=lax.Precision.HIGHEST)
    cols = cols.reshape(n_layers, B_HEADS, n_r, GRID_W, GRID_W) * LOG2_E
    rows_per_blk = QBLK // GRID_W
    rows = 16
    variants = []
    for r0 in (0, 2, 4, rows - 4, rows - 2):
        bs = min(max(r0 - WIN_R // 2, 0), rows - BAND_ROWS)
        q_rows = []
        for qo in range(rows_per_blk):
            qr = r0 + qo
            rs = min(max(qr - WIN_R // 2, 0), rows - WIN_R)
            k_rows = []
            for ko in range(BAND_ROWS):
                kr = bs + ko
                ir = min(max(kr - qr + WIN_R - 1, 0), n_r - 1)
                if rs <= kr < rs + WIN_R:
                    k_rows.append(jnp.where(jnp.asarray(col_ok)[None, None], cols[:, :, ir], NEG_INF))
                else:
                    k_rows.append(jnp.full((n_layers, B_HEADS, GRID_W, GRID_W), NEG_INF, F32))
            q_rows.append(jnp.concatenate(k_rows, axis=-1))
        variants.append(jnp.concatenate(q_rows, axis=-2))
    t = jnp.stack(variants, axis=1)
    return t.reshape(n_layers * N_BIAS_VARIANTS, B_HEADS, QBLK, BAND_ROWS * GRID_W)


def _group_member(x, d, wrapped):
    return jnp.where(wrapped, pltpu.roll(x, EXPERTS_PER_GROUP - d, 0), pltpu.roll(x, N_EXPERTS - d, 0))


def _route_t(logits_t):
    mx = jnp.max(logits_t, axis=0, keepdims=True)
    ex = jnp.exp(logits_t - mx)
    probs = ex / jnp.sum(ex, axis=0, keepdims=True)
    e_idx = lax.broadcasted_iota(jnp.int32, probs.shape, 0)
    pos = e_idx % EXPERTS_PER_GROUP
    rank = jnp.zeros(probs.shape, jnp.int32)
    for d in range(1, EXPERTS_PER_GROUP):
        wrapped = pos + d >= EXPERTS_PER_GROUP
        other = _group_member(probs, d, wrapped)
        beats = (other > probs) | ((other == probs) & wrapped)
        rank = rank + beats.astype(jnp.int32)
    top = rank < 2
    kept = jnp.where(top, probs, 0.0)
    score = kept
    for d in range(1, EXPERTS_PER_GROUP):
        score = score + _group_member(kept, d, pos + d >= EXPERTS_PER_GROUP)
    grp = e_idx // EXPERTS_PER_GROUP
    lost = jnp.zeros(probs.shape, jnp.bool_)
    for d in range(1, N_GROUPS):
        other = pltpu.roll(score, N_EXPERTS - d * EXPERTS_PER_GROUP, 0)
        earlier = grp + d >= N_GROUPS
        lost = lost | (other > score) | ((other == score) & earlier)
    won = jnp.logical_not(lost)
    gates = jnp.where(top & won, probs / score, 0.0)
    group_onehot = jnp.where(won & (pos == 0), 1.0, 0.0)
    return gates, group_onehot


def _post_attn_kernel(lidx_ref, oat_ref, ob_ref, x_ref, wo_ref, gb_ref, lng_ref, lnb_ref,
                      wrh_ref, wrl_ref, br_ref, h_ref, hb_ref, gates_ref, goh_ref):
    del lidx_ref
    tm = x_ref.shape[0]
    n_part = tm // POST_ROWS
    parts = [pl.ds(r * POST_ROWS, POST_ROWS) for r in range(n_part)]
    dn = (((1,), (1,)), ((), ()))
    dn_t = (((0,), (0,)), ((), ()))

    def rms_b(rows):
        b = ob_ref[rows, :]
        return (b * lax.rsqrt(jnp.mean(b * b, axis=-1, keepdims=True) + LN_EPS) * gb_ref[0]).astype(BF16)

    nbs = [rms_b(rows) for rows in parts]
    mixes = [lax.dot_general(oat_ref[0, :, rows], wo_ref[0, 0:A_WIDTH, :], dn_t, preferred_element_type=F32)
             + jnp.dot(nb_, wo_ref[0, A_WIDTH:, :], preferred_element_type=F32) for rows, nb_ in zip(parts, nbs)]
    hs = [_layer_norm(ALPHA * x_ref[rows, :] + mix, lng_ref[0, 0:1], lnb_ref[0, 0:1]) for rows, mix in zip(parts, mixes)]
    for rows, h in zip(parts, hs):
        h_ref[rows, :] = h
        h_hi = h.astype(BF16)
        hb_ref[rows, :] = h_hi
        h_lo = (h - h_hi.astype(F32)).astype(BF16)
        logits_t = (lax.dot_general(wrh_ref[...], h_hi, dn, preferred_element_type=F32)
                    + lax.dot_general(wrh_ref[...], h_lo, dn, preferred_element_type=F32)
                    + lax.dot_general(wrl_ref[...], h_hi, dn, preferred_element_type=F32)) + br_ref[...]
        gates_t, group_onehot = _route_t(logits_t)
        goh_ref[:, rows] = group_onehot.astype(BF16)
        gates_ref[:, rows] = gates_t


def _post_attn(lidx, oat, ob, x, wo, gb, lng, lnb, wrh, wrl, br):
    t = x.shape[0]
    tm = ROW_TILE
    tiles_per_seq = oat.shape[2] // tm
    row = lambda i, l: (i, 0)
    lay = lambda i, l: (l[0], 0, 0)
    full = lambda i, l: (0, 0)
    grid_spec = pltpu.PrefetchScalarGridSpec(
        num_scalar_prefetch=1,
        grid=(t // tm,),
        in_specs=[
            pl.BlockSpec((1, A_WIDTH, tm), lambda i, l: (i // tiles_per_seq, 0, i % tiles_per_seq)),
            pl.BlockSpec((tm, B_WIDTH), row),
            pl.BlockSpec((tm, D_MODEL), row),
            pl.BlockSpec((1, D_MODEL, D_MODEL), lay),
            pl.BlockSpec((1, 1, B_WIDTH), lay),
            pl.BlockSpec((1, 3, D_MODEL), lay),
            pl.BlockSpec((1, 3, D_MODEL), lay),
            pl.BlockSpec((N_EXPERTS, D_MODEL), full),
            pl.BlockSpec((N_EXPERTS, D_MODEL), full),
            pl.BlockSpec((N_EXPERTS, 1), full),
        ],
        out_specs=[
            pl.BlockSpec((tm, D_MODEL), row),
            pl.BlockSpec((tm, D_MODEL), row),
            pl.BlockSpec((N_EXPERTS, tm), lambda i, l: (0, i)),
            pl.BlockSpec((N_EXPERTS, tm), lambda i, l: (0, i)),
        ],
    )
    return pl.pallas_call(
        _post_attn_kernel,
        out_shape=(jax.ShapeDtypeStruct((t, D_MODEL), F32),
                   jax.ShapeDtypeStruct((t, D_MODEL), BF16),
                   jax.ShapeDtypeStruct((N_EXPERTS, t), F32),
                   jax.ShapeDtypeStruct((N_EXPERTS, t), BF16)),
        grid_spec=grid_spec,
        compiler_params=_params(("parallel",)), name="post_attn",
    )(lidx, oat, ob, x, wo, gb, lng, lnb, wrh, wrl, br)


def _moe_kernel(lidx_ref, bounds_ref, hb_ref, gates_ref, goh_ref, upper_ref, wg_ref, wu_ref, wd_ref, o_ref,
                perm_ref, xs_ref, gs_ref, ys_ref):
    del lidx_ref
    i = pl.program_id(0)
    e = pl.program_id(1)
    tm = hb_ref.shape[0]
    base = i * (2 * N_GROUPS)

    @pl.when(e == 0)
    def _sort():
        goh = goh_ref[...]
        ranks = jnp.dot(goh, upper_ref[...], preferred_element_type=F32)
        row = lax.broadcasted_iota(jnp.int32, (N_EXPERTS, 1), 0)
        start = jnp.zeros((N_EXPERTS, 1), F32)
        for g in range(N_GROUPS):
            start = jnp.where(row == g * EXPERTS_PER_GROUP, bounds_ref[base + g].astype(F32), start)
        pos = jnp.sum(goh.astype(F32) * (ranks + start), axis=0, keepdims=True)
        slot = lax.broadcasted_iota(jnp.int32, (tm, tm), 0)
        perm = jnp.where(slot == pos.astype(jnp.int32), 1.0, 0.0).astype(BF16)
        perm_ref[...] = perm
        xs_ref[0:tm, :] = jnp.dot(perm, hb_ref[...], preferred_element_type=F32).astype(BF16)
        xs_ref[tm:, :] = jnp.zeros((MOE_PAD_ROWS, D_MODEL), BF16)
        gs_ref[tm:, :] = jnp.zeros((MOE_PAD_ROWS, LANES), F32)
        gates_t = gates_ref[...]
        g_hi = gates_t.astype(BF16)
        g_lo = (gates_t - g_hi.astype(F32)).astype(BF16)
        both = lax.dot_general(jnp.concatenate([g_hi, g_lo], axis=0), perm, (((1,), (1,)), ((), ())),
                               preferred_element_type=F32)
        sorted_t = both[0:N_EXPERTS] + both[N_EXPERTS:2 * N_EXPERTS]
        gs_ref[0:tm, :] = jnp.concatenate([sorted_t, jnp.zeros((LANES - N_EXPERTS, tm), F32)], axis=0).T
        ys_ref[...] = jnp.zeros(ys_ref.shape, F32)

    grp = (e * EXPERTS_PER_STEP) // EXPERTS_PER_GROUP
    first = bounds_ref[base + grp] // MOE_SUB
    stop = (bounds_ref[base + N_GROUPS + grp] + MOE_SUB - 1) // MOE_SUB

    def experts(row0, n_rows):
        rows = pl.ds(pl.multiple_of(row0, MOE_SUB), n_rows)
        lane = lax.broadcasted_iota(jnp.int32, (n_rows, LANES), 1)
        x = xs_ref[rows, :]
        gs = gs_ref[rows, :]
        total = None
        for j in range(EXPERTS_PER_STEP):
            gate = jnp.dot(x, wg_ref[0, 0, j], preferred_element_type=F32)
            up = jnp.dot(x, wu_ref[0, 0, j], preferred_element_type=F32)
            act = (gate * jax.nn.sigmoid(gate) * up).astype(BF16)
            y = jnp.dot(act, wd_ref[0, 0, j], preferred_element_type=F32)
            g_col = jnp.sum(jnp.where(lane == e * EXPERTS_PER_STEP + j, gs, 0.0), axis=1, keepdims=True)
            total = g_col * y if total is None else total + g_col * y
        ys_ref[rows, :] += total

    n_full = (stop - first) // MOE_MAX_UNITS
    rest = stop - first - MOE_MAX_UNITS * n_full
    tail = (first + MOE_MAX_UNITS * n_full) * MOE_SUB

    def full_body(s, carry):
        experts((first + MOE_MAX_UNITS * s) * MOE_SUB, MOE_MAX_UNITS * MOE_SUB)
        return carry

    lax.fori_loop(0, n_full, full_body, 0)
    for lo, units in MOE_TAIL_WINDOWS:
        @pl.when((rest > lo) & (rest <= units))
        def _(units=units):
            experts(tail, units * MOE_SUB)

    @pl.when(e == N_EXPERTS // EXPERTS_PER_STEP - 1)
    def _unsort():
        o_ref[...] = lax.dot_general(perm_ref[...], ys_ref[0:tm, :].astype(BF16), (((0,), (0,)), ((), ())),
                                     preferred_element_type=F32)


def _tile_bounds(goh, tm):
    nt = goh.shape[1] // tm
    cnt = goh[::EXPERTS_PER_GROUP].astype(F32).reshape(N_GROUPS, nt, tm).sum(-1).T.astype(jnp.int32)
    end = jnp.cumsum(cnt, axis=1)
    return jnp.concatenate([end - cnt, end], axis=1).reshape(-1)


def _moe(lidx, hb, gates, goh, wg, wu, wd):
    t = hb.shape[0]
    tm = min(MOE_TILE, t)
    bounds = _tile_bounds(goh, tm)
    n_steps = N_EXPERTS // EXPERTS_PER_STEP
    wg, wu, wd = (w.reshape(w.shape[0], n_steps, EXPERTS_PER_STEP, *w.shape[2:]) for w in (wg, wu, wd))
    upper = (jnp.arange(tm)[:, None] < jnp.arange(tm)[None, :]).astype(BF16)
    grid_spec = pltpu.PrefetchScalarGridSpec(
        num_scalar_prefetch=2,
        grid=(t // tm, n_steps),
        in_specs=[
            pl.BlockSpec((tm, D_MODEL), lambda i, e, l, b: (i, 0)),
            pl.BlockSpec((N_EXPERTS, tm), lambda i, e, l, b: (0, i)),
            pl.BlockSpec((N_EXPERTS, tm), lambda i, e, l, b: (0, i)),
            pl.BlockSpec((tm, tm), lambda i, e, l, b: (0, 0)),
            pl.BlockSpec((1, 1, EXPERTS_PER_STEP, D_MODEL, D_FF_EXPERT), lambda i, e, l, b: (l[0], e, 0, 0, 0)),
            pl.BlockSpec((1, 1, EXPERTS_PER_STEP, D_MODEL, D_FF_EXPERT), lambda i, e, l, b: (l[0], e, 0, 0, 0)),
            pl.BlockSpec((1, 1, EXPERTS_PER_STEP, D_FF_EXPERT, D_MODEL), lambda i, e, l, b: (l[0], e, 0, 0, 0)),
        ],
        out_specs=pl.BlockSpec((tm, D_MODEL), lambda i, e, l, b: (i, 0)),
        scratch_shapes=[pltpu.VMEM((tm, tm), BF16), pltpu.VMEM((tm + MOE_PAD_ROWS, D_MODEL), BF16),
                        pltpu.VMEM((tm + MOE_PAD_ROWS, LANES), F32), pltpu.VMEM((tm + MOE_PAD_ROWS, D_MODEL), F32)],
    )
    return pl.pallas_call(
        _moe_kernel, out_shape=jax.ShapeDtypeStruct((t, D_MODEL), F32), grid_spec=grid_spec,
        compiler_params=_params(("parallel", "arbitrary")), name="moe",
    )(lidx, bounds, hb, gates, goh, upper, wg, wu, wd)


def _post_moe_kernel(lidx_ref, h_ref, f_ref, p_ref, wpg_ref, bpg_ref, wpp_ref, lng_ref, lnb_ref, x_ref):
    del lidx_ref
    parts = [pl.ds(r * POST_ROWS, POST_ROWS) for r in range(h_ref.shape[0] // POST_ROWS)]
    h2s = [_layer_norm(ALPHA * h_ref[rows, :] + f_ref[rows, :], lng_ref[0, 1:2], lnb_ref[0, 1:2]) for rows in parts]
    zs = [jnp.dot(h2.astype(BF16), wpg_ref[0], preferred_element_type=F32) + bpg_ref[0] for h2 in h2s]
    projs = [jnp.dot(p_ref[0, rows, :].astype(BF16), wpp_ref[0], preferred_element_type=F32) for rows in parts]
    for rows, h2, z, proj in zip(parts, h2s, zs, projs):
        pe = jax.nn.sigmoid(z) * proj
        x_ref[rows, :] = _layer_norm(ALPHA * h2 + pe, lng_ref[0, 2:3], lnb_ref[0, 2:3])


def _post_moe(lidx, h, ffn, p, wpg, bpg, wpp, lng, lnb):
    t = h.shape[0]
    tm = ROW_TILE
    row = lambda i, l: (i, 0)
    lay = lambda i, l: (l[0], 0, 0)
    grid_spec = pltpu.PrefetchScalarGridSpec(
        num_scalar_prefetch=1,
        grid=(t // tm,),
        in_specs=[
            pl.BlockSpec((tm, D_MODEL), row),
            pl.BlockSpec((tm, D_MODEL), row),
            pl.BlockSpec((1, tm, PLE_DIM), lambda i, l: (l[0], i, 0)),
            pl.BlockSpec((1, D_MODEL, D_MODEL), lay),
            pl.BlockSpec((1, 1, D_MODEL), lay),
            pl.BlockSpec((1, PLE_DIM, D_MODEL), lay),
            pl.BlockSpec((1, 3, D_MODEL), lay),
            pl.BlockSpec((1, 3, D_MODEL), lay),
        ],
        out_specs=pl.BlockSpec((tm, D_MODEL), row),
    )
    return pl.pallas_call(
        _post_moe_kernel, out_shape=jax.ShapeDtypeStruct((t, D_MODEL), F32), grid_spec=grid_spec,
        compiler_params=_params(("parallel",)), name="post_moe",
    )(lidx, h, ffn, p, wpg, bpg, wpp, lng, lnb)


def _rope_tables_t(s):
    t = jnp.arange(s, dtype=jnp.int32)
    row = (t // GRID_W).astype(F32)
    col = (t % GRID_W).astype(F32)
    n_freq = HEAD_DIM // 4
    inv_freq = ROPE_THETA ** (-jnp.arange(n_freq, dtype=F32) / n_freq)
    ang = jnp.concatenate([inv_freq[:, None] * row[None, :], inv_freq[:, None] * col[None, :]], axis=0)
    return jnp.cos(ang), jnp.sin(ang)


def _prepare_weights(w_in, q_norm_a, k_norm_a, rpb_b, g_out_a, g_out_b, w_out, w_router, b_router,
                     w_gate, w_up, w_down, w_ple_gate, b_ple_gate, w_ple_proj):
    n_layers = w_in.shape[0]
    qa, ka, va, qn, kn, vn = jnp.split(w_in, [512, 640, 768, 1280, 1792], axis=-1)
    wt = jnp.swapaxes(jnp.concatenate([qa, ka, va], axis=-1), 1, 2).astype(BF16)
    wn = jnp.concatenate([qn, kn, vn], axis=-1).astype(BF16)
    wr_t = w_router.T
    wr_hi = wr_t.astype(BF16)
    wr_lo = (wr_t - wr_hi.astype(F32)).astype(BF16)
    return dict(
        wt=wt, wn=wn,
        qg=q_norm_a.reshape(n_layers, HEAD_DIM, 1), kg=k_norm_a.reshape(n_layers, HEAD_DIM, 1),
        bias=_nattn_bias_tables(rpb_b),
        ga=g_out_a.reshape(n_layers, A_WIDTH, 1), gb=g_out_b.reshape(n_layers, 1, B_WIDTH),
        wo=w_out.astype(BF16), wr_hi=wr_hi, wr_lo=wr_lo, br=b_router.reshape(N_EXPERTS, 1),
        wg=w_gate.astype(BF16), wu=w_up.astype(BF16), wd=w_down.astype(BF16),
        wpg=w_ple_gate.astype(BF16), bpg=b_ple_gate.reshape(n_layers, 1, D_MODEL), wpp=w_ple_proj.astype(BF16),
    )


def _layer(lidx, x, p_flat, w, ln_g, ln_b, rope):
    nb, s, _ = x.shape
    qz, ka, vat, qn, kn, vn, qnorm, knorm = _inproj(lidx, x, w["wt"], w["wn"], rope[0], rope[1], w["qg"], w["kg"])
    oat = _gattn(lidx, qz, ka, vat, qnorm, knorm, w["ga"])
    ob = _nattn(lidx, qn, kn, vn, w["bias"])
    t = nb * s
    h, hb, gates, goh = _post_attn(lidx, oat, ob.reshape(t, B_WIDTH), x.reshape(t, D_MODEL),
                                   w["wo"], w["gb"], ln_g, ln_b, w["wr_hi"], w["wr_lo"], w["br"])
    ffn = _moe(lidx, hb, gates, goh, w["wg"], w["wu"], w["wd"])
    xo = _post_moe(lidx, h, ffn, p_flat, w["wpg"], w["bpg"], w["wpp"], ln_g, ln_b)
    return xo.reshape(nb, s, D_MODEL)


def kernel(x_prompt, x_sample, p_prompt, p_sample, w_in, q_norm_a, k_norm_a, rpb_b, g_out_a, g_out_b, w_out, ln_g, ln_b, w_router, b_router, w_gate, w_up, w_down, w_ple_gate, b_ple_gate, w_ple_proj):
    n_layers = w_in.shape[0]
    w = _prepare_weights(w_in, q_norm_a, k_norm_a, rpb_b, g_out_a, g_out_b, w_out, w_router, b_router,
                         w_gate, w_up, w_down, w_ple_gate, b_ple_gate, w_ple_proj)
    trunks = []
    for x, p in ((x_prompt, p_prompt), (x_sample, p_sample)):
        nb, s, _ = x.shape
        trunks.append((x, p.reshape(n_layers, nb * s, PLE_DIM), _rope_tables_t(s)))

    outs = []
    for x, p_flat, rope in trunks:
        for layer in range(n_layers):
            x = _layer(jnp.full((1,), layer, jnp.int32), x, p_flat, w, ln_g, ln_b, rope)
        outs.append(x)
    return tuple(outs)
```

```python
import functools
import math

import jax
import jax.numpy as jnp
import numpy as np
from jax import lax
from jax.experimental import pallas as pl
from jax.experimental.pallas import tpu as pltpu

F32 = jnp.float32
BF16 = jnp.bfloat16

D_MODEL = 1024
N_LAYERS = 4
HEAD_DIM = 64
A_HEADS = 8
A_KV_HEADS = 2
A_GROUP = A_HEADS // A_KV_HEADS
B_HEADS = 8
A_WIDTH = A_HEADS * HEAD_DIM
KV_WIDTH = A_KV_HEADS * HEAD_DIM
B_WIDTH = B_HEADS * HEAD_DIM
GRID_W = 64
QBLK = 128
WIN_R = 8
WIN_C = 16
ROPE_THETA = 10000.0
N_EXPERTS = 16
N_GROUPS = 4
EXPERTS_PER_GROUP = N_EXPERTS // N_GROUPS
D_FF_EXPERT = 512
PLE_DIM = 256
LN_EPS = 1e-5
QK_EPS = 1e-6
NEG_INF = -1e30
ALPHA = (2 * N_LAYERS) ** 0.25
Q_SCALE = HEAD_DIM ** -0.5
LOG2_E = math.log2(math.e)

LANES = 128

TOKEN_TILE = 1024
ATTN_Q_TILE = 512
ROW_TILE = 1024
POST_ROWS = 128
MOE_TILE = 1024
MOE_SUB = 32
MOE_MAX_UNITS = 12
MOE_TAIL_WINDOWS = ((0, 4), (4, 8), (8, 9), (9, 10), (10, 11))
MOE_PAD_ROWS = max(units - lo - 1 for lo, units in MOE_TAIL_WINDOWS) * MOE_SUB
EXPERTS_PER_STEP = 4
BAND_ROWS = 10
BAND_BLKS = BAND_ROWS * GRID_W // QBLK
N_BIAS_VARIANTS = 5
NATTN_QBLKS = 4
NATTN_FETCH_BLKS = BAND_BLKS + NATTN_QBLKS - 1
NATTN_AHEAD = 3
GATTN_AHEAD = 1
VMEM_LIMIT = 56 * 1024 * 1024

_QA0, _KA0, _VA0, _PT_ROWS = 0, 512, 640, 768
_QN0, _KN0, _VN0, _NAT_COLS = 0, 512, 1024, 1536
ONES_ROWS = 16
VT_ROWS = HEAD_DIM + ONES_ROWS
SHIFT_BOUND_MAX = 40.0


def _params(sem):
    return pltpu.CompilerParams(dimension_semantics=sem, vmem_limit_bytes=VMEM_LIMIT)


def _layer_norm(y, g, b):
    mu = jnp.mean(y, axis=-1, keepdims=True)
    d = y - mu
    var = jnp.mean(d * d, axis=-1, keepdims=True)
    return d * lax.rsqrt(var + LN_EPS) * g + b


def _rope_t(y, cos, sin):
    cr, cc = cos[0:16], cos[16:32]
    sr, sc = sin[0:16], sin[16:32]
    a1, a2, b1, b2 = y[0:16], y[16:32], y[32:48], y[48:64]
    return jnp.concatenate([a1 * cr - a2 * sr, a1 * sr + a2 * cr,
                            b1 * cc - b2 * sc, b1 * sc + b2 * cc], axis=0)


def _norm_rope_t(y, gain, cos, sin):
    ms = jnp.sum(y * y, axis=0, keepdims=True) * (1.0 / HEAD_DIM)
    return _rope_t(y * lax.rsqrt(ms + QK_EPS) * gain, cos, sin)


def _inproj_kernel(lidx_ref, x_ref, wt_ref, wn_ref, cos_ref, sin_ref, qg_ref, kg_ref,
                   qz_ref, ka_ref, vat_ref, qn_ref, kn_ref, vn_ref, qnorm_ref, knorm_ref):
    del lidx_ref
    xb = x_ref[0].astype(BF16)
    pt = lax.dot_general(wt_ref[0], xb, (((1,), (1,)), ((), ())),
                         preferred_element_type=F32)
    tm = xb.shape[0]
    cos = cos_ref[...]
    sin = sin_ref[...]
    qg = qg_ref[0] * (Q_SCALE * LOG2_E)
    kg = kg_ref[0]
    zeros = jnp.zeros((HEAD_DIM, tm), BF16)
    head_row = lax.broadcasted_iota(jnp.int32, (A_HEADS, 1), 0)
    l2 = lambda y: jnp.sqrt(jnp.sum(y * y, axis=0, keepdims=True))
    qnorm = jnp.zeros((A_HEADS, tm), F32)
    for h in range(A_HEADS):
        qf = _norm_rope_t(pt[_QA0 + h * HEAD_DIM:_QA0 + (h + 1) * HEAD_DIM], qg, cos, sin)
        qnorm = jnp.where(head_row == h, l2(qf), qnorm)
        q = qf.astype(BF16)
        if h // A_GROUP == 0:
            qz_ref[0, h] = jnp.concatenate([q, zeros], axis=0)
        else:
            qz_ref[0, h] = jnp.concatenate([zeros, q], axis=0)
    qnorm_ref[0] = qnorm
    ks = [_norm_rope_t(pt[_KA0 + g * HEAD_DIM:_KA0 + (g + 1) * HEAD_DIM], kg, cos, sin) for g in range(A_KV_HEADS)]
    knorm = jnp.zeros((A_HEADS, tm), F32)
    for g in range(A_KV_HEADS):
        knorm = jnp.where(head_row == g, l2(ks[g]), knorm)
    knorm_ref[0, 0] = knorm
    kt = jnp.concatenate(ks, axis=0)
    ka_ref[0, 0] = kt.T.astype(BF16)
    ones = jnp.ones((ONES_ROWS, tm), BF16)
    for g in range(A_KV_HEADS):
        v = pt[_VA0 + g * HEAD_DIM:_VA0 + (g + 1) * HEAD_DIM].astype(BF16)
        vat_ref[0, 0, g] = jnp.concatenate([v, ones], axis=0)
    nat = jnp.dot(xb, wn_ref[0], preferred_element_type=F32)
    qn_ref[0] = (nat[:, _QN0:_QN0 + B_WIDTH] * (Q_SCALE * LOG2_E)).astype(BF16)
    kn_ref[0] = nat[:, _KN0:_KN0 + B_WIDTH].astype(BF16).reshape(tm // QBLK, QBLK, B_WIDTH)
    vn_ref[0] = nat[:, _VN0:_VN0 + B_WIDTH].astype(BF16).reshape(tm // QBLK, QBLK, B_WIDTH)


def _inproj(lidx, x, wt, wn, cos_t, sin_t, qg, kg):
    nb, s, _ = x.shape
    tm = TOKEN_TILE
    nc = s // tm
    out_shape = (
        jax.ShapeDtypeStruct((nb, A_HEADS, 2 * HEAD_DIM, s), BF16),
        jax.ShapeDtypeStruct((nb, nc, tm, KV_WIDTH), BF16),
        jax.ShapeDtypeStruct((nb, nc, A_KV_HEADS, VT_ROWS, tm), BF16),
        jax.ShapeDtypeStruct((nb, s, B_WIDTH), BF16),
        jax.ShapeDtypeStruct((nb, s // QBLK, QBLK, B_WIDTH), BF16),
        jax.ShapeDtypeStruct((nb, s // QBLK, QBLK, B_WIDTH), BF16),
        jax.ShapeDtypeStruct((nb, A_HEADS, s), F32),
        jax.ShapeDtypeStruct((nb, nc, A_HEADS, tm), F32),
    )
    grid_spec = pltpu.PrefetchScalarGridSpec(
        num_scalar_prefetch=1,
        grid=(nb, nc),
        in_specs=[
            pl.BlockSpec((1, tm, D_MODEL), lambda b, c, l: (b, c, 0)),
            pl.BlockSpec((1, _PT_ROWS, D_MODEL), lambda b, c, l: (l[0], 0, 0)),
            pl.BlockSpec((1, D_MODEL, _NAT_COLS), lambda b, c, l: (l[0], 0, 0)),
            pl.BlockSpec((HEAD_DIM // 2, tm), lambda b, c, l: (0, c)),
            pl.BlockSpec((HEAD_DIM // 2, tm), lambda b, c, l: (0, c)),
            pl.BlockSpec((1, HEAD_DIM, 1), lambda b, c, l: (l[0], 0, 0)),
            pl.BlockSpec((1, HEAD_DIM, 1), lambda b, c, l: (l[0], 0, 0)),
        ],
        out_specs=[
            pl.BlockSpec((1, A_HEADS, 2 * HEAD_DIM, tm), lambda b, c, l: (b, 0, 0, c)),
            pl.BlockSpec((1, 1, tm, KV_WIDTH), lambda b, c, l: (b, c, 0, 0)),
            pl.BlockSpec((1, 1, A_KV_HEADS, VT_ROWS, tm), lambda b, c, l: (b, c, 0, 0, 0)),
            pl.BlockSpec((1, tm, B_WIDTH), lambda b, c, l: (b, c, 0)),
            pl.BlockSpec((1, tm // QBLK, QBLK, B_WIDTH), lambda b, c, l: (b, c, 0, 0)),
            pl.BlockSpec((1, tm // QBLK, QBLK, B_WIDTH), lambda b, c, l: (b, c, 0, 0)),
            pl.BlockSpec((1, A_HEADS, tm), lambda b, c, l: (b, 0, c)),
            pl.BlockSpec((1, 1, A_HEADS, tm), lambda b, c, l: (b, c, 0, 0)),
        ],
    )
    return pl.pallas_call(
        _inproj_kernel, out_shape=out_shape, grid_spec=grid_spec,
        compiler_params=_params(("parallel", "parallel")), name="inproj",
    )(lidx, x, wt, wn, cos_t, sin_t, qg, kg)


def _col_max(st):
    slabs = [st[r:r + 128] for r in range(0, st.shape[0], 128)]
    while len(slabs) > 1:
        slabs = [jnp.maximum(slabs[i], slabs[i + 1]) for i in range(0, len(slabs), 2)]
    return jnp.max(slabs[0], axis=0, keepdims=True)


def _gattn_kernel(lidx_ref, qz_ref, k_ref, vt_ref, qn_ref, kn_ref, ga_ref, o_ref, *scratch):
    del lidx_ref
    m_refs, acc_refs = scratch[:A_HEADS], scratch[A_HEADS:]
    n_chunks = k_ref.shape[1]
    for h in range(A_HEADS):
        acc_refs[h][...] = jnp.zeros(acc_refs[h].shape, F32)

    k_norm_max = jnp.max(jnp.max(kn_ref[0], axis=0), axis=1, keepdims=True)
    row = lax.broadcasted_iota(jnp.int32, (A_HEADS, 1), 0)
    k_of_head = jnp.where(row < A_GROUP, k_norm_max[0:1], k_norm_max[1:2])
    bound = qn_ref[0] * k_of_head
    bounded = jnp.max(bound) <= SHIFT_BOUND_MAX

    def chunk_body(c, carry, online):
        kc = k_ref[0, c]
        scores = lambda h: jnp.dot(kc, qz_ref[0, h], preferred_element_type=F32)
        pending = [scores(h) for h in range(GATTN_AHEAD)]
        for h in range(A_HEADS):
            st = pending.pop(0)
            if h + GATTN_AHEAD < A_HEADS:
                pending.append(scores(h + GATTN_AHEAD))
            vt = vt_ref[0, c, h // A_GROUP]
            if online:
                m_old = m_refs[h][...]
                m_new = jnp.maximum(m_old, _col_max(st))
                alpha = jnp.exp2(m_old - m_new)
                p = jnp.exp2(st - m_new).astype(BF16)
                acc_refs[h][...] = alpha * acc_refs[h][...] + jnp.dot(vt, p, preferred_element_type=F32)
                m_refs[h][...] = m_new
            else:
                p = jnp.exp2(st - m_refs[h][...]).astype(BF16)
                acc_refs[h][...] += jnp.dot(vt, p, preferred_element_type=F32)
        return carry

    @pl.when(bounded)
    def _():
        for h in range(A_HEADS):
            m_refs[h][...] = bound[h:h + 1]
        lax.fori_loop(0, n_chunks, functools.partial(chunk_body, online=False), 0)

    @pl.when(jnp.logical_not(bounded))
    def _():
        for h in range(A_HEADS):
            m_refs[h][...] = jnp.full(m_refs[h].shape, -jnp.inf, F32)
        lax.fori_loop(0, n_chunks, functools.partial(chunk_body, online=True), 0)

    outs = []
    for h in range(A_HEADS):
        acc = acc_refs[h][...]
        outs.append(acc[0:HEAD_DIM] * (1.0 / acc[HEAD_DIM:HEAD_DIM + 1]))
    ot = jnp.concatenate(outs, axis=0)
    ms = jnp.sum(ot * ot, axis=0, keepdims=True) * (1.0 / A_WIDTH)
    o_ref[0] = (ot * lax.rsqrt(ms + LN_EPS) * ga_ref[0]).astype(BF16)


def _gattn(lidx, qz, ka, vat, qnorm, knorm, ga_col):
    nb, _, _, s = qz.shape
    nc, tk = ka.shape[1], ka.shape[2]
    tq = ATTN_Q_TILE
    grid_spec = pltpu.PrefetchScalarGridSpec(
        num_scalar_prefetch=1,
        grid=(nb, s // tq),
        in_specs=[
            pl.BlockSpec((1, A_HEADS, 2 * HEAD_DIM, tq), lambda b, i, l: (b, 0, 0, i)),
            pl.BlockSpec((1, nc, tk, KV_WIDTH), lambda b, i, l: (b, 0, 0, 0)),
            pl.BlockSpec((1, nc, A_KV_HEADS, VT_ROWS, tk), lambda b, i, l: (b, 0, 0, 0, 0)),
            pl.BlockSpec((1, A_HEADS, tq), lambda b, i, l: (b, 0, i)),
            pl.BlockSpec((1, nc, A_HEADS, tk), lambda b, i, l: (b, 0, 0, 0)),
            pl.BlockSpec((1, A_WIDTH, 1), lambda b, i, l: (l[0], 0, 0)),
        ],
        out_specs=pl.BlockSpec((1, A_WIDTH, tq), lambda b, i, l: (b, 0, i)),
        scratch_shapes=[pltpu.VMEM((1, tq), F32)] * A_HEADS + [pltpu.VMEM((VT_ROWS, tq), F32)] * A_HEADS,
    )
    return pl.pallas_call(
        _gattn_kernel, out_shape=jax.ShapeDtypeStruct((nb, A_WIDTH, s), BF16), grid_spec=grid_spec,
        compiler_params=_params(("parallel", "parallel")), name="gattn",
    )(lidx, qz, ka, vat, qnorm, knorm, ga_col)


def _nattn_kernel(nblk, q_ref, k_ref, v_ref, *rest):
    bias_refs, o_ref = rest[:NATTN_QBLKS], rest[NATTN_QBLKS]
    first_head = lax.broadcasted_iota(jnp.int32, (QBLK, LANES), 1) < HEAD_DIM
    dn = (((1,), (1,)), ((), ()))
    j0 = pl.program_id(1) * NATTN_QBLKS
    fetched = _fetch_start_blk(j0, nblk)
    work = [(qb, h) for qb in range(NATTN_QBLKS) for h in range(B_HEADS)]
    offs = [_band_start_blk(j0 + qb, nblk) - fetched for qb in range(NATTN_QBLKS)]

    def band(ref, qb, c):
        return jnp.concatenate([ref[0, offs[qb] + t, :, c * LANES:(c + 1) * LANES] for t in range(BAND_BLKS)], axis=0)

    def scores(qb, h):
        c, hh = h // 2, h % 2
        qp = q_ref[0, qb * QBLK:(qb + 1) * QBLK, c * LANES:(c + 1) * LANES]
        qm = jnp.where(first_head if hh == 0 else jnp.logical_not(first_head), qp, jnp.zeros_like(qp))
        return lax.dot_general(qm, band(k_ref, qb, c), dn, preferred_element_type=F32) + bias_refs[qb][0, h]

    outs = []
    pending = [scores(*work[i]) for i in range(NATTN_AHEAD)]
    for i, (qb, h) in enumerate(work):
        s = pending.pop(0)
        if i + NATTN_AHEAD < len(work):
            pending.append(scores(*work[i + NATTN_AHEAD]))
        m = jnp.max(s, axis=-1, keepdims=True)
        p = jnp.exp2(s - m)
        l = jnp.sum(p, axis=-1, keepdims=True)
        outs.append(jnp.dot(p.astype(BF16), band(v_ref, qb, h // 2), preferred_element_type=F32) * (1.0 / l))
    for qb in range(NATTN_QBLKS):
        o = outs[qb * B_HEADS:(qb + 1) * B_HEADS]
        pairs = [jnp.where(first_head, o[2 * c], o[2 * c + 1]) for c in range(B_HEADS // 2)]
        o_ref[0, qb * QBLK:(qb + 1) * QBLK, :] = jnp.concatenate(pairs, axis=-1)


def _band_start_blk(j, nblk):
    rows_per_blk = QBLK // GRID_W
    rows = nblk * rows_per_blk
    bs = jnp.clip(j * rows_per_blk - WIN_R // 2, 0, rows - BAND_ROWS)
    return bs // rows_per_blk


def _bias_variant(j, nblk):
    return jnp.where(j < 2, j, jnp.where(j >= nblk - 2, j - nblk + N_BIAS_VARIANTS, 2))


def _fetch_start_blk(j0, nblk):
    return jnp.minimum(_band_start_blk(j0, nblk), nblk - NATTN_FETCH_BLKS)


def _nattn(lidx, qn, kn, vn, bias):
    nb, s, _ = qn.shape
    nblk = s // QBLK
    tq = NATTN_QBLKS * QBLK
    band_spec = pl.BlockSpec((pl.Element(1), pl.Element(NATTN_FETCH_BLKS), pl.Element(QBLK), pl.Element(B_WIDTH)),
                             lambda b, j, l: (b, _fetch_start_blk(j * NATTN_QBLKS, nblk), 0, 0))
    bias_specs = [pl.BlockSpec((1, B_HEADS, QBLK, BAND_ROWS * GRID_W),
                               functools.partial(lambda b, j, l, qb: (l[0] * N_BIAS_VARIANTS
                                                                      + _bias_variant(j * NATTN_QBLKS + qb, nblk), 0, 0, 0), qb=qb))
                  for qb in range(NATTN_QBLKS)]
    grid_spec = pltpu.PrefetchScalarGridSpec(
        num_scalar_prefetch=1,
        grid=(nb, nblk // NATTN_QBLKS),
        in_specs=[pl.BlockSpec((1, tq, B_WIDTH), lambda b, j, l: (b, j, 0)), band_spec, band_spec] + bias_specs,
        out_specs=pl.BlockSpec((1, tq, B_WIDTH), lambda b, j, l: (b, j, 0)),
    )

    def kernel(lidx_ref, *refs):
        del lidx_ref
        _nattn_kernel(nblk, *refs)

    return pl.pallas_call(
        kernel, out_shape=jax.ShapeDtypeStruct((nb, s, B_WIDTH), F32), grid_spec=grid_spec,
        compiler_params=_params(("parallel", "arbitrary")), name="nattn",
    )(lidx, qn, kn, vn, *([bias] * NATTN_QBLKS))


def _nattn_bias_tables(rpb):
    n_layers = rpb.shape[0]
    n_r, n_c = 2 * WIN_R - 1, 2 * WIN_C - 1
    qc = np.arange(GRID_W)[:, None]
    kc = np.arange(GRID_W)[None, :]
    ic = np.clip(kc - qc + WIN_C - 1, 0, n_c - 1)
    cs = np.clip(qc - WIN_C // 2, 0, GRID_W - WIN_C)
    col_ok = (kc >= cs) & (kc < cs + WIN_C)
    onehot = jnp.asarray((ic.reshape(-1)[None, :] == np.arange(n_c)[:, None]).astype(np.float32))
    cols = jnp.dot(rpb.astype(F32).reshape(-1, n_c), onehot, precision=lax.Precision.HIGHEST)
    cols = cols.reshape(n_layers, B_HEADS, n_r, GRID_W, GRID_W) * LOG2_E
    rows_per_blk = QBLK // GRID_W
    rows = 16
    variants = []
    for r0 in (0, 2, 4, rows - 4, rows - 2):
        bs = min(max(r0 - WIN_R // 2, 0), rows - BAND_ROWS)
        q_rows = []
        for qo in range(rows_per_blk):
            qr = r0 + qo
            rs = min(max(qr - WIN_R // 2, 0), rows - WIN_R)
            k_rows = []
            for ko in range(BAND_ROWS):
                kr = bs + ko
                ir = min(max(kr - qr + WIN_R - 1, 0), n_r - 1)
                if rs <= kr < rs + WIN_R:
                    k_rows.append(jnp.where(jnp.asarray(col_ok)[None, None], cols[:, :, ir], NEG_INF))
                else:
                    k_rows.append(jnp.full((n_layers, B_HEADS, GRID_W, GRID_W), NEG_INF, F32))
            q_rows.append(jnp.concatenate(k_rows, axis=-1))
        variants.append(jnp.concatenate(q_rows, axis=-2))
    t = jnp.stack(variants, axis=1)
    return t.reshape(n_layers * N_BIAS_VARIANTS, B_HEADS, QBLK, BAND_ROWS * GRID_W)


def _group_member(x, d, wrapped):
    return jnp.where(wrapped, pltpu.roll(x, EXPERTS_PER_GROUP - d, 0), pltpu.roll(x, N_EXPERTS - d, 0))


def _route_t(logits_t):
    mx = jnp.max(logits_t, axis=0, keepdims=True)
    ex = jnp.exp(logits_t - mx)
    probs = ex / jnp.sum(ex, axis=0, keepdims=True)
    e_idx = lax.broadcasted_iota(jnp.int32, probs.shape, 0)
    pos = e_idx % EXPERTS_PER_GROUP
    rank = jnp.zeros(probs.shape, jnp.int32)
    for d in range(1, EXPERTS_PER_GROUP):
        wrapped = pos + d >= EXPERTS_PER_GROUP
        other = _group_member(probs, d, wrapped)
        beats = (other > probs) | ((other == probs) & wrapped)
        rank = rank + beats.astype(jnp.int32)
    top = rank < 2
    kept = jnp.where(top, probs, 0.0)
    score = kept
    for d in range(1, EXPERTS_PER_GROUP):
        score = score + _group_member(kept, d, pos + d >= EXPERTS_PER_GROUP)
    grp = e_idx // EXPERTS_PER_GROUP
    lost = jnp.zeros(probs.shape, jnp.bool_)
    for d in range(1, N_GROUPS):
        other = pltpu.roll(score, N_EXPERTS - d * EXPERTS_PER_GROUP, 0)
        earlier = grp + d >= N_GROUPS
        lost = lost | (other > score) | ((other == score) & earlier)
    won = jnp.logical_not(lost)
    gates = jnp.where(top & won, probs / score, 0.0)
    group_onehot = jnp.where(won & (pos == 0), 1.0, 0.0)
    return gates, group_onehot


def _post_attn_kernel(lidx_ref, oat_ref, ob_ref, x_ref, wo_ref, gb_ref, lng_ref, lnb_ref,
                      wrh_ref, wrl_ref, br_ref, h_ref, hb_ref, gates_ref, goh_ref):
    del lidx_ref
    tm = x_ref.shape[0]
    n_part = tm // POST_ROWS
    parts = [pl.ds(r * POST_ROWS, POST_ROWS) for r in range(n_part)]
    dn = (((1,), (1,)), ((), ()))
    dn_t = (((0,), (0,)), ((), ()))

    def rms_b(rows):
        b = ob_ref[rows, :]
        return (b * lax.rsqrt(jnp.mean(b * b, axis=-1, keepdims=True) + LN_EPS) * gb_ref[0]).astype(BF16)

    nbs = [rms_b(rows) for rows in parts]
    mixes = [lax.dot_general(oat_ref[0, :, rows], wo_ref[0, 0:A_WIDTH, :], dn_t, preferred_element_type=F32)
             + jnp.dot(nb_, wo_ref[0, A_WIDTH:, :], preferred_element_type=F32) for rows, nb_ in zip(parts, nbs)]
    hs = [_layer_norm(ALPHA * x_ref[rows, :] + mix, lng_ref[0, 0:1], lnb_ref[0, 0:1]) for rows, mix in zip(parts, mixes)]
    for rows, h in zip(parts, hs):
        h_ref[rows, :] = h
        h_hi = h.astype(BF16)
        hb_ref[rows, :] = h_hi
        h_lo = (h - h_hi.astype(F32)).astype(BF16)
        logits_t = (lax.dot_general(wrh_ref[...], h_hi, dn, preferred_element_type=F32)
                    + lax.dot_general(wrh_ref[...], h_lo, dn, preferred_element_type=F32)
                    + lax.dot_general(wrl_ref[...], h_hi, dn, preferred_element_type=F32)) + br_ref[...]
        gates_t, group_onehot = _route_t(logits_t)
        goh_ref[:, rows] = group_onehot.astype(BF16)
        gates_ref[:, rows] = gates_t


def _post_attn(lidx, oat, ob, x, wo, gb, lng, lnb, wrh, wrl, br):
    t = x.shape[0]
    tm = ROW_TILE
    tiles_per_seq = oat.shape[2] // tm
    row = lambda i, l: (i, 0)
    lay = lambda i, l: (l[0], 0, 0)
    full = lambda i, l: (0, 0)
    grid_spec = pltpu.PrefetchScalarGridSpec(
        num_scalar_prefetch=1,
        grid=(t // tm,),
        in_specs=[
            pl.BlockSpec((1, A_WIDTH, tm), lambda i, l: (i // tiles_per_seq, 0, i % tiles_per_seq)),
            pl.BlockSpec((tm, B_WIDTH), row),
            pl.BlockSpec((tm, D_MODEL), row),
            pl.BlockSpec((1, D_MODEL, D_MODEL), lay),
            pl.BlockSpec((1, 1, B_WIDTH), lay),
            pl.BlockSpec((1, 3, D_MODEL), lay),
            pl.BlockSpec((1, 3, D_MODEL), lay),
            pl.BlockSpec((N_EXPERTS, D_MODEL), full),
            pl.BlockSpec((N_EXPERTS, D_MODEL), full),
            pl.BlockSpec((N_EXPERTS, 1), full),
        ],
        out_specs=[
            pl.BlockSpec((tm, D_MODEL), row),
            pl.BlockSpec((tm, D_MODEL), row),
            pl.BlockSpec((N_EXPERTS, tm), lambda i, l: (0, i)),
            pl.BlockSpec((N_EXPERTS, tm), lambda i, l: (0, i)),
        ],
    )
    return pl.pallas_call(
        _post_attn_kernel,
        out_shape=(jax.ShapeDtypeStruct((t, D_MODEL), F32),
                   jax.ShapeDtypeStruct((t, D_MODEL), BF16),
                   jax.ShapeDtypeStruct((N_EXPERTS, t), F32),
                   jax.ShapeDtypeStruct((N_EXPERTS, t), BF16)),
        grid_spec=grid_spec,
        compiler_params=_params(("parallel",)), name="post_attn",
    )(lidx, oat, ob, x, wo, gb, lng, lnb, wrh, wrl, br)


def _moe_kernel(lidx_ref, bounds_ref, hb_ref, gates_ref, goh_ref, upper_ref, wg_ref, wu_ref, wd_ref, o_ref,
                perm_ref, xs_ref, gs_ref, ys_ref):
    del lidx_ref
    i = pl.program_id(0)
    e = pl.program_id(1)
    tm = hb_ref.shape[0]
    base = i * (2 * N_GROUPS)

    @pl.when(e == 0)
    def _sort():
        goh = goh_ref[...]
        ranks = jnp.dot(goh, upper_ref[...], preferred_element_type=F32)
        row = lax.broadcasted_iota(jnp.int32, (N_EXPERTS, 1), 0)
        start = jnp.zeros((N_EXPERTS, 1), F32)
        for g in range(N_GROUPS):
            start = jnp.where(row == g * EXPERTS_PER_GROUP, bounds_ref[base + g].astype(F32), start)
        pos = jnp.sum(goh.astype(F32) * (ranks + start), axis=0, keepdims=True)
        slot = lax.broadcasted_iota(jnp.int32, (tm, tm), 0)
        perm = jnp.where(slot == pos.astype(jnp.int32), 1.0, 0.0).astype(BF16)
        perm_ref[...] = perm
        xs_ref[0:tm, :] = jnp.dot(perm, hb_ref[...], preferred_element_type=F32).astype(BF16)
        xs_ref[tm:, :] = jnp.zeros((MOE_PAD_ROWS, D_MODEL), BF16)
        gs_ref[tm:, :] = jnp.zeros((MOE_PAD_ROWS, LANES), F32)
        gates_t = gates_ref[...]
        g_hi = gates_t.astype(BF16)
        g_lo = (gates_t - g_hi.astype(F32)).astype(BF16)
        both = lax.dot_general(jnp.concatenate([g_hi, g_lo], axis=0), perm, (((1,), (1,)), ((), ())),
                               preferred_element_type=F32)
        sorted_t = both[0:N_EXPERTS] + both[N_EXPERTS:2 * N_EXPERTS]
        gs_ref[0:tm, :] = jnp.concatenate([sorted_t, jnp.zeros((LANES - N_EXPERTS, tm), F32)], axis=0).T
        ys_ref[...] = jnp.zeros(ys_ref.shape, F32)

    grp = (e * EXPERTS_PER_STEP) // EXPERTS_PER_GROUP
    first = bounds_ref[base + grp] // MOE_SUB
    stop = (bounds_ref[base + N_GROUPS + grp] + MOE_SUB - 1) // MOE_SUB

    def experts(row0, n_rows):
        rows = pl.ds(pl.multiple_of(row0, MOE_SUB), n_rows)
        lane = lax.broadcasted_iota(jnp.int32, (n_rows, LANES), 1)
        x = xs_ref[rows, :]
        gs = gs_ref[rows, :]
        total = None
        for j in range(EXPERTS_PER_STEP):
            gate = jnp.dot(x, wg_ref[0, 0, j], preferred_element_type=F32)
            up = jnp.dot(x, wu_ref[0, 0, j], preferred_element_type=F32)
            act = (gate * jax.nn.sigmoid(gate) * up).astype(BF16)
            y = jnp.dot(act, wd_ref[0, 0, j], preferred_element_type=F32)
            g_col = jnp.sum(jnp.where(lane == e * EXPERTS_PER_STEP + j, gs, 0.0), axis=1, keepdims=True)
            total = g_col * y if total is None else total + g_col * y
        ys_ref[rows, :] += total

    n_full = (stop - first) // MOE_MAX_UNITS
    rest = stop - first - MOE_MAX_UNITS * n_full
    tail = (first + MOE_MAX_UNITS * n_full) * MOE_SUB

    def full_body(s, carry):
        experts((first + MOE_MAX_UNITS * s) * MOE_SUB, MOE_MAX_UNITS * MOE_SUB)
        return carry

    lax.fori_loop(0, n_full, full_body, 0)
    for lo, units in MOE_TAIL_WINDOWS:
        @pl.when((rest > lo) & (rest <= units))
        def _(units=units):
            experts(tail, units * MOE_SUB)

    @pl.when(e == N_EXPERTS // EXPERTS_PER_STEP - 1)
    def _unsort():
        o_ref[...] = lax.dot_general(perm_ref[...], ys_ref[0:tm, :].astype(BF16), (((0,), (0,)), ((), ())),
                                     preferred_element_type=F32)


def _tile_bounds(goh, tm):
    nt = goh.shape[1] // tm
    cnt = goh[::EXPERTS_PER_GROUP].astype(F32).reshape(N_GROUPS, nt, tm).sum(-1).T.astype(jnp.int32)
    end = jnp.cumsum(cnt, axis=1)
    return jnp.concatenate([end - cnt, end], axis=1).reshape(-1)


def _moe(lidx, hb, gates, goh, wg, wu, wd):
    t = hb.shape[0]
    tm = min(MOE_TILE, t)
    bounds = _tile_bounds(goh, tm)
    n_steps = N_EXPERTS // EXPERTS_PER_STEP
    wg, wu, wd = (w.reshape(w.shape[0], n_steps, EXPERTS_PER_STEP, *w.shape[2:]) for w in (wg, wu, wd))
    upper = (jnp.arange(tm)[:, None] < jnp.arange(tm)[None, :]).astype(BF16)
    grid_spec = pltpu.PrefetchScalarGridSpec(
        num_scalar_prefetch=2,
        grid=(t // tm, n_steps),
        in_specs=[
            pl.BlockSpec((tm, D_MODEL), lambda i, e, l, b: (i, 0)),
            pl.BlockSpec((N_EXPERTS, tm), lambda i, e, l, b: (0, i)),
            pl.BlockSpec((N_EXPERTS, tm), lambda i, e, l, b: (0, i)),
            pl.BlockSpec((tm, tm), lambda i, e, l, b: (0, 0)),
            pl.BlockSpec((1, 1, EXPERTS_PER_STEP, D_MODEL, D_FF_EXPERT), lambda i, e, l, b: (l[0], e, 0, 0, 0)),
            pl.BlockSpec((1, 1, EXPERTS_PER_STEP, D_MODEL, D_FF_EXPERT), lambda i, e, l, b: (l[0], e, 0, 0, 0)),
            pl.BlockSpec((1, 1, EXPERTS_PER_STEP, D_FF_EXPERT, D_MODEL), lambda i, e, l, b: (l[0], e, 0, 0, 0)),
        ],
        out_specs=pl.BlockSpec((tm, D_MODEL), lambda i, e, l, b: (i, 0)),
        scratch_shapes=[pltpu.VMEM((tm, tm), BF16), pltpu.VMEM((tm + MOE_PAD_ROWS, D_MODEL), BF16),
                        pltpu.VMEM((tm + MOE_PAD_ROWS, LANES), F32), pltpu.VMEM((tm + MOE_PAD_ROWS, D_MODEL), F32)],
    )
    return pl.pallas_call(
        _moe_kernel, out_shape=jax.ShapeDtypeStruct((t, D_MODEL), F32), grid_spec=grid_spec,
        compiler_params=_params(("parallel", "arbitrary")), name="moe",
    )(lidx, bounds, hb, gates, goh, upper, wg, wu, wd)


def _post_moe_kernel(lidx_ref, h_ref, f_ref, p_ref, wpg_ref, bpg_ref, wpp_ref, lng_ref, lnb_ref, x_ref):
    del lidx_ref
    parts = [pl.ds(r * POST_ROWS, POST_ROWS) for r in range(h_ref.shape[0] // POST_ROWS)]
    h2s = [_layer_norm(ALPHA * h_ref[rows, :] + f_ref[rows, :], lng_ref[0, 1:2], lnb_ref[0, 1:2]) for rows in parts]
    zs = [jnp.dot(h2.astype(BF16), wpg_ref[0], preferred_element_type=F32) + bpg_ref[0] for h2 in h2s]
    projs = [jnp.dot(p_ref[0, rows, :].astype(BF16), wpp_ref[0], preferred_element_type=F32) for rows in parts]
    for rows, h2, z, proj in zip(parts, h2s, zs, projs):
        pe = jax.nn.sigmoid(z) * proj
        x_ref[rows, :] = _layer_norm(ALPHA * h2 + pe, lng_ref[0, 2:3], lnb_ref[0, 2:3])


def _post_moe(lidx, h, ffn, p, wpg, bpg, wpp, lng, lnb):
    t = h.shape[0]
    tm = ROW_TILE
    row = lambda i, l: (i, 0)
    lay = lambda i, l: (l[0], 0, 0)
    grid_spec = pltpu.PrefetchScalarGridSpec(
        num_scalar_prefetch=1,
        grid=(t // tm,),
        in_specs=[
            pl.BlockSpec((tm, D_MODEL), row),
            pl.BlockSpec((tm, D_MODEL), row),
            pl.BlockSpec((1, tm, PLE_DIM), lambda i, l: (l[0], i, 0)),
            pl.BlockSpec((1, D_MODEL, D_MODEL), lay),
            pl.BlockSpec((1, 1, D_MODEL), lay),
            pl.BlockSpec((1, PLE_DIM, D_MODEL), lay),
            pl.BlockSpec((1, 3, D_MODEL), lay),
            pl.BlockSpec((1, 3, D_MODEL), lay),
        ],
        out_specs=pl.BlockSpec((tm, D_MODEL), row),
    )
    return pl.pallas_call(
        _post_moe_kernel, out_shape=jax.ShapeDtypeStruct((t, D_MODEL), F32), grid_spec=grid_spec,
        compiler_params=_params(("parallel",)), name="post_moe",
    )(lidx, h, ffn, p, wpg, bpg, wpp, lng, lnb)


def _rope_tables_t(s):
    t = jnp.arange(s, dtype=jnp.int32)
    row = (t // GRID_W).astype(F32)
    col = (t % GRID_W).astype(F32)
    n_freq = HEAD_DIM // 4
    inv_freq = ROPE_THETA ** (-jnp.arange(n_freq, dtype=F32) / n_freq)
    ang = jnp.concatenate([inv_freq[:, None] * row[None, :], inv_freq[:, None] * col[None, :]], axis=0)
    return jnp.cos(ang), jnp.sin(ang)


def _prepare_weights(w_in, q_norm_a, k_norm_a, rpb_b, g_out_a, g_out_b, w_out, w_router, b_router,
                     w_gate, w_up, w_down, w_ple_gate, b_ple_gate, w_ple_proj):
    n_layers = w_in.shape[0]
    qa, ka, va, qn, kn, vn = jnp.split(w_in, [512, 640, 768, 1280, 1792], axis=-1)
    wt = jnp.swapaxes(jnp.concatenate([qa, ka, va], axis=-1), 1, 2).astype(BF16)
    wn = jnp.concatenate([qn, kn, vn], axis=-1).astype(BF16)
    wr_t = w_router.T
    wr_hi = wr_t.astype(BF16)
    wr_lo = (wr_t - wr_hi.astype(F32)).astype(BF16)
    return dict(
        wt=wt, wn=wn,
        qg=q_norm_a.reshape(n_layers, HEAD_DIM, 1), kg=k_norm_a.reshape(n_layers, HEAD_DIM, 1),
        bias=_nattn_bias_tables(rpb_b),
        ga=g_out_a.reshape(n_layers, A_WIDTH, 1), gb=g_out_b.reshape(n_layers, 1, B_WIDTH),
        wo=w_out.astype(BF16), wr_hi=wr_hi, wr_lo=wr_lo, br=b_router.reshape(N_EXPERTS, 1),
        wg=w_gate.astype(BF16), wu=w_up.astype(BF16), wd=w_down.astype(BF16),
        wpg=w_ple_gate.astype(BF16), bpg=b_ple_gate.reshape(n_layers, 1, D_MODEL), wpp=w_ple_proj.astype(BF16),
    )


def _layer(lidx, x, p_flat, w, ln_g, ln_b, rope):
    nb, s, _ = x.shape
    qz, ka, vat, qn, kn, vn, qnorm, knorm = _inproj(lidx, x, w["wt"], w["wn"], rope[0], rope[1], w["qg"], w["kg"])
    oat = _gattn(lidx, qz, ka, vat, qnorm, knorm, w["ga"])
    ob = _nattn(lidx, qn, kn, vn, w["bias"])
    t = nb * s
    h, hb, gates, goh = _post_attn(lidx, oat, ob.reshape(t, B_WIDTH), x.reshape(t, D_MODEL),
                                   w["wo"], w["gb"], ln_g, ln_b, w["wr_hi"], w["wr_lo"], w["br"])
    ffn = _moe(lidx, hb, gates, goh, w["wg"], w["wu"], w["wd"])
    xo = _post_moe(lidx, h, ffn, p_flat, w["wpg"], w["bpg"], w["wpp"], ln_g, ln_b)
    return xo.reshape(nb, s, D_MODEL)


def kernel(x_prompt, x_sample, p_prompt, p_sample, w_in, q_norm_a, k_norm_a, rpb_b, g_out_a, g_out_b, w_out, ln_g, ln_b, w_router, b_router, w_gate, w_up, w_down, w_ple_gate, b_ple_gate, w_ple_proj):
    n_layers = w_in.shape[0]
    w = _prepare_weights(w_in, q_norm_a, k_norm_a, rpb_b, g_out_a, g_out_b, w_out, w_router, b_router,
                         w_gate, w_up, w_down, w_ple_gate, b_ple_gate, w_ple_proj)
    trunks = []
    for x, p in ((x_prompt, p_prompt), (x_sample, p_sample)):
        nb, s, _ = x.shape
        trunks.append((x, p.reshape(n_layers, nb * s, PLE_DIM), _rope_tables_t(s)))

    outs = []
    for x, p_flat, rope in trunks:
        for layer in range(n_layers):
            x = _layer(jnp.full((1,), layer, jnp.int32), x, p_flat, w, ln_g, ln_b, rope)
        outs.append(x)
    return tuple(outs)
```

```python
import functools
import math

import jax
import jax.numpy as jnp
import numpy as np
from jax import lax
from jax.experimental import pallas as pl
from jax.experimental.pallas import tpu as pltpu

F32 = jnp.float32
BF16 = jnp.bfloat16

D_MODEL = 1024
N_LAYERS = 4
HEAD_DIM = 64
A_HEADS = 8
A_KV_HEADS = 2
A_GROUP = A_HEADS // A_KV_HEADS
B_HEADS = 8
A_WIDTH = A_HEADS * HEAD_DIM
KV_WIDTH = A_KV_HEADS * HEAD_DIM
B_WIDTH = B_HEADS * HEAD_DIM
GRID_W = 64
QBLK = 128
WIN_R = 8
WIN_C = 16
ROPE_THETA = 10000.0
N_EXPERTS = 16
N_GROUPS = 4
EXPERTS_PER_GROUP = N_EXPERTS // N_GROUPS
D_FF_EXPERT = 512
PLE_DIM = 256
LN_EPS = 1e-5
QK_EPS = 1e-6
NEG_INF = -1e30
ALPHA = (2 * N_LAYERS) ** 0.25
Q_SCALE = HEAD_DIM ** -0.5
LOG2_E = math.log2(math.e)

LANES = 128

TOKEN_TILE = 1024
ATTN_Q_TILE = 512
ROW_TILE = 1024
POST_ROWS = 128
MOE_TILE = 1024
MOE_SUB = 32
MOE_MAX_UNITS = 12
MOE_TAIL_WINDOWS = ((0, 4), (4, 8), (8, 9), (9, 10), (10, 11))
MOE_PAD_ROWS = max(units - lo - 1 for lo, units in MOE_TAIL_WINDOWS) * MOE_SUB
EXPERTS_PER_STEP = 4
BAND_ROWS = 10
BAND_BLKS = BAND_ROWS * GRID_W // QBLK
N_BIAS_VARIANTS = 5
NATTN_QBLKS = 8
NATTN_FETCH_BLKS = BAND_BLKS + NATTN_QBLKS - 1
NATTN_AHEAD = 3
GATTN_AHEAD = 1
VMEM_LIMIT = 56 * 1024 * 1024

_QA0, _KA0, _VA0, _PT_ROWS = 0, 512, 640, 768
_QN0, _KN0, _VN0, _NAT_COLS = 0, 512, 1024, 1536
ONES_ROWS = 16
VT_ROWS = HEAD_DIM + ONES_ROWS
SHIFT_BOUND_MAX = 40.0


def _params(sem):
    return pltpu.CompilerParams(dimension_semantics=sem, vmem_limit_bytes=VMEM_LIMIT)


def _layer_norm(y, g, b):
    mu = jnp.mean(y, axis=-1, keepdims=True)
    d = y - mu
    var = jnp.mean(d * d, axis=-1, keepdims=True)
    return d * lax.rsqrt(var + LN_EPS) * g + b


def _rope_t(y, cos, sin):
    cr, cc = cos[0:16], cos[16:32]
    sr, sc = sin[0:16], sin[16:32]
    a1, a2, b1, b2 = y[0:16], y[16:32], y[32:48], y[48:64]
    return jnp.concatenate([a1 * cr - a2 * sr, a1 * sr + a2 * cr,
                            b1 * cc - b2 * sc, b1 * sc + b2 * cc], axis=0)


def _norm_rope_t(y, gain, cos, sin):
    ms = jnp.sum(y * y, axis=0, keepdims=True) * (1.0 / HEAD_DIM)
    return _rope_t(y * lax.rsqrt(ms + QK_EPS) * gain, cos, sin)


def _inproj_kernel(lidx_ref, x_ref, wt_ref, wn_ref, cos_ref, sin_ref, qg_ref, kg_ref,
                   qz_ref, ka_ref, vat_ref, qn_ref, kn_ref, vn_ref, qnorm_ref, knorm_ref):
    del lidx_ref
    xb = x_ref[0].astype(BF16)
    pt = lax.dot_general(wt_ref[0], xb, (((1,), (1,)), ((), ())),
                         preferred_element_type=F32)
    tm = xb.shape[0]
    cos = cos_ref[...]
    sin = sin_ref[...]
    qg = qg_ref[0] * (Q_SCALE * LOG2_E)
    kg = kg_ref[0]
    zeros = jnp.zeros((HEAD_DIM, tm), BF16)
    head_row = lax.broadcasted_iota(jnp.int32, (A_HEADS, 1), 0)
    l2 = lambda y: jnp.sqrt(jnp.sum(y * y, axis=0, keepdims=True))
    qnorm = jnp.zeros((A_HEADS, tm), F32)
    for h in range(A_HEADS):
        qf = _norm_rope_t(pt[_QA0 + h * HEAD_DIM:_QA0 + (h + 1) * HEAD_DIM], qg, cos, sin)
        qnorm = jnp.where(head_row == h, l2(qf), qnorm)
        q = qf.astype(BF16)
        if h // A_GROUP == 0:
            qz_ref[0, h] = jnp.concatenate([q, zeros], axis=0)
        else:
            qz_ref[0, h] = jnp.concatenate([zeros, q], axis=0)
    qnorm_ref[0] = qnorm
    ks = [_norm_rope_t(pt[_KA0 + g * HEAD_DIM:_KA0 + (g + 1) * HEAD_DIM], kg, cos, sin) for g in range(A_KV_HEADS)]
    knorm = jnp.zeros((A_HEADS, tm), F32)
    for g in range(A_KV_HEADS):
        knorm = jnp.where(head_row == g, l2(ks[g]), knorm)
    knorm_ref[0, 0] = knorm
    kt = jnp.concatenate(ks, axis=0)
    ka_ref[0, 0] = kt.T.astype(BF16)
    ones = jnp.ones((ONES_ROWS, tm), BF16)
    for g in range(A_KV_HEADS):
        v = pt[_VA0 + g * HEAD_DIM:_VA0 + (g + 1) * HEAD_DIM].astype(BF16)
        vat_ref[0, 0, g] = jnp.concatenate([v, ones], axis=0)
    nat = jnp.dot(xb, wn_ref[0], preferred_element_type=F32)
    qn_ref[0] = (nat[:, _QN0:_QN0 + B_WIDTH] * (Q_SCALE * LOG2_E)).astype(BF16)
    kn_ref[0] = nat[:, _KN0:_KN0 + B_WIDTH].astype(BF16).reshape(tm // QBLK, QBLK, B_WIDTH)
    vn_ref[0] = nat[:, _VN0:_VN0 + B_WIDTH].astype(BF16).reshape(tm // QBLK, QBLK, B_WIDTH)


def _inproj(lidx, x, wt, wn, cos_t, sin_t, qg, kg):
    nb, s, _ = x.shape
    tm = TOKEN_TILE
    nc = s // tm
    out_shape = (
        jax.ShapeDtypeStruct((nb, A_HEADS, 2 * HEAD_DIM, s), BF16),
        jax.ShapeDtypeStruct((nb, nc, tm, KV_WIDTH), BF16),
        jax.ShapeDtypeStruct((nb, nc, A_KV_HEADS, VT_ROWS, tm), BF16),
        jax.ShapeDtypeStruct((nb, s, B_WIDTH), BF16),
        jax.ShapeDtypeStruct((nb, s // QBLK, QBLK, B_WIDTH), BF16),
        jax.ShapeDtypeStruct((nb, s // QBLK, QBLK, B_WIDTH), BF16),
        jax.ShapeDtypeStruct((nb, A_HEADS, s), F32),
        jax.ShapeDtypeStruct((nb, nc, A_HEADS, tm), F32),
    )
    grid_spec = pltpu.PrefetchScalarGridSpec(
        num_scalar_prefetch=1,
        grid=(nb, nc),
        in_specs=[
            pl.BlockSpec((1, tm, D_MODEL), lambda b, c, l: (b, c, 0)),
            pl.BlockSpec((1, _PT_ROWS, D_MODEL), lambda b, c, l: (l[0], 0, 0)),
            pl.BlockSpec((1, D_MODEL, _NAT_COLS), lambda b, c, l: (l[0], 0, 0)),
            pl.BlockSpec((HEAD_DIM // 2, tm), lambda b, c, l: (0, c)),
            pl.BlockSpec((HEAD_DIM // 2, tm), lambda b, c, l: (0, c)),
            pl.BlockSpec((1, HEAD_DIM, 1), lambda b, c, l: (l[0], 0, 0)),
            pl.BlockSpec((1, HEAD_DIM, 1), lambda b, c, l: (l[0], 0, 0)),
        ],
        out_specs=[
            pl.BlockSpec((1, A_HEADS, 2 * HEAD_DIM, tm), lambda b, c, l: (b, 0, 0, c)),
            pl.BlockSpec((1, 1, tm, KV_WIDTH), lambda b, c, l: (b, c, 0, 0)),
            pl.BlockSpec((1, 1, A_KV_HEADS, VT_ROWS, tm), lambda b, c, l: (b, c, 0, 0, 0)),
            pl.BlockSpec((1, tm, B_WIDTH), lambda b, c, l: (b, c, 0)),
            pl.BlockSpec((1, tm // QBLK, QBLK, B_WIDTH), lambda b, c, l: (b, c, 0, 0)),
            pl.BlockSpec((1, tm // QBLK, QBLK, B_WIDTH), lambda b, c, l: (b, c, 0, 0)),
            pl.BlockSpec((1, A_HEADS, tm), lambda b, c, l: (b, 0, c)),
            pl.BlockSpec((1, 1, A_HEADS, tm), lambda b, c, l: (b, c, 0, 0)),
        ],
    )
    return pl.pallas_call(
        _inproj_kernel, out_shape=out_shape, grid_spec=grid_spec,
        compiler_params=_params(("parallel", "parallel")), name="inproj",
    )(lidx, x, wt, wn, cos_t, sin_t, qg, kg)


def _col_max(st):
    slabs = [st[r:r + 128] for r in range(0, st.shape[0], 128)]
    while len(slabs) > 1:
        slabs = [jnp.maximum(slabs[i], slabs[i + 1]) for i in range(0, len(slabs), 2)]
    return jnp.max(slabs[0], axis=0, keepdims=True)


def _gattn_kernel(lidx_ref, qz_ref, k_ref, vt_ref, qn_ref, kn_ref, ga_ref, o_ref, *scratch):
    del lidx_ref
    m_refs, acc_refs = scratch[:A_HEADS], scratch[A_HEADS:]
    n_chunks = k_ref.shape[1]
    for h in range(A_HEADS):
        acc_refs[h][...] = jnp.zeros(acc_refs[h].shape, F32)

    k_norm_max = jnp.max(jnp.max(kn_ref[0], axis=0), axis=1, keepdims=True)
    row = lax.broadcasted_iota(jnp.int32, (A_HEADS, 1), 0)
    k_of_head = jnp.where(row < A_GROUP, k_norm_max[0:1], k_norm_max[1:2])
    bound = qn_ref[0] * k_of_head
    bounded = jnp.max(bound) <= SHIFT_BOUND_MAX

    def chunk_body(c, carry, online):
        kc = k_ref[0, c]
        scores = lambda h: jnp.dot(kc, qz_ref[0, h], preferred_element_type=F32)
        pending = [scores(h) for h in range(GATTN_AHEAD)]
        for h in range(A_HEADS):
            st = pending.pop(0)
            if h + GATTN_AHEAD < A_HEADS:
                pending.append(scores(h + GATTN_AHEAD))
            vt = vt_ref[0, c, h // A_GROUP]
            if online:
                m_old = m_refs[h][...]
                m_new = jnp.maximum(m_old, _col_max(st))
                alpha = jnp.exp2(m_old - m_new)
                p = jnp.exp2(st - m_new).astype(BF16)
                acc_refs[h][...] = alpha * acc_refs[h][...] + jnp.dot(vt, p, preferred_element_type=F32)
                m_refs[h][...] = m_new
            else:
                p = jnp.exp2(st - m_refs[h][...]).astype(BF16)
                acc_refs[h][...] += jnp.dot(vt, p, preferred_element_type=F32)
        return carry

    @pl.when(bounded)
    def _():
        for h in range(A_HEADS):
            m_refs[h][...] = bound[h:h + 1]
        lax.fori_loop(0, n_chunks, functools.partial(chunk_body, online=False), 0)

    @pl.when(jnp.logical_not(bounded))
    def _():
        for h in range(A_HEADS):
            m_refs[h][...] = jnp.full(m_refs[h].shape, -jnp.inf, F32)
        lax.fori_loop(0, n_chunks, functools.partial(chunk_body, online=True), 0)

    outs = []
    for h in range(A_HEADS):
        acc = acc_refs[h][...]
        outs.append(acc[0:HEAD_DIM] * (1.0 / acc[HEAD_DIM:HEAD_DIM + 1]))
    ot = jnp.concatenate(outs, axis=0)
    ms = jnp.sum(ot * ot, axis=0, keepdims=True) * (1.0 / A_WIDTH)
    o_ref[0] = (ot * lax.rsqrt(ms + LN_EPS) * ga_ref[0]).astype(BF16)


def _gattn(lidx, qz, ka, vat, qnorm, knorm, ga_col):
    nb, _, _, s = qz.shape
    nc, tk = ka.shape[1], ka.shape[2]
    tq = ATTN_Q_TILE
    grid_spec = pltpu.PrefetchScalarGridSpec(
        num_scalar_prefetch=1,
        grid=(nb, s // tq),
        in_specs=[
            pl.BlockSpec((1, A_HEADS, 2 * HEAD_DIM, tq), lambda b, i, l: (b, 0, 0, i)),
            pl.BlockSpec((1, nc, tk, KV_WIDTH), lambda b, i, l: (b, 0, 0, 0)),
            pl.BlockSpec((1, nc, A_KV_HEADS, VT_ROWS, tk), lambda b, i, l: (b, 0, 0, 0, 0)),
            pl.BlockSpec((1, A_HEADS, tq), lambda b, i, l: (b, 0, i)),
            pl.BlockSpec((1, nc, A_HEADS, tk), lambda b, i, l: (b, 0, 0, 0)),
            pl.BlockSpec((1, A_WIDTH, 1), lambda b, i, l: (l[0], 0, 0)),
        ],
        out_specs=pl.BlockSpec((1, A_WIDTH, tq), lambda b, i, l: (b, 0, i)),
        scratch_shapes=[pltpu.VMEM((1, tq), F32)] * A_HEADS + [pltpu.VMEM((VT_ROWS, tq), F32)] * A_HEADS,
    )
    return pl.pallas_call(
        _gattn_kernel, out_shape=jax.ShapeDtypeStruct((nb, A_WIDTH, s), BF16), grid_spec=grid_spec,
        compiler_params=_params(("parallel", "parallel")), name="gattn",
    )(lidx, qz, ka, vat, qnorm, knorm, ga_col)


def _nattn_kernel(nblk, q_ref, k_ref, v_ref, bias_ref, o_ref):
    first_head = lax.broadcasted_iota(jnp.int32, (QBLK, LANES), 1) < HEAD_DIM
    dn = (((1,), (1,)), ((), ()))
    j0 = pl.program_id(1) * NATTN_QBLKS
    fetched = _fetch_start_blk(j0, nblk)
    work = [(qb, h) for qb in range(NATTN_QBLKS) for h in range(B_HEADS)]
    offs = [_band_start_blk(j0 + qb, nblk) - fetched for qb in range(NATTN_QBLKS)]
    variants = [_bias_variant(j0 + qb, nblk) for qb in range(NATTN_QBLKS)]

    def band(ref, qb, c):
        return jnp.concatenate([ref[0, offs[qb] + t, :, c * LANES:(c + 1) * LANES] for t in range(BAND_BLKS)], axis=0)

    def scores(qb, h):
        c, hh = h // 2, h % 2
        qp = q_ref[0, qb * QBLK:(qb + 1) * QBLK, c * LANES:(c + 1) * LANES]
        qm = jnp.where(first_head if hh == 0 else jnp.logical_not(first_head), qp, jnp.zeros_like(qp))
        return lax.dot_general(qm, band(k_ref, qb, c), dn, preferred_element_type=F32) + bias_ref[variants[qb], h]

    outs = []
    pending = [scores(*work[i]) for i in range(NATTN_AHEAD)]
    for i, (qb, h) in enumerate(work):
        s = pending.pop(0)
        if i + NATTN_AHEAD < len(work):
            pending.append(scores(*work[i + NATTN_AHEAD]))
        m = jnp.max(s, axis=-1, keepdims=True)
        p = jnp.exp2(s - m)
        l = jnp.sum(p, axis=-1, keepdims=True)
        outs.append(jnp.dot(p.astype(BF16), band(v_ref, qb, h // 2), preferred_element_type=F32) * (1.0 / l))
    for qb in range(NATTN_QBLKS):
        o = outs[qb * B_HEADS:(qb + 1) * B_HEADS]
        pairs = [jnp.where(first_head, o[2 * c], o[2 * c + 1]) for c in range(B_HEADS // 2)]
        o_ref[0, qb * QBLK:(qb + 1) * QBLK, :] = jnp.concatenate(pairs, axis=-1)


def _band_start_blk(j, nblk):
    rows_per_blk = QBLK // GRID_W
    rows = nblk * rows_per_blk
    bs = jnp.clip(j * rows_per_blk - WIN_R // 2, 0, rows - BAND_ROWS)
    return bs // rows_per_blk


def _bias_variant(j, nblk):
    return jnp.where(j < 2, j, jnp.where(j >= nblk - 2, j - nblk + N_BIAS_VARIANTS, 2))


def _fetch_start_blk(j0, nblk):
    return jnp.minimum(_band_start_blk(j0, nblk), nblk - NATTN_FETCH_BLKS)


def _nattn(lidx, qn, kn, vn, bias):
    nb, s, _ = qn.shape
    nblk = s // QBLK
    assert nblk % NATTN_QBLKS == 0 and nblk >= NATTN_FETCH_BLKS, "sequence too short for the neighbourhood tiling"
    tq = NATTN_QBLKS * QBLK
    band_spec = pl.BlockSpec((pl.Element(1), pl.Element(NATTN_FETCH_BLKS), pl.Element(QBLK), pl.Element(B_WIDTH)),
                             lambda b, j, l: (b, _fetch_start_blk(j * NATTN_QBLKS, nblk), 0, 0))
    bias_spec = pl.BlockSpec((N_BIAS_VARIANTS, B_HEADS, QBLK, BAND_ROWS * GRID_W), lambda b, j, l: (l[0], 0, 0, 0))
    grid_spec = pltpu.PrefetchScalarGridSpec(
        num_scalar_prefetch=1,
        grid=(nb, nblk // NATTN_QBLKS),
        in_specs=[pl.BlockSpec((1, tq, B_WIDTH), lambda b, j, l: (b, j, 0)), band_spec, band_spec, bias_spec],
        out_specs=pl.BlockSpec((1, tq, B_WIDTH), lambda b, j, l: (b, j, 0)),
    )

    def kernel(lidx_ref, *refs):
        del lidx_ref
        _nattn_kernel(nblk, *refs)

    return pl.pallas_call(
        kernel, out_shape=jax.ShapeDtypeStruct((nb, s, B_WIDTH), F32), grid_spec=grid_spec,
        compiler_params=_params(("parallel", "arbitrary")), name="nattn",
    )(lidx, qn, kn, vn, bias)


def _nattn_bias_tables(rpb):
    n_layers = rpb.shape[0]
    n_r, n_c = 2 * WIN_R - 1, 2 * WIN_C - 1
    qc = np.arange(GRID_W)[:, None]
    kc = np.arange(GRID_W)[None, :]
    ic = np.clip(kc - qc + WIN_C - 1, 0, n_c - 1)
    cs = np.clip(qc - WIN_C // 2, 0, GRID_W - WIN_C)
    col_ok = (kc >= cs) & (kc < cs + WIN_C)
    onehot = jnp.asarray((ic.reshape(-1)[None, :] == np.arange(n_c)[:, None]).astype(np.float32))
    cols = jnp.dot(rpb.astype(F32).reshape(-1, n_c), onehot, precision=lax.Precision.HIGHEST)
    cols = cols.reshape(n_layers, B_HEADS, n_r, GRID_W, GRID_W) * LOG2_E
    rows_per_blk = QBLK // GRID_W
    rows = 16
    variants = []
    for r0 in (0, 2, 4, rows - 4, rows - 2):
        bs = min(max(r0 - WIN_R // 2, 0), rows - BAND_ROWS)
        q_rows = []
        for qo in range(rows_per_blk):
            qr = r0 + qo
            rs = min(max(qr - WIN_R // 2, 0), rows - WIN_R)
            k_rows = []
            for ko in range(BAND_ROWS):
                kr = bs + ko
                ir = min(max(kr - qr + WIN_R - 1, 0), n_r - 1)
                if rs <= kr < rs + WIN_R:
                    k_rows.append(jnp.where(jnp.asarray(col_ok)[None, None], cols[:, :, ir], NEG_INF))
                else:
                    k_rows.append(jnp.full((n_layers, B_HEADS, GRID_W, GRID_W), NEG_INF, F32))
            q_rows.append(jnp.concatenate(k_rows, axis=-1))
        variants.append(jnp.concatenate(q_rows, axis=-2))
    t = jnp.stack(variants, axis=1)
    return t.reshape(n_layers * N_BIAS_VARIANTS, B_HEADS, QBLK, BAND_ROWS * GRID_W)


def _group_member(x, d, wrapped):
    return jnp.where(wrapped, pltpu.roll(x, EXPERTS_PER_GROUP - d, 0), pltpu.roll(x, N_EXPERTS - d, 0))


def _route_t(logits_t):
    mx = jnp.max(logits_t, axis=0, keepdims=True)
    ex = jnp.exp(logits_t - mx)
    probs = ex / jnp.sum(ex, axis=0, keepdims=True)
    e_idx = lax.broadcasted_iota(jnp.int32, probs.shape, 0)
    pos = e_idx % EXPERTS_PER_GROUP
    rank = jnp.zeros(probs.shape, jnp.int32)
    for d in range(1, EXPERTS_PER_GROUP):
        wrapped = pos + d >= EXPERTS_PER_GROUP
        other = _group_member(probs, d, wrapped)
        beats = (other > probs) | ((other == probs) & wrapped)
        rank = rank + beats.astype(jnp.int32)
    top = rank < 2
    kept = jnp.where(top, probs, 0.0)
    score = kept
    for d in range(1, EXPERTS_PER_GROUP):
        score = score + _group_member(kept, d, pos + d >= EXPERTS_PER_GROUP)
    grp = e_idx // EXPERTS_PER_GROUP
    lost = jnp.zeros(probs.shape, jnp.bool_)
    for d in range(1, N_GROUPS):
        other = pltpu.roll(score, N_EXPERTS - d * EXPERTS_PER_GROUP, 0)
        earlier = grp + d >= N_GROUPS
        lost = lost | (other > score) | ((other == score) & earlier)
    won = jnp.logical_not(lost)
    gates = jnp.where(top & won, probs / score, 0.0)
    group_onehot = jnp.where(won & (pos == 0), 1.0, 0.0)
    return gates, group_onehot


def _post_attn_kernel(lidx_ref, oat_ref, ob_ref, x_ref, wo_ref, gb_ref, lng_ref, lnb_ref,
                      wrh_ref, wrl_ref, br_ref, h_ref, hb_ref, gates_ref, goh_ref):
    del lidx_ref
    tm = x_ref.shape[0]
    n_part = tm // POST_ROWS
    parts = [pl.ds(r * POST_ROWS, POST_ROWS) for r in range(n_part)]
    dn = (((1,), (1,)), ((), ()))
    dn_t = (((0,), (0,)), ((), ()))

    def rms_b(rows):
        b = ob_ref[rows, :]
        return (b * lax.rsqrt(jnp.mean(b * b, axis=-1, keepdims=True) + LN_EPS) * gb_ref[0]).astype(BF16)

    nbs = [rms_b(rows) for rows in parts]
    mixes = [lax.dot_general(oat_ref[0, :, rows], wo_ref[0, 0:A_WIDTH, :], dn_t, preferred_element_type=F32)
             + jnp.dot(nb_, wo_ref[0, A_WIDTH:, :], preferred_element_type=F32) for rows, nb_ in zip(parts, nbs)]
    hs = [_layer_norm(ALPHA * x_ref[rows, :] + mix, lng_ref[0, 0:1], lnb_ref[0, 0:1]) for rows, mix in zip(parts, mixes)]
    for rows, h in zip(parts, hs):
        h_ref[rows, :] = h
        h_hi = h.astype(BF16)
        hb_ref[rows, :] = h_hi
        h_lo = (h - h_hi.astype(F32)).astype(BF16)
        logits_t = (lax.dot_general(wrh_ref[...], h_hi, dn, preferred_element_type=F32)
                    + lax.dot_general(wrh_ref[...], h_lo, dn, preferred_element_type=F32)
                    + lax.dot_general(wrl_ref[...], h_hi, dn, preferred_element_type=F32)) + br_ref[...]
        gates_t, group_onehot = _route_t(logits_t)
        goh_ref[:, rows] = group_onehot.astype(BF16)
        gates_ref[:, rows] = gates_t


def _post_attn(lidx, oat, ob, x, wo, gb, lng, lnb, wrh, wrl, br):
    t = x.shape[0]
    tm = ROW_TILE
    tiles_per_seq = oat.shape[2] // tm
    row = lambda i, l: (i, 0)
    lay = lambda i, l: (l[0], 0, 0)
    full = lambda i, l: (0, 0)
    grid_spec = pltpu.PrefetchScalarGridSpec(
        num_scalar_prefetch=1,
        grid=(t // tm,),
        in_specs=[
            pl.BlockSpec((1, A_WIDTH, tm), lambda i, l: (i // tiles_per_seq, 0, i % tiles_per_seq)),
            pl.BlockSpec((tm, B_WIDTH), row),
            pl.BlockSpec((tm, D_MODEL), row),
            pl.BlockSpec((1, D_MODEL, D_MODEL), lay),
            pl.BlockSpec((1, 1, B_WIDTH), lay),
            pl.BlockSpec((1, 3, D_MODEL), lay),
            pl.BlockSpec((1, 3, D_MODEL), lay),
            pl.BlockSpec((N_EXPERTS, D_MODEL), full),
            pl.BlockSpec((N_EXPERTS, D_MODEL), full),
            pl.BlockSpec((N_EXPERTS, 1), full),
        ],
        out_specs=[
            pl.BlockSpec((tm, D_MODEL), row),
            pl.BlockSpec((tm, D_MODEL), row),
            pl.BlockSpec((N_EXPERTS, tm), lambda i, l: (0, i)),
            pl.BlockSpec((N_EXPERTS, tm), lambda i, l: (0, i)),
        ],
    )
    return pl.pallas_call(
        _post_attn_kernel,
        out_shape=(jax.ShapeDtypeStruct((t, D_MODEL), F32),
                   jax.ShapeDtypeStruct((t, D_MODEL), BF16),
                   jax.ShapeDtypeStruct((N_EXPERTS, t), F32),
                   jax.ShapeDtypeStruct((N_EXPERTS, t), BF16)),
        grid_spec=grid_spec,
        compiler_params=_params(("parallel",)), name="post_attn",
    )(lidx, oat, ob, x, wo, gb, lng, lnb, wrh, wrl, br)


def _moe_kernel(lidx_ref, bounds_ref, hb_ref, gates_ref, goh_ref, upper_ref, wg_ref, wu_ref, wd_ref, o_ref,
                perm_ref, xs_ref, gs_ref, ys_ref):
    del lidx_ref
    i = pl.program_id(0)
    e = pl.program_id(1)
    tm = hb_ref.shape[0]
    base = i * (2 * N_GROUPS)

    @pl.when(e == 0)
    def _sort():
        goh = goh_ref[...]
        ranks = jnp.dot(goh, upper_ref[...], preferred_element_type=F32)
        row = lax.broadcasted_iota(jnp.int32, (N_EXPERTS, 1), 0)
        start = jnp.zeros((N_EXPERTS, 1), F32)
        for g in range(N_GROUPS):
            start = jnp.where(row == g * EXPERTS_PER_GROUP, bounds_ref[base + g].astype(F32), start)
        pos = jnp.sum(goh.astype(F32) * (ranks + start), axis=0, keepdims=True)
        slot = lax.broadcasted_iota(jnp.int32, (tm, tm), 0)
        perm = jnp.where(slot == pos.astype(jnp.int32), 1.0, 0.0).astype(BF16)
        perm_ref[...] = perm
        xs_ref[0:tm, :] = jnp.dot(perm, hb_ref[...], preferred_element_type=F32).astype(BF16)
        xs_ref[tm:, :] = jnp.zeros((MOE_PAD_ROWS, D_MODEL), BF16)
        gs_ref[tm:, :] = jnp.zeros((MOE_PAD_ROWS, LANES), F32)
        gates_t = gates_ref[...]
        g_hi = gates_t.astype(BF16)
        g_lo = (gates_t - g_hi.astype(F32)).astype(BF16)
        both = lax.dot_general(jnp.concatenate([g_hi, g_lo], axis=0), perm, (((1,), (1,)), ((), ())),
                               preferred_element_type=F32)
        sorted_t = both[0:N_EXPERTS] + both[N_EXPERTS:2 * N_EXPERTS]
        gs_ref[0:tm, :] = jnp.concatenate([sorted_t, jnp.zeros((LANES - N_EXPERTS, tm), F32)], axis=0).T
        ys_ref[...] = jnp.zeros(ys_ref.shape, F32)

    grp = (e * EXPERTS_PER_STEP) // EXPERTS_PER_GROUP
    first = bounds_ref[base + grp] // MOE_SUB
    stop = (bounds_ref[base + N_GROUPS + grp] + MOE_SUB - 1) // MOE_SUB

    def experts(row0, n_rows):
        rows = pl.ds(pl.multiple_of(row0, MOE_SUB), n_rows)
        lane = lax.broadcasted_iota(jnp.int32, (n_rows, LANES), 1)
        x = xs_ref[rows, :]
        gs = gs_ref[rows, :]
        total = None
        for j in range(EXPERTS_PER_STEP):
            gate = jnp.dot(x, wg_ref[0, 0, j], preferred_element_type=F32)
            up = jnp.dot(x, wu_ref[0, 0, j], preferred_element_type=F32)
            act = (gate * jax.nn.sigmoid(gate) * up).astype(BF16)
            y = jnp.dot(act, wd_ref[0, 0, j], preferred_element_type=F32)
            g_col = jnp.sum(jnp.where(lane == e * EXPERTS_PER_STEP + j, gs, 0.0), axis=1, keepdims=True)
            total = g_col * y if total is None else total + g_col * y
        ys_ref[rows, :] += total

    n_full = (stop - first) // MOE_MAX_UNITS
    rest = stop - first - MOE_MAX_UNITS * n_full
    tail = (first + MOE_MAX_UNITS * n_full) * MOE_SUB

    def full_body(s, carry):
        experts((first + MOE_MAX_UNITS * s) * MOE_SUB, MOE_MAX_UNITS * MOE_SUB)
        return carry

    lax.fori_loop(0, n_full, full_body, 0)
    for lo, units in MOE_TAIL_WINDOWS:
        @pl.when((rest > lo) & (rest <= units))
        def _(units=units):
            experts(tail, units * MOE_SUB)

    @pl.when(e == N_EXPERTS // EXPERTS_PER_STEP - 1)
    def _unsort():
        o_ref[...] = lax.dot_general(perm_ref[...], ys_ref[0:tm, :].astype(BF16), (((0,), (0,)), ((), ())),
                                     preferred_element_type=F32)


def _tile_bounds(goh, tm):
    nt = goh.shape[1] // tm
    cnt = goh[::EXPERTS_PER_GROUP].astype(F32).reshape(N_GROUPS, nt, tm).sum(-1).T.astype(jnp.int32)
    end = jnp.cumsum(cnt, axis=1)
    return jnp.concatenate([end - cnt, end], axis=1).reshape(-1)


def _moe(lidx, hb, gates, goh, wg, wu, wd):
    t = hb.shape[0]
    tm = min(MOE_TILE, t)
    bounds = _tile_bounds(goh, tm)
    n_steps = N_EXPERTS // EXPERTS_PER_STEP
    wg, wu, wd = (w.reshape(w.shape[0], n_steps, EXPERTS_PER_STEP, *w.shape[2:]) for w in (wg, wu, wd))
    upper = (jnp.arange(tm)[:, None] < jnp.arange(tm)[None, :]).astype(BF16)
    grid_spec = pltpu.PrefetchScalarGridSpec(
        num_scalar_prefetch=2,
        grid=(t // tm, n_steps),
        in_specs=[
            pl.BlockSpec((tm, D_MODEL), lambda i, e, l, b: (i, 0)),
            pl.BlockSpec((N_EXPERTS, tm), lambda i, e, l, b: (0, i)),
            pl.BlockSpec((N_EXPERTS, tm), lambda i, e, l, b: (0, i)),
            pl.BlockSpec((tm, tm), lambda i, e, l, b: (0, 0)),
            pl.BlockSpec((1, 1, EXPERTS_PER_STEP, D_MODEL, D_FF_EXPERT), lambda i, e, l, b: (l[0], e, 0, 0, 0)),
            pl.BlockSpec((1, 1, EXPERTS_PER_STEP, D_MODEL, D_FF_EXPERT), lambda i, e, l, b: (l[0], e, 0, 0, 0)),
            pl.BlockSpec((1, 1, EXPERTS_PER_STEP, D_FF_EXPERT, D_MODEL), lambda i, e, l, b: (l[0], e, 0, 0, 0)),
        ],
        out_specs=pl.BlockSpec((tm, D_MODEL), lambda i, e, l, b: (i, 0)),
        scratch_shapes=[pltpu.VMEM((tm, tm), BF16), pltpu.VMEM((tm + MOE_PAD_ROWS, D_MODEL), BF16),
                        pltpu.VMEM((tm + MOE_PAD_ROWS, LANES), F32), pltpu.VMEM((tm + MOE_PAD_ROWS, D_MODEL), F32)],
    )
    return pl.pallas_call(
        _moe_kernel, out_shape=jax.ShapeDtypeStruct((t, D_MODEL), F32), grid_spec=grid_spec,
        compiler_params=_params(("parallel", "arbitrary")), name="moe",
    )(lidx, bounds, hb, gates, goh, upper, wg, wu, wd)


def _post_moe_kernel(lidx_ref, h_ref, f_ref, p_ref, wpg_ref, bpg_ref, wpp_ref, lng_ref, lnb_ref, x_ref):
    del lidx_ref
    parts = [pl.ds(r * POST_ROWS, POST_ROWS) for r in range(h_ref.shape[0] // POST_ROWS)]
    h2s = [_layer_norm(ALPHA * h_ref[rows, :] + f_ref[rows, :], lng_ref[0, 1:2], lnb_ref[0, 1:2]) for rows in parts]
    zs = [jnp.dot(h2.astype(BF16), wpg_ref[0], preferred_element_type=F32) + bpg_ref[0] for h2 in h2s]
    projs = [jnp.dot(p_ref[0, rows, :].astype(BF16), wpp_ref[0], preferred_element_type=F32) for rows in parts]
    for rows, h2, z, proj in zip(parts, h2s, zs, projs):
        pe = jax.nn.sigmoid(z) * proj
        x_ref[rows, :] = _layer_norm(ALPHA * h2 + pe, lng_ref[0, 2:3], lnb_ref[0, 2:3])


def _post_moe(lidx, h, ffn, p, wpg, bpg, wpp, lng, lnb):
    t = h.shape[0]
    tm = ROW_TILE
    row = lambda i, l: (i, 0)
    lay = lambda i, l: (l[0], 0, 0)
    grid_spec = pltpu.PrefetchScalarGridSpec(
        num_scalar_prefetch=1,
        grid=(t // tm,),
        in_specs=[
            pl.BlockSpec((tm, D_MODEL), row),
            pl.BlockSpec((tm, D_MODEL), row),
            pl.BlockSpec((1, tm, PLE_DIM), lambda i, l: (l[0], i, 0)),
            pl.BlockSpec((1, D_MODEL, D_MODEL), lay),
            pl.BlockSpec((1, 1, D_MODEL), lay),
            pl.BlockSpec((1, PLE_DIM, D_MODEL), lay),
            pl.BlockSpec((1, 3, D_MODEL), lay),
            pl.BlockSpec((1, 3, D_MODEL), lay),
        ],
        out_specs=pl.BlockSpec((tm, D_MODEL), row),
    )
    return pl.pallas_call(
        _post_moe_kernel, out_shape=jax.ShapeDtypeStruct((t, D_MODEL), F32), grid_spec=grid_spec,
        compiler_params=_params(("parallel",)), name="post_moe",
    )(lidx, h, ffn, p, wpg, bpg, wpp, lng, lnb)


def _rope_tables_t(s):
    t = jnp.arange(s, dtype=jnp.int32)
    row = (t // GRID_W).astype(F32)
    col = (t % GRID_W).astype(F32)
    n_freq = HEAD_DIM // 4
    inv_freq = ROPE_THETA ** (-jnp.arange(n_freq, dtype=F32) / n_freq)
    ang = jnp.concatenate([inv_freq[:, None] * row[None, :], inv_freq[:, None] * col[None, :]], axis=0)
    return jnp.cos(ang), jnp.sin(ang)


def _prepare_weights(w_in, q_norm_a, k_norm_a, rpb_b, g_out_a, g_out_b, w_out, w_router, b_router,
                     w_gate, w_up, w_down, w_ple_gate, b_ple_gate, w_ple_proj):
    n_layers = w_in.shape[0]
    qa, ka, va, qn, kn, vn = jnp.split(w_in, [512, 640, 768, 1280, 1792], axis=-1)
    wt = jnp.swapaxes(jnp.concatenate([qa, ka, va], axis=-1), 1, 2).astype(BF16)
    wn = jnp.concatenate([qn, kn, vn], axis=-1).astype(BF16)
    wr_t = w_router.T
    wr_hi = wr_t.astype(BF16)
    wr_lo = (wr_t - wr_hi.astype(F32)).astype(BF16)
    return dict(
        wt=wt, wn=wn,
        qg=q_norm_a.reshape(n_layers, HEAD_DIM, 1), kg=k_norm_a.reshape(n_layers, HEAD_DIM, 1),
        bias=_nattn_bias_tables(rpb_b),
        ga=g_out_a.reshape(n_layers, A_WIDTH, 1), gb=g_out_b.reshape(n_layers, 1, B_WIDTH),
        wo=w_out.astype(BF16), wr_hi=wr_hi, wr_lo=wr_lo, br=b_router.reshape(N_EXPERTS, 1),
        wg=w_gate.astype(BF16), wu=w_up.astype(BF16), wd=w_down.astype(BF16),
        wpg=w_ple_gate.astype(BF16), bpg=b_ple_gate.reshape(n_layers, 1, D_MODEL), wpp=w_ple_proj.astype(BF16),
    )


def _layer(lidx, x, p_flat, w, ln_g, ln_b, rope):
    nb, s, _ = x.shape
    qz, ka, vat, qn, kn, vn, qnorm, knorm = _inproj(lidx, x, w["wt"], w["wn"], rope[0], rope[1], w["qg"], w["kg"])
    oat = _gattn(lidx, qz, ka, vat, qnorm, knorm, w["ga"])
    ob = _nattn(lidx, qn, kn, vn, w["bias"])
    t = nb * s
    h, hb, gates, goh = _post_attn(lidx, oat, ob.reshape(t, B_WIDTH), x.reshape(t, D_MODEL),
                                   w["wo"], w["gb"], ln_g, ln_b, w["wr_hi"], w["wr_lo"], w["br"])
    ffn = _moe(lidx, hb, gates, goh, w["wg"], w["wu"], w["wd"])
    xo = _post_moe(lidx, h, ffn, p_flat, w["wpg"], w["bpg"], w["wpp"], ln_g, ln_b)
    return xo.reshape(nb, s, D_MODEL)


def kernel(x_prompt, x_sample, p_prompt, p_sample, w_in, q_norm_a, k_norm_a, rpb_b, g_out_a, g_out_b, w_out, ln_g, ln_b, w_router, b_router, w_gate, w_up, w_down, w_ple_gate, b_ple_gate, w_ple_proj):
    n_layers = w_in.shape[0]
    w = _prepare_weights(w_in, q_norm_a, k_norm_a, rpb_b, g_out_a, g_out_b, w_out, w_router, b_router,
                         w_gate, w_up, w_down, w_ple_gate, b_ple_gate, w_ple_proj)
    trunks = []
    for x, p in ((x_prompt, p_prompt), (x_sample, p_sample)):
        nb, s, _ = x.shape
        trunks.append((x, p.reshape(n_layers, nb * s, PLE_DIM), _rope_tables_t(s)))

    outs = []
    for x, p_flat, rope in trunks:
        for layer in range(n_layers):
            x = _layer(jnp.full((1,), layer, jnp.int32), x, p_flat, w, ln_g, ln_b, rope)
        outs.append(x)
    return tuple(outs)
```

```python
import functools
import math

import jax
import jax.numpy as jnp
import numpy as np
from jax import lax
from jax.experimental import pallas as pl
from jax.experimental.pallas import tpu as pltpu

F32 = jnp.float32
BF16 = jnp.bfloat16

D_MODEL = 1024
N_LAYERS = 4
HEAD_DIM = 64
A_HEADS = 8
A_KV_HEADS = 2
A_GROUP = A_HEADS // A_KV_HEADS
B_HEADS = 8
A_WIDTH = A_HEADS * HEAD_DIM
KV_WIDTH = A_KV_HEADS * HEAD_DIM
B_WIDTH = B_HEADS * HEAD_DIM
GRID_W = 64
QBLK = 128
WIN_R = 8
WIN_C = 16
ROPE_THETA = 10000.0
N_EXPERTS = 16
N_GROUPS = 4
EXPERTS_PER_GROUP = N_EXPERTS // N_GROUPS
D_FF_EXPERT = 512
PLE_DIM = 256
LN_EPS = 1e-5
QK_EPS = 1e-6
NEG_INF = -1e30
ALPHA = (2 * N_LAYERS) ** 0.25
Q_SCALE = HEAD_DIM ** -0.5
LOG2_E = math.log2(math.e)

LANES = 128

TOKEN_TILE = 1024
ATTN_Q_TILE = 512
ROW_TILE = 1024
POST_ROWS = 128
MOE_TILE = 1024
MOE_SUB = 32
MOE_MAX_UNITS = 12
MOE_TAIL_WINDOWS = ((0, 4), (4, 8), (8, 9), (9, 10), (10, 11))
MOE_PAD_ROWS = max(units - lo - 1 for lo, units in MOE_TAIL_WINDOWS) * MOE_SUB
EXPERTS_PER_STEP = 4
BAND_ROWS = 10
BAND_BLKS = BAND_ROWS * GRID_W // QBLK
N_BIAS_VARIANTS = 5
NATTN_QBLKS = 8
NATTN_FETCH_BLKS = BAND_BLKS + NATTN_QBLKS - 1
NATTN_AHEAD = 3
GATTN_AHEAD = 1
VMEM_LIMIT = 56 * 1024 * 1024

_QA0, _KA0, _VA0, _PT_ROWS = 0, 512, 640, 768
_QN0, _KN0, _VN0, _NAT_COLS = 0, 512, 1024, 1536
ONES_ROWS = 16
VT_ROWS = HEAD_DIM + ONES_ROWS
SHIFT_BOUND_MAX = 40.0


def _params(sem):
    return pltpu.CompilerParams(dimension_semantics=sem, vmem_limit_bytes=VMEM_LIMIT)


def _layer_norm(y, g, b):
    mu = jnp.mean(y, axis=-1, keepdims=True)
    d = y - mu
    var = jnp.mean(d * d, axis=-1, keepdims=True)
    return d * lax.rsqrt(var + LN_EPS) * g + b


def _rope_t(y, cos, sin):
    cr, cc = cos[0:16], cos[16:32]
    sr, sc = sin[0:16], sin[16:32]
    a1, a2, b1, b2 = y[0:16], y[16:32], y[32:48], y[48:64]
    return jnp.concatenate([a1 * cr - a2 * sr, a1 * sr + a2 * cr,
                            b1 * cc - b2 * sc, b1 * sc + b2 * cc], axis=0)


def _norm_rope_t(y, gain, cos, sin):
    ms = jnp.sum(y * y, axis=0, keepdims=True) * (1.0 / HEAD_DIM)
    return _rope_t(y * lax.rsqrt(ms + QK_EPS) * gain, cos, sin)


def _inproj_kernel(lidx_ref, x_ref, wt_ref, wn_ref, cos_ref, sin_ref, qg_ref, kg_ref,
                   qz_ref, ka_ref, vat_ref, qn_ref, kn_ref, vn_ref, qnorm_ref, knorm_ref):
    del lidx_ref
    xb = x_ref[0].astype(BF16)
    pt = lax.dot_general(wt_ref[0], xb, (((1,), (1,)), ((), ())),
                         preferred_element_type=F32)
    tm = xb.shape[0]
    cos = cos_ref[...]
    sin = sin_ref[...]
    qg = qg_ref[0] * (Q_SCALE * LOG2_E)
    kg = kg_ref[0]
    zeros = jnp.zeros((HEAD_DIM, tm), BF16)
    head_row = lax.broadcasted_iota(jnp.int32, (A_HEADS, 1), 0)
    l2 = lambda y: jnp.sqrt(jnp.sum(y * y, axis=0, keepdims=True))
    qnorm = jnp.zeros((A_HEADS, tm), F32)
    for h in range(A_HEADS):
        qf = _norm_rope_t(pt[_QA0 + h * HEAD_DIM:_QA0 + (h + 1) * HEAD_DIM], qg, cos, sin)
        qnorm = jnp.where(head_row == h, l2(qf), qnorm)
        q = qf.astype(BF16)
        if h // A_GROUP == 0:
            qz_ref[0, h] = jnp.concatenate([q, zeros], axis=0)
        else:
            qz_ref[0, h] = jnp.concatenate([zeros, q], axis=0)
    qnorm_ref[0] = qnorm
    ks = [_norm_rope_t(pt[_KA0 + g * HEAD_DIM:_KA0 + (g + 1) * HEAD_DIM], kg, cos, sin) for g in range(A_KV_HEADS)]
    knorm = jnp.zeros((A_HEADS, tm), F32)
    for g in range(A_KV_HEADS):
        knorm = jnp.where(head_row == g, l2(ks[g]), knorm)
    knorm_ref[0, 0] = knorm
    kt = jnp.concatenate(ks, axis=0)
    ka_ref[0, 0] = kt.T.astype(BF16)
    ones = jnp.ones((ONES_ROWS, tm), BF16)
    for g in range(A_KV_HEADS):
        v = pt[_VA0 + g * HEAD_DIM:_VA0 + (g + 1) * HEAD_DIM].astype(BF16)
        vat_ref[0, 0, g] = jnp.concatenate([v, ones], axis=0)
    nat = jnp.dot(xb, wn_ref[0], preferred_element_type=F32)
    qn_ref[0] = (nat[:, _QN0:_QN0 + B_WIDTH] * (Q_SCALE * LOG2_E)).astype(BF16)
    kn_ref[0] = nat[:, _KN0:_KN0 + B_WIDTH].astype(BF16).reshape(tm // QBLK, QBLK, B_WIDTH)
    vn_ref[0] = nat[:, _VN0:_VN0 + B_WIDTH].astype(BF16).reshape(tm // QBLK, QBLK, B_WIDTH)


def _inproj(lidx, x, wt, wn, cos_t, sin_t, qg, kg):
    nb, s, _ = x.shape
    tm = TOKEN_TILE
    nc = s // tm
    out_shape = (
        jax.ShapeDtypeStruct((nb, A_HEADS, 2 * HEAD_DIM, s), BF16),
        jax.ShapeDtypeStruct((nb, nc, tm, KV_WIDTH), BF16),
        jax.ShapeDtypeStruct((nb, nc, A_KV_HEADS, VT_ROWS, tm), BF16),
        jax.ShapeDtypeStruct((nb, s, B_WIDTH), BF16),
        jax.ShapeDtypeStruct((nb, s // QBLK, QBLK, B_WIDTH), BF16),
        jax.ShapeDtypeStruct((nb, s // QBLK, QBLK, B_WIDTH), BF16),
        jax.ShapeDtypeStruct((nb, A_HEADS, s), F32),
        jax.ShapeDtypeStruct((nb, nc, A_HEADS, tm), F32),
    )
    grid_spec = pltpu.PrefetchScalarGridSpec(
        num_scalar_prefetch=1,
        grid=(nb, nc),
        in_specs=[
            pl.BlockSpec((1, tm, D_MODEL), lambda b, c, l: (b, c, 0)),
            pl.BlockSpec((1, _PT_ROWS, D_MODEL), lambda b, c, l: (l[0], 0, 0)),
            pl.BlockSpec((1, D_MODEL, _NAT_COLS), lambda b, c, l: (l[0], 0, 0)),
            pl.BlockSpec((HEAD_DIM // 2, tm), lambda b, c, l: (0, c)),
            pl.BlockSpec((HEAD_DIM // 2, tm), lambda b, c, l: (0, c)),
            pl.BlockSpec((1, HEAD_DIM, 1), lambda b, c, l: (l[0], 0, 0)),
            pl.BlockSpec((1, HEAD_DIM, 1), lambda b, c, l: (l[0], 0, 0)),
        ],
        out_specs=[
            pl.BlockSpec((1, A_HEADS, 2 * HEAD_DIM, tm), lambda b, c, l: (b, 0, 0, c)),
            pl.BlockSpec((1, 1, tm, KV_WIDTH), lambda b, c, l: (b, c, 0, 0)),
            pl.BlockSpec((1, 1, A_KV_HEADS, VT_ROWS, tm), lambda b, c, l: (b, c, 0, 0, 0)),
            pl.BlockSpec((1, tm, B_WIDTH), lambda b, c, l: (b, c, 0)),
            pl.BlockSpec((1, tm // QBLK, QBLK, B_WIDTH), lambda b, c, l: (b, c, 0, 0)),
            pl.BlockSpec((1, tm // QBLK, QBLK, B_WIDTH), lambda b, c, l: (b, c, 0, 0)),
            pl.BlockSpec((1, A_HEADS, tm), lambda b, c, l: (b, 0, c)),
            pl.BlockSpec((1, 1, A_HEADS, tm), lambda b, c, l: (b, c, 0, 0)),
        ],
    )
    return pl.pallas_call(
        _inproj_kernel, out_shape=out_shape, grid_spec=grid_spec,
        compiler_params=_params(("parallel", "parallel")), name="inproj",
    )(lidx, x, wt, wn, cos_t, sin_t, qg, kg)


def _col_max(st):
    slabs = [st[r:r + 128] for r in range(0, st.shape[0], 128)]
    while len(slabs) > 1:
        slabs = [jnp.maximum(slabs[i], slabs[i + 1]) for i in range(0, len(slabs), 2)]
    return jnp.max(slabs[0], axis=0, keepdims=True)


def _gattn_kernel(lidx_ref, qz_ref, k_ref, vt_ref, qn_ref, kn_ref, ga_ref, o_ref, *scratch):
    del lidx_ref
    m_refs, acc_refs = scratch[:A_HEADS], scratch[A_HEADS:]
    n_chunks = k_ref.shape[1]
    for h in range(A_HEADS):
        acc_refs[h][...] = jnp.zeros(acc_refs[h].shape, F32)

    k_norm_max = jnp.max(jnp.max(kn_ref[0], axis=0), axis=1, keepdims=True)
    row = lax.broadcasted_iota(jnp.int32, (A_HEADS, 1), 0)
    k_of_head = jnp.where(row < A_GROUP, k_norm_max[0:1], k_norm_max[1:2])
    bound = qn_ref[0] * k_of_head
    bounded = jnp.max(bound) <= SHIFT_BOUND_MAX

    def chunk_body(c, carry, online):
        kc = k_ref[0, c]
        scores = lambda h: jnp.dot(kc, qz_ref[0, h], preferred_element_type=F32)
        pending = [scores(h) for h in range(GATTN_AHEAD)]
        for h in range(A_HEADS):
            st = pending.pop(0)
            if h + GATTN_AHEAD < A_HEADS:
                pending.append(scores(h + GATTN_AHEAD))
            vt = vt_ref[0, c, h // A_GROUP]
            if online:
                m_old = m_refs[h][...]
                m_new = jnp.maximum(m_old, _col_max(st))
                alpha = jnp.exp2(m_old - m_new)
                p = jnp.exp2(st - m_new).astype(BF16)
                acc_refs[h][...] = alpha * acc_refs[h][...] + jnp.dot(vt, p, preferred_element_type=F32)
                m_refs[h][...] = m_new
            else:
                p = jnp.exp2(st - m_refs[h][...]).astype(BF16)
                acc_refs[h][...] += jnp.dot(vt, p, preferred_element_type=F32)
        return carry

    @pl.when(bounded)
    def _():
        work = [(c, h) for c in range(n_chunks) for h in range(A_HEADS)]
        scores = lambda c, h: jnp.dot(k_ref[0, c], qz_ref[0, h], preferred_element_type=F32)
        pending = [scores(*work[i]) for i in range(GATTN_AHEAD)]
        for i, (c, h) in enumerate(work):
            st = pending.pop(0)
            if i + GATTN_AHEAD < len(work):
                pending.append(scores(*work[i + GATTN_AHEAD]))
            p = jnp.exp2(st - bound[h:h + 1]).astype(BF16)
            acc_refs[h][...] += jnp.dot(vt_ref[0, c, h // A_GROUP], p, preferred_element_type=F32)

    @pl.when(jnp.logical_not(bounded))
    def _():
        for h in range(A_HEADS):
            m_refs[h][...] = jnp.full(m_refs[h].shape, -jnp.inf, F32)
        lax.fori_loop(0, n_chunks, functools.partial(chunk_body, online=True), 0)

    outs = []
    for h in range(A_HEADS):
        acc = acc_refs[h][...]
        outs.append(acc[0:HEAD_DIM] * (1.0 / acc[HEAD_DIM:HEAD_DIM + 1]))
    ot = jnp.concatenate(outs, axis=0)
    ms = jnp.sum(ot * ot, axis=0, keepdims=True) * (1.0 / A_WIDTH)
    o_ref[0] = (ot * lax.rsqrt(ms + LN_EPS) * ga_ref[0]).astype(BF16)


def _gattn(lidx, qz, ka, vat, qnorm, knorm, ga_col):
    nb, _, _, s = qz.shape
    nc, tk = ka.shape[1], ka.shape[2]
    tq = ATTN_Q_TILE
    grid_spec = pltpu.PrefetchScalarGridSpec(
        num_scalar_prefetch=1,
        grid=(nb, s // tq),
        in_specs=[
            pl.BlockSpec((1, A_HEADS, 2 * HEAD_DIM, tq), lambda b, i, l: (b, 0, 0, i)),
            pl.BlockSpec((1, nc, tk, KV_WIDTH), lambda b, i, l: (b, 0, 0, 0)),
            pl.BlockSpec((1, nc, A_KV_HEADS, VT_ROWS, tk), lambda b, i, l: (b, 0, 0, 0, 0)),
            pl.BlockSpec((1, A_HEADS, tq), lambda b, i, l: (b, 0, i)),
            pl.BlockSpec((1, nc, A_HEADS, tk), lambda b, i, l: (b, 0, 0, 0)),
            pl.BlockSpec((1, A_WIDTH, 1), lambda b, i, l: (l[0], 0, 0)),
        ],
        out_specs=pl.BlockSpec((1, A_WIDTH, tq), lambda b, i, l: (b, 0, i)),
        scratch_shapes=[pltpu.VMEM((1, tq), F32)] * A_HEADS + [pltpu.VMEM((VT_ROWS, tq), F32)] * A_HEADS,
    )
    return pl.pallas_call(
        _gattn_kernel, out_shape=jax.ShapeDtypeStruct((nb, A_WIDTH, s), BF16), grid_spec=grid_spec,
        compiler_params=_params(("parallel", "parallel")), name="gattn",
    )(lidx, qz, ka, vat, qnorm, knorm, ga_col)


def _nattn_kernel(nblk, q_ref, k_ref, v_ref, bias_ref, o_ref):
    first_head = lax.broadcasted_iota(jnp.int32, (QBLK, LANES), 1) < HEAD_DIM
    dn = (((1,), (1,)), ((), ()))
    j0 = pl.program_id(1) * NATTN_QBLKS
    fetched = _fetch_start_blk(j0, nblk)
    work = [(qb, h) for qb in range(NATTN_QBLKS) for h in range(B_HEADS)]
    offs = [_band_start_blk(j0 + qb, nblk) - fetched for qb in range(NATTN_QBLKS)]
    variants = [_bias_variant(j0 + qb, nblk) for qb in range(NATTN_QBLKS)]

    def band(ref, qb, c):
        return jnp.concatenate([ref[0, offs[qb] + t, :, c * LANES:(c + 1) * LANES] for t in range(BAND_BLKS)], axis=0)

    def scores(qb, h):
        c, hh = h // 2, h % 2
        qp = q_ref[0, qb * QBLK:(qb + 1) * QBLK, c * LANES:(c + 1) * LANES]
        qm = jnp.where(first_head if hh == 0 else jnp.logical_not(first_head), qp, jnp.zeros_like(qp))
        return lax.dot_general(qm, band(k_ref, qb, c), dn, preferred_element_type=F32) + bias_ref[variants[qb], h]

    outs = []
    pending = [scores(*work[i]) for i in range(NATTN_AHEAD)]
    for i, (qb, h) in enumerate(work):
        s = pending.pop(0)
        if i + NATTN_AHEAD < len(work):
            pending.append(scores(*work[i + NATTN_AHEAD]))
        m = jnp.max(s, axis=-1, keepdims=True)
        p = jnp.exp2(s - m)
        l = jnp.sum(p, axis=-1, keepdims=True)
        outs.append(jnp.dot(p.astype(BF16), band(v_ref, qb, h // 2), preferred_element_type=F32) * (1.0 / l))
    for qb in range(NATTN_QBLKS):
        o = outs[qb * B_HEADS:(qb + 1) * B_HEADS]
        pairs = [jnp.where(first_head, o[2 * c], o[2 * c + 1]) for c in range(B_HEADS // 2)]
        o_ref[0, qb * QBLK:(qb + 1) * QBLK, :] = jnp.concatenate(pairs, axis=-1)


def _band_start_blk(j, nblk):
    rows_per_blk = QBLK // GRID_W
    rows = nblk * rows_per_blk
    bs = jnp.clip(j * rows_per_blk - WIN_R // 2, 0, rows - BAND_ROWS)
    return bs // rows_per_blk


def _bias_variant(j, nblk):
    return jnp.where(j < 2, j, jnp.where(j >= nblk - 2, j - nblk + N_BIAS_VARIANTS, 2))


def _fetch_start_blk(j0, nblk):
    return jnp.minimum(_band_start_blk(j0, nblk), nblk - NATTN_FETCH_BLKS)


def _nattn(lidx, qn, kn, vn, bias):
    nb, s, _ = qn.shape
    nblk = s // QBLK
    assert nblk % NATTN_QBLKS == 0 and nblk >= NATTN_FETCH_BLKS, "sequence too short for the neighbourhood tiling"
    tq = NATTN_QBLKS * QBLK
    band_spec = pl.BlockSpec((pl.Element(1), pl.Element(NATTN_FETCH_BLKS), pl.Element(QBLK), pl.Element(B_WIDTH)),
                             lambda b, j, l: (b, _fetch_start_blk(j * NATTN_QBLKS, nblk), 0, 0))
    bias_spec = pl.BlockSpec((N_BIAS_VARIANTS, B_HEADS, QBLK, BAND_ROWS * GRID_W), lambda b, j, l: (l[0], 0, 0, 0))
    grid_spec = pltpu.PrefetchScalarGridSpec(
        num_scalar_prefetch=1,
        grid=(nb, nblk // NATTN_QBLKS),
        in_specs=[pl.BlockSpec((1, tq, B_WIDTH), lambda b, j, l: (b, j, 0)), band_spec, band_spec, bias_spec],
        out_specs=pl.BlockSpec((1, tq, B_WIDTH), lambda b, j, l: (b, j, 0)),
    )

    def kernel(lidx_ref, *refs):
        del lidx_ref
        _nattn_kernel(nblk, *refs)

    return pl.pallas_call(
        kernel, out_shape=jax.ShapeDtypeStruct((nb, s, B_WIDTH), F32), grid_spec=grid_spec,
        compiler_params=_params(("parallel", "arbitrary")), name="nattn",
    )(lidx, qn, kn, vn, bias)


def _nattn_bias_tables(rpb):
    n_layers = rpb.shape[0]
    n_r, n_c = 2 * WIN_R - 1, 2 * WIN_C - 1
    qc = np.arange(GRID_W)[:, None]
    kc = np.arange(GRID_W)[None, :]
    ic = np.clip(kc - qc + WIN_C - 1, 0, n_c - 1)
    cs = np.clip(qc - WIN_C // 2, 0, GRID_W - WIN_C)
    col_ok = (kc >= cs) & (kc < cs + WIN_C)
    onehot = jnp.asarray((ic.reshape(-1)[None, :] == np.arange(n_c)[:, None]).astype(np.float32))
    cols = jnp.dot(rpb.astype(F32).reshape(-1, n_c), onehot, precision=lax.Precision.HIGHEST)
    cols = cols.reshape(n_layers, B_HEADS, n_r, GRID_W, GRID_W) * LOG2_E
    rows_per_blk = QBLK // GRID_W
    rows = 16
    variants = []
    for r0 in (0, 2, 4, rows - 4, rows - 2):
        bs = min(max(r0 - WIN_R // 2, 0), rows - BAND_ROWS)
        q_rows = []
        for qo in range(rows_per_blk):
            qr = r0 + qo
            rs = min(max(qr - WIN_R // 2, 0), rows - WIN_R)
            k_rows = []
            for ko in range(BAND_ROWS):
                kr = bs + ko
                ir = min(max(kr - qr + WIN_R - 1, 0), n_r - 1)
                if rs <= kr < rs + WIN_R:
                    k_rows.append(jnp.where(jnp.asarray(col_ok)[None, None], cols[:, :, ir], NEG_INF))
                else:
                    k_rows.append(jnp.full((n_layers, B_HEADS, GRID_W, GRID_W), NEG_INF, F32))
            q_rows.append(jnp.concatenate(k_rows, axis=-1))
        variants.append(jnp.concatenate(q_rows, axis=-2))
    t = jnp.stack(variants, axis=1)
    return t.reshape(n_layers * N_BIAS_VARIANTS, B_HEADS, QBLK, BAND_ROWS * GRID_W)


def _group_member(x, d, wrapped):
    return jnp.where(wrapped, pltpu.roll(x, EXPERTS_PER_GROUP - d, 0), pltpu.roll(x, N_EXPERTS - d, 0))


def _route_t(logits_t):
    mx = jnp.max(logits_t, axis=0, keepdims=True)
    ex = jnp.exp(logits_t - mx)
    probs = ex / jnp.sum(ex, axis=0, keepdims=True)
    e_idx = lax.broadcasted_iota(jnp.int32, probs.shape, 0)
    pos = e_idx % EXPERTS_PER_GROUP
    rank = jnp.zeros(probs.shape, jnp.int32)
    for d in range(1, EXPERTS_PER_GROUP):
        wrapped = pos + d >= EXPERTS_PER_GROUP
        other = _group_member(probs, d, wrapped)
        beats = (other > probs) | ((other == probs) & wrapped)
        rank = rank + beats.astype(jnp.int32)
    top = rank < 2
    kept = jnp.where(top, probs, 0.0)
    score = kept
    for d in range(1, EXPERTS_PER_GROUP):
        score = score + _group_member(kept, d, pos + d >= EXPERTS_PER_GROUP)
    grp = e_idx // EXPERTS_PER_GROUP
    lost = jnp.zeros(probs.shape, jnp.bool_)
    for d in range(1, N_GROUPS):
        other = pltpu.roll(score, N_EXPERTS - d * EXPERTS_PER_GROUP, 0)
        earlier = grp + d >= N_GROUPS
        lost = lost | (other > score) | ((other == score) & earlier)
    won = jnp.logical_not(lost)
    gates = jnp.where(top & won, probs / score, 0.0)
    group_onehot = jnp.where(won & (pos == 0), 1.0, 0.0)
    return gates, group_onehot


def _post_attn_kernel(lidx_ref, oat_ref, ob_ref, x_ref, wo_ref, gb_ref, lng_ref, lnb_ref,
                      wrh_ref, wrl_ref, br_ref, h_ref, hb_ref, gates_ref, goh_ref, bounds_ref):
    del lidx_ref
    tm = x_ref.shape[0]
    n_part = tm // POST_ROWS
    parts = [pl.ds(r * POST_ROWS, POST_ROWS) for r in range(n_part)]
    dn = (((1,), (1,)), ((), ()))
    dn_t = (((0,), (0,)), ((), ()))

    def rms_b(rows):
        b = ob_ref[rows, :]
        return (b * lax.rsqrt(jnp.mean(b * b, axis=-1, keepdims=True) + LN_EPS) * gb_ref[0]).astype(BF16)

    nbs = [rms_b(rows) for rows in parts]
    mixes = [lax.dot_general(oat_ref[0, :, rows], wo_ref[0, 0:A_WIDTH, :], dn_t, preferred_element_type=F32)
             + jnp.dot(nb_, wo_ref[0, A_WIDTH:, :], preferred_element_type=F32) for rows, nb_ in zip(parts, nbs)]
    hs = [_layer_norm(ALPHA * x_ref[rows, :] + mix, lng_ref[0, 0:1], lnb_ref[0, 0:1]) for rows, mix in zip(parts, mixes)]
    for rows, h in zip(parts, hs):
        h_ref[rows, :] = h
        h_hi = h.astype(BF16)
        hb_ref[rows, :] = h_hi
        h_lo = (h - h_hi.astype(F32)).astype(BF16)
        logits_t = (lax.dot_general(wrh_ref[...], h_hi, dn, preferred_element_type=F32)
                    + lax.dot_general(wrh_ref[...], h_lo, dn, preferred_element_type=F32)
                    + lax.dot_general(wrl_ref[...], h_hi, dn, preferred_element_type=F32)) + br_ref[...]
        gates_t, group_onehot = _route_t(logits_t)
        goh_ref[:, rows] = group_onehot.astype(BF16)
        gates_ref[:, rows] = gates_t
    cnt = jnp.sum(goh_ref[...].astype(F32), axis=1, keepdims=True)
    slot_row = lax.broadcasted_iota(jnp.int32, (2 * N_GROUPS, LANES), 0)
    bounds = jnp.zeros((2 * N_GROUPS, LANES), F32)
    start = jnp.zeros((1, 1), F32)
    for g in range(N_GROUPS):
        end = start + cnt[g * EXPERTS_PER_GROUP:g * EXPERTS_PER_GROUP + 1]
        bounds = jnp.where(slot_row == g, start, jnp.where(slot_row == N_GROUPS + g, end, bounds))
        start = end
    bounds_ref[0] = bounds.astype(jnp.int32)


def _post_attn(lidx, oat, ob, x, wo, gb, lng, lnb, wrh, wrl, br):
    t = x.shape[0]
    tm = ROW_TILE
    tiles_per_seq = oat.shape[2] // tm
    row = lambda i, l: (i, 0)
    lay = lambda i, l: (l[0], 0, 0)
    full = lambda i, l: (0, 0)
    grid_spec = pltpu.PrefetchScalarGridSpec(
        num_scalar_prefetch=1,
        grid=(t // tm,),
        in_specs=[
            pl.BlockSpec((1, A_WIDTH, tm), lambda i, l: (i // tiles_per_seq, 0, i % tiles_per_seq)),
            pl.BlockSpec((tm, B_WIDTH), row),
            pl.BlockSpec((tm, D_MODEL), row),
            pl.BlockSpec((1, D_MODEL, D_MODEL), lay),
            pl.BlockSpec((1, 1, B_WIDTH), lay),
            pl.BlockSpec((1, 3, D_MODEL), lay),
            pl.BlockSpec((1, 3, D_MODEL), lay),
            pl.BlockSpec((N_EXPERTS, D_MODEL), full),
            pl.BlockSpec((N_EXPERTS, D_MODEL), full),
            pl.BlockSpec((N_EXPERTS, 1), full),
        ],
        out_specs=[
            pl.BlockSpec((tm, D_MODEL), row),
            pl.BlockSpec((tm, D_MODEL), row),
            pl.BlockSpec((N_EXPERTS, tm), lambda i, l: (0, i)),
            pl.BlockSpec((N_EXPERTS, tm), lambda i, l: (0, i)),
            pl.BlockSpec((1, 2 * N_GROUPS, LANES), lambda i, l: (i, 0, 0)),
        ],
    )
    return pl.pallas_call(
        _post_attn_kernel,
        out_shape=(jax.ShapeDtypeStruct((t, D_MODEL), F32),
                   jax.ShapeDtypeStruct((t, D_MODEL), BF16),
                   jax.ShapeDtypeStruct((N_EXPERTS, t), F32),
                   jax.ShapeDtypeStruct((N_EXPERTS, t), BF16),
                   jax.ShapeDtypeStruct((t // tm, 2 * N_GROUPS, LANES), jnp.int32)),
        grid_spec=grid_spec,
        compiler_params=_params(("parallel",)), name="post_attn",
    )(lidx, oat, ob, x, wo, gb, lng, lnb, wrh, wrl, br)


def _moe_kernel(lidx_ref, bounds_ref, hb_ref, gates_ref, goh_ref, upper_ref, wg_ref, wu_ref, wd_ref, o_ref,
                perm_ref, xs_ref, gs_ref, ys_ref):
    del lidx_ref
    i = pl.program_id(0)
    e = pl.program_id(1)
    tm = hb_ref.shape[0]
    base = i * (2 * N_GROUPS)

    @pl.when(e == 0)
    def _sort():
        goh = goh_ref[...]
        ranks = jnp.dot(goh, upper_ref[...], preferred_element_type=F32)
        row = lax.broadcasted_iota(jnp.int32, (N_EXPERTS, 1), 0)
        start = jnp.zeros((N_EXPERTS, 1), F32)
        for g in range(N_GROUPS):
            start = jnp.where(row == g * EXPERTS_PER_GROUP, bounds_ref[base + g].astype(F32), start)
        pos = jnp.sum(goh.astype(F32) * (ranks + start), axis=0, keepdims=True)
        slot = lax.broadcasted_iota(jnp.int32, (tm, tm), 0)
        perm = jnp.where(slot == pos.astype(jnp.int32), 1.0, 0.0).astype(BF16)
        perm_ref[...] = perm
        xs_ref[0:tm, :] = jnp.dot(perm, hb_ref[...], preferred_element_type=F32).astype(BF16)
        xs_ref[tm:, :] = jnp.zeros((MOE_PAD_ROWS, D_MODEL), BF16)
        gs_ref[tm:, :] = jnp.zeros((MOE_PAD_ROWS, LANES), F32)
        gates_t = gates_ref[...]
        g_hi = gates_t.astype(BF16)
        g_lo = (gates_t - g_hi.astype(F32)).astype(BF16)
        both = lax.dot_general(jnp.concatenate([g_hi, g_lo], axis=0), perm, (((1,), (1,)), ((), ())),
                               preferred_element_type=F32)
        sorted_t = both[0:N_EXPERTS] + both[N_EXPERTS:2 * N_EXPERTS]
        gs_ref[0:tm, :] = jnp.concatenate([sorted_t, jnp.zeros((LANES - N_EXPERTS, tm), F32)], axis=0).T
        ys_ref[...] = jnp.zeros(ys_ref.shape, F32)

    grp = (e * EXPERTS_PER_STEP) // EXPERTS_PER_GROUP
    first = bounds_ref[base + grp] // MOE_SUB
    stop = (bounds_ref[base + N_GROUPS + grp] + MOE_SUB - 1) // MOE_SUB

    def experts(row0, n_rows):
        rows = pl.ds(pl.multiple_of(row0, MOE_SUB), n_rows)
        lane = lax.broadcasted_iota(jnp.int32, (n_rows, LANES), 1)
        x = xs_ref[rows, :]
        gs = gs_ref[rows, :]
        total = None
        for j in range(EXPERTS_PER_STEP):
            gate = jnp.dot(x, wg_ref[0, 0, j], preferred_element_type=F32)
            up = jnp.dot(x, wu_ref[0, 0, j], preferred_element_type=F32)
            act = (gate * jax.nn.sigmoid(gate) * up).astype(BF16)
            y = jnp.dot(act, wd_ref[0, 0, j], preferred_element_type=F32)
            g_col = jnp.sum(jnp.where(lane == e * EXPERTS_PER_STEP + j, gs, 0.0), axis=1, keepdims=True)
            total = g_col * y if total is None else total + g_col * y
        ys_ref[rows, :] += total

    n_full = (stop - first) // MOE_MAX_UNITS
    rest = stop - first - MOE_MAX_UNITS * n_full
    tail = (first + MOE_MAX_UNITS * n_full) * MOE_SUB

    def full_body(s, carry):
        experts((first + MOE_MAX_UNITS * s) * MOE_SUB, MOE_MAX_UNITS * MOE_SUB)
        return carry

    lax.fori_loop(0, n_full, full_body, 0)
    for lo, units in MOE_TAIL_WINDOWS:
        @pl.when((rest > lo) & (rest <= units))
        def _(units=units):
            experts(tail, units * MOE_SUB)

    @pl.when(e == N_EXPERTS // EXPERTS_PER_STEP - 1)
    def _unsort():
        o_ref[...] = lax.dot_general(perm_ref[...], ys_ref[0:tm, :].astype(BF16), (((0,), (0,)), ((), ())),
                                     preferred_element_type=F32)


def _moe(lidx, hb, gates, goh, tile_bounds, wg, wu, wd):
    t = hb.shape[0]
    tm = MOE_TILE
    assert tm == ROW_TILE and tile_bounds.shape[0] == t // tm, "group bounds come per tile of the token-wise kernel"
    bounds = tile_bounds[:, :, 0].reshape(-1)
    n_steps = N_EXPERTS // EXPERTS_PER_STEP
    wg, wu, wd = (w.reshape(w.shape[0], n_steps, EXPERTS_PER_STEP, *w.shape[2:]) for w in (wg, wu, wd))
    upper = (jnp.arange(tm)[:, None] < jnp.arange(tm)[None, :]).astype(BF16)
    grid_spec = pltpu.PrefetchScalarGridSpec(
        num_scalar_prefetch=2,
        grid=(t // tm, n_steps),
        in_specs=[
            pl.BlockSpec((tm, D_MODEL), lambda i, e, l, b: (i, 0)),
            pl.BlockSpec((N_EXPERTS, tm), lambda i, e, l, b: (0, i)),
            pl.BlockSpec((N_EXPERTS, tm), lambda i, e, l, b: (0, i)),
            pl.BlockSpec((tm, tm), lambda i, e, l, b: (0, 0)),
            pl.BlockSpec((1, 1, EXPERTS_PER_STEP, D_MODEL, D_FF_EXPERT), lambda i, e, l, b: (l[0], e, 0, 0, 0)),
            pl.BlockSpec((1, 1, EXPERTS_PER_STEP, D_MODEL, D_FF_EXPERT), lambda i, e, l, b: (l[0], e, 0, 0, 0)),
            pl.BlockSpec((1, 1, EXPERTS_PER_STEP, D_FF_EXPERT, D_MODEL), lambda i, e, l, b: (l[0], e, 0, 0, 0)),
        ],
        out_specs=pl.BlockSpec((tm, D_MODEL), lambda i, e, l, b: (i, 0)),
        scratch_shapes=[pltpu.VMEM((tm, tm), BF16), pltpu.VMEM((tm + MOE_PAD_ROWS, D_MODEL), BF16),
                        pltpu.VMEM((tm + MOE_PAD_ROWS, LANES), F32), pltpu.VMEM((tm + MOE_PAD_ROWS, D_MODEL), F32)],
    )
    return pl.pallas_call(
        _moe_kernel, out_shape=jax.ShapeDtypeStruct((t, D_MODEL), F32), grid_spec=grid_spec,
        compiler_params=_params(("parallel", "arbitrary")), name="moe",
    )(lidx, bounds, hb, gates, goh, upper, wg, wu, wd)


def _post_moe_kernel(lidx_ref, h_ref, f_ref, p_ref, wpg_ref, bpg_ref, wpp_ref, lng_ref, lnb_ref, x_ref):
    del lidx_ref
    parts = [pl.ds(r * POST_ROWS, POST_ROWS) for r in range(h_ref.shape[0] // POST_ROWS)]
    h2s = [_layer_norm(ALPHA * h_ref[rows, :] + f_ref[rows, :], lng_ref[0, 1:2], lnb_ref[0, 1:2]) for rows in parts]
    zs = [jnp.dot(h2.astype(BF16), wpg_ref[0], preferred_element_type=F32) + bpg_ref[0] for h2 in h2s]
    projs = [jnp.dot(p_ref[0, rows, :].astype(BF16), wpp_ref[0], preferred_element_type=F32) for rows in parts]
    for rows, h2, z, proj in zip(parts, h2s, zs, projs):
        pe = jax.nn.sigmoid(z) * proj
        x_ref[rows, :] = _layer_norm(ALPHA * h2 + pe, lng_ref[0, 2:3], lnb_ref[0, 2:3])


def _post_moe(lidx, h, ffn, p, wpg, bpg, wpp, lng, lnb):
    t = h.shape[0]
    tm = ROW_TILE
    row = lambda i, l: (i, 0)
    lay = lambda i, l: (l[0], 0, 0)
    grid_spec = pltpu.PrefetchScalarGridSpec(
        num_scalar_prefetch=1,
        grid=(t // tm,),
        in_specs=[
            pl.BlockSpec((tm, D_MODEL), row),
            pl.BlockSpec((tm, D_MODEL), row),
            pl.BlockSpec((1, tm, PLE_DIM), lambda i, l: (l[0], i, 0)),
            pl.BlockSpec((1, D_MODEL, D_MODEL), lay),
            pl.BlockSpec((1, 1, D_MODEL), lay),
            pl.BlockSpec((1, PLE_DIM, D_MODEL), lay),
            pl.BlockSpec((1, 3, D_MODEL), lay),
            pl.BlockSpec((1, 3, D_MODEL), lay),
        ],
        out_specs=pl.BlockSpec((tm, D_MODEL), row),
    )
    return pl.pallas_call(
        _post_moe_kernel, out_shape=jax.ShapeDtypeStruct((t, D_MODEL), F32), grid_spec=grid_spec,
        compiler_params=_params(("parallel",)), name="post_moe",
    )(lidx, h, ffn, p, wpg, bpg, wpp, lng, lnb)


def _rope_tables_t(s):
    t = jnp.arange(s, dtype=jnp.int32)
    row = (t // GRID_W).astype(F32)
    col = (t % GRID_W).astype(F32)
    n_freq = HEAD_DIM // 4
    inv_freq = ROPE_THETA ** (-jnp.arange(n_freq, dtype=F32) / n_freq)
    ang = jnp.concatenate([inv_freq[:, None] * row[None, :], inv_freq[:, None] * col[None, :]], axis=0)
    return jnp.cos(ang), jnp.sin(ang)


def _prepare_weights(w_in, q_norm_a, k_norm_a, rpb_b, g_out_a, g_out_b, w_out, w_router, b_router,
                     w_gate, w_up, w_down, w_ple_gate, b_ple_gate, w_ple_proj):
    n_layers = w_in.shape[0]
    qa, ka, va, qn, kn, vn = jnp.split(w_in, [512, 640, 768, 1280, 1792], axis=-1)
    wt = jnp.swapaxes(jnp.concatenate([qa, ka, va], axis=-1), 1, 2).astype(BF16)
    wn = jnp.concatenate([qn, kn, vn], axis=-1).astype(BF16)
    wr_t = w_router.T
    wr_hi = wr_t.astype(BF16)
    wr_lo = (wr_t - wr_hi.astype(F32)).astype(BF16)
    return dict(
        wt=wt, wn=wn,
        qg=q_norm_a.reshape(n_layers, HEAD_DIM, 1), kg=k_norm_a.reshape(n_layers, HEAD_DIM, 1),
        bias=_nattn_bias_tables(rpb_b),
        ga=g_out_a.reshape(n_layers, A_WIDTH, 1), gb=g_out_b.reshape(n_layers, 1, B_WIDTH),
        wo=w_out.astype(BF16), wr_hi=wr_hi, wr_lo=wr_lo, br=b_router.reshape(N_EXPERTS, 1),
        wg=w_gate.astype(BF16), wu=w_up.astype(BF16), wd=w_down.astype(BF16),
        wpg=w_ple_gate.astype(BF16), bpg=b_ple_gate.reshape(n_layers, 1, D_MODEL), wpp=w_ple_proj.astype(BF16),
    )


def _layer(lidx, x, p_flat, w, ln_g, ln_b, rope):
    nb, s, _ = x.shape
    qz, ka, vat, qn, kn, vn, qnorm, knorm = _inproj(lidx, x, w["wt"], w["wn"], rope[0], rope[1], w["qg"], w["kg"])
    oat = _gattn(lidx, qz, ka, vat, qnorm, knorm, w["ga"])
    ob = _nattn(lidx, qn, kn, vn, w["bias"])
    t = nb * s
    h, hb, gates, goh, tile_bounds = _post_attn(lidx, oat, ob.reshape(t, B_WIDTH), x.reshape(t, D_MODEL),
                                   w["wo"], w["gb"], ln_g, ln_b, w["wr_hi"], w["wr_lo"], w["br"])
    ffn = _moe(lidx, hb, gates, goh, tile_bounds, w["wg"], w["wu"], w["wd"])
    xo = _post_moe(lidx, h, ffn, p_flat, w["wpg"], w["bpg"], w["wpp"], ln_g, ln_b)
    return xo.reshape(nb, s, D_MODEL)


def kernel(x_prompt, x_sample, p_prompt, p_sample, w_in, q_norm_a, k_norm_a, rpb_b, g_out_a, g_out_b, w_out, ln_g, ln_b, w_router, b_router, w_gate, w_up, w_down, w_ple_gate, b_ple_gate, w_ple_proj):
    n_layers = w_in.shape[0]
    w = _prepare_weights(w_in, q_norm_a, k_norm_a, rpb_b, g_out_a, g_out_b, w_out, w_router, b_router,
                         w_gate, w_up, w_down, w_ple_gate, b_ple_gate, w_ple_proj)
    trunks = []
    for x, p in ((x_prompt, p_prompt), (x_sample, p_sample)):
        nb, s, _ = x.shape
        trunks.append((x, p.reshape(n_layers, nb * s, PLE_DIM), _rope_tables_t(s)))

    outs = []
    for x, p_flat, rope in trunks:
        for layer in range(n_layers):
            x = _layer(jnp.full((1,), layer, jnp.int32), x, p_flat, w, ln_g, ln_b, rope)
        outs.append(x)
    return tuple(outs)
```

```python
import functools
import math

import jax
import jax.numpy as jnp
import numpy as np
from jax import lax
from jax.experimental import pallas as pl
from jax.experimental.pallas import tpu as pltpu

F32 = jnp.float32
BF16 = jnp.bfloat16

D_MODEL = 1024
N_LAYERS = 4
HEAD_DIM = 64
A_HEADS = 8
A_KV_HEADS = 2
A_GROUP = A_HEADS // A_KV_HEADS
B_HEADS = 8
A_WIDTH = A_HEADS * HEAD_DIM
KV_WIDTH = A_KV_HEADS * HEAD_DIM
B_WIDTH = B_HEADS * HEAD_DIM
GRID_W = 64
QBLK = 128
WIN_R = 8
WIN_C = 16
ROPE_THETA = 10000.0
N_EXPERTS = 16
N_GROUPS = 4
EXPERTS_PER_GROUP = N_EXPERTS // N_GROUPS
D_FF_EXPERT = 512
PLE_DIM = 256
LN_EPS = 1e-5
QK_EPS = 1e-6
NEG_INF = -1e30
ALPHA = (2 * N_LAYERS) ** 0.25
Q_SCALE = HEAD_DIM ** -0.5
LOG2_E = math.log2(math.e)

LANES = 128

TOKEN_TILE = 1024
ATTN_Q_TILE = 512
ROW_TILE = 1024
POST_ROWS = 128
MOE_TILE = 1024
MOE_SUB = 32
MOE_MAX_UNITS = 12
MOE_TAIL_WINDOWS = ((0, 4), (4, 8), (8, 9), (9, 10), (10, 11))
MOE_PAD_ROWS = max(units - lo - 1 for lo, units in MOE_TAIL_WINDOWS) * MOE_SUB
EXPERTS_PER_STEP = 4
BAND_ROWS = 10
BAND_BLKS = BAND_ROWS * GRID_W // QBLK
N_BIAS_VARIANTS = 5
NATTN_QBLKS = 8
NATTN_FETCH_BLKS = BAND_BLKS + NATTN_QBLKS - 1
NATTN_AHEAD = 3
GATTN_AHEAD = 1
VMEM_LIMIT = 56 * 1024 * 1024

_QA0, _KA0, _VA0, _PT_ROWS = 0, 512, 640, 768
_QN0, _KN0, _VN0, _NAT_COLS = 0, 512, 1024, 1536
ONES_ROWS = 16
VT_ROWS = HEAD_DIM + ONES_ROWS
SHIFT_BOUND_MAX = 40.0


def _params(sem):
    return pltpu.CompilerParams(dimension_semantics=sem, vmem_limit_bytes=VMEM_LIMIT)


def _layer_norm(y, g, b):
    mu = jnp.mean(y, axis=-1, keepdims=True)
    d = y - mu
    var = jnp.mean(d * d, axis=-1, keepdims=True)
    return d * lax.rsqrt(var + LN_EPS) * g + b


def _rope_t(y, cos, sin):
    cr, cc = cos[0:16], cos[16:32]
    sr, sc = sin[0:16], sin[16:32]
    a1, a2, b1, b2 = y[0:16], y[16:32], y[32:48], y[48:64]
    return jnp.concatenate([a1 * cr - a2 * sr, a1 * sr + a2 * cr,
                            b1 * cc - b2 * sc, b1 * sc + b2 * cc], axis=0)


def _norm_rope_t(y, gain, cos, sin):
    ms = jnp.sum(y * y, axis=0, keepdims=True) * (1.0 / HEAD_DIM)
    return _rope_t(y * lax.rsqrt(ms + QK_EPS) * gain, cos, sin)


def _inproj_kernel(lidx_ref, x_ref, wt_ref, wn_ref, cos_ref, sin_ref, qg_ref, kg_ref,
                   qz_ref, ka_ref, vat_ref, qn_ref, kn_ref, vn_ref, qnorm_ref, knorm_ref):
    del lidx_ref
    xb = x_ref[0].astype(BF16)
    pt = lax.dot_general(wt_ref[0], xb, (((1,), (1,)), ((), ())),
                         preferred_element_type=F32)
    tm = xb.shape[0]
    cos = cos_ref[...]
    sin = sin_ref[...]
    qg = qg_ref[0] * (Q_SCALE * LOG2_E)
    kg = kg_ref[0]
    zeros = jnp.zeros((HEAD_DIM, tm), BF16)
    head_row = lax.broadcasted_iota(jnp.int32, (A_HEADS, 1), 0)
    l2 = lambda y: jnp.sqrt(jnp.sum(y * y, axis=0, keepdims=True))
    qnorm = jnp.zeros((A_HEADS, tm), F32)
    for h in range(A_HEADS):
        qf = _norm_rope_t(pt[_QA0 + h * HEAD_DIM:_QA0 + (h + 1) * HEAD_DIM], qg, cos, sin)
        qnorm = jnp.where(head_row == h, l2(qf), qnorm)
        q = qf.astype(BF16)
        if h // A_GROUP == 0:
            qz_ref[0, h] = jnp.concatenate([q, zeros], axis=0)
        else:
            qz_ref[0, h] = jnp.concatenate([zeros, q], axis=0)
    qnorm_ref[0] = qnorm
    ks = [_norm_rope_t(pt[_KA0 + g * HEAD_DIM:_KA0 + (g + 1) * HEAD_DIM], kg, cos, sin) for g in range(A_KV_HEADS)]
    knorm = jnp.zeros((A_HEADS, tm), F32)
    for g in range(A_KV_HEADS):
        knorm = jnp.where(head_row == g, l2(ks[g]), knorm)
    knorm_ref[0, 0] = knorm
    kt = jnp.concatenate(ks, axis=0)
    ka_ref[0, 0] = kt.T.astype(BF16)
    ones = jnp.ones((ONES_ROWS, tm), BF16)
    for g in range(A_KV_HEADS):
        v = pt[_VA0 + g * HEAD_DIM:_VA0 + (g + 1) * HEAD_DIM].astype(BF16)
        vat_ref[0, 0, g] = jnp.concatenate([v, ones], axis=0)
    nat = jnp.dot(xb, wn_ref[0], preferred_element_type=F32)
    qn_ref[0] = (nat[:, _QN0:_QN0 + B_WIDTH] * (Q_SCALE * LOG2_E)).astype(BF16)
    kn_ref[0] = nat[:, _KN0:_KN0 + B_WIDTH].astype(BF16).reshape(tm // QBLK, QBLK, B_WIDTH)
    vn_ref[0] = nat[:, _VN0:_VN0 + B_WIDTH].astype(BF16).reshape(tm // QBLK, QBLK, B_WIDTH)


def _inproj(lidx, x, wt, wn, cos_t, sin_t, qg, kg):
    nb, s, _ = x.shape
    tm = TOKEN_TILE
    nc = s // tm
    out_shape = (
        jax.ShapeDtypeStruct((nb, A_HEADS, 2 * HEAD_DIM, s), BF16),
        jax.ShapeDtypeStruct((nb, nc, tm, KV_WIDTH), BF16),
        jax.ShapeDtypeStruct((nb, nc, A_KV_HEADS, VT_ROWS, tm), BF16),
        jax.ShapeDtypeStruct((nb, s, B_WIDTH), BF16),
        jax.ShapeDtypeStruct((nb, s // QBLK, QBLK, B_WIDTH), BF16),
        jax.ShapeDtypeStruct((nb, s // QBLK, QBLK, B_WIDTH), BF16),
        jax.ShapeDtypeStruct((nb, A_HEADS, s), F32),
        jax.ShapeDtypeStruct((nb, nc, A_HEADS, tm), F32),
    )
    grid_spec = pltpu.PrefetchScalarGridSpec(
        num_scalar_prefetch=1,
        grid=(nb, nc),
        in_specs=[
            pl.BlockSpec((1, tm, D_MODEL), lambda b, c, l: (b, c, 0)),
            pl.BlockSpec((1, _PT_ROWS, D_MODEL), lambda b, c, l: (l[0], 0, 0)),
            pl.BlockSpec((1, D_MODEL, _NAT_COLS), lambda b, c, l: (l[0], 0, 0)),
            pl.BlockSpec((HEAD_DIM // 2, tm), lambda b, c, l: (0, c)),
            pl.BlockSpec((HEAD_DIM // 2, tm), lambda b, c, l: (0, c)),
            pl.BlockSpec((1, HEAD_DIM, 1), lambda b, c, l: (l[0], 0, 0)),
            pl.BlockSpec((1, HEAD_DIM, 1), lambda b, c, l: (l[0], 0, 0)),
        ],
        out_specs=[
            pl.BlockSpec((1, A_HEADS, 2 * HEAD_DIM, tm), lambda b, c, l: (b, 0, 0, c)),
            pl.BlockSpec((1, 1, tm, KV_WIDTH), lambda b, c, l: (b, c, 0, 0)),
            pl.BlockSpec((1, 1, A_KV_HEADS, VT_ROWS, tm), lambda b, c, l: (b, c, 0, 0, 0)),
            pl.BlockSpec((1, tm, B_WIDTH), lambda b, c, l: (b, c, 0)),
            pl.BlockSpec((1, tm // QBLK, QBLK, B_WIDTH), lambda b, c, l: (b, c, 0, 0)),
            pl.BlockSpec((1, tm // QBLK, QBLK, B_WIDTH), lambda b, c, l: (b, c, 0, 0)),
            pl.BlockSpec((1, A_HEADS, tm), lambda b, c, l: (b, 0, c)),
            pl.BlockSpec((1, 1, A_HEADS, tm), lambda b, c, l: (b, c, 0, 0)),
        ],
    )
    return pl.pallas_call(
        _inproj_kernel, out_shape=out_shape, grid_spec=grid_spec,
        compiler_params=_params(("parallel", "parallel")), name="inproj",
    )(lidx, x, wt, wn, cos_t, sin_t, qg, kg)


def _col_max(st):
    slabs = [st[r:r + 128] for r in range(0, st.shape[0], 128)]
    while len(slabs) > 1:
        slabs = [jnp.maximum(slabs[i], slabs[i + 1]) for i in range(0, len(slabs), 2)]
    return jnp.max(slabs[0], axis=0, keepdims=True)


def _gattn_kernel(lidx_ref, qz_ref, k_ref, vt_ref, qn_ref, kn_ref, ga_ref, o_ref, *scratch):
    del lidx_ref
    m_refs, acc_refs = scratch[:A_HEADS], scratch[A_HEADS:]
    n_chunks = k_ref.shape[1]
    for h in range(A_HEADS):
        acc_refs[h][...] = jnp.zeros(acc_refs[h].shape, F32)

    k_norm_max = jnp.max(jnp.max(kn_ref[0], axis=0), axis=1, keepdims=True)
    row = lax.broadcasted_iota(jnp.int32, (A_HEADS, 1), 0)
    k_of_head = jnp.where(row < A_GROUP, k_norm_max[0:1], k_norm_max[1:2])
    bound = qn_ref[0] * k_of_head
    bounded = jnp.max(bound) <= SHIFT_BOUND_MAX

    def chunk_body(c, carry, online):
        kc = k_ref[0, c]
        scores = lambda h: jnp.dot(kc, qz_ref[0, h], preferred_element_type=F32)
        pending = [scores(h) for h in range(GATTN_AHEAD)]
        for h in range(A_HEADS):
            st = pending.pop(0)
            if h + GATTN_AHEAD < A_HEADS:
                pending.append(scores(h + GATTN_AHEAD))
            vt = vt_ref[0, c, h // A_GROUP]
            if online:
                m_old = m_refs[h][...]
                m_new = jnp.maximum(m_old, _col_max(st))
                alpha = jnp.exp2(m_old - m_new)
                p = jnp.exp2(st - m_new).astype(BF16)
                acc_refs[h][...] = alpha * acc_refs[h][...] + jnp.dot(vt, p, preferred_element_type=F32)
                m_refs[h][...] = m_new
            else:
                p = jnp.exp2(st - m_refs[h][...]).astype(BF16)
                acc_refs[h][...] += jnp.dot(vt, p, preferred_element_type=F32)
        return carry

    @pl.when(bounded)
    def _():
        work = [(c, h) for c in range(n_chunks) for h in range(A_HEADS)]
        scores = lambda c, h: jnp.dot(k_ref[0, c], qz_ref[0, h], preferred_element_type=F32)
        pending = [scores(*work[i]) for i in range(GATTN_AHEAD)]
        for i, (c, h) in enumerate(work):
            st = pending.pop(0)
            if i + GATTN_AHEAD < len(work):
                pending.append(scores(*work[i + GATTN_AHEAD]))
            p = jnp.exp2(st - bound[h:h + 1]).astype(BF16)
            acc_refs[h][...] += jnp.dot(vt_ref[0, c, h // A_GROUP], p, preferred_element_type=F32)

    @pl.when(jnp.logical_not(bounded))
    def _():
        for h in range(A_HEADS):
            m_refs[h][...] = jnp.full(m_refs[h].shape, -jnp.inf, F32)
        lax.fori_loop(0, n_chunks, functools.partial(chunk_body, online=True), 0)

    outs = []
    for h in range(A_HEADS):
        acc = acc_refs[h][...]
        outs.append(acc[0:HEAD_DIM] * (1.0 / acc[HEAD_DIM:HEAD_DIM + 1]))
    ot = jnp.concatenate(outs, axis=0)
    ms = jnp.sum(ot * ot, axis=0, keepdims=True) * (1.0 / A_WIDTH)
    o_ref[0] = (ot * lax.rsqrt(ms + LN_EPS) * ga_ref[0]).astype(BF16)


def _gattn(lidx, qz, ka, vat, qnorm, knorm, ga_col):
    nb, _, _, s = qz.shape
    nc, tk = ka.shape[1], ka.shape[2]
    tq = ATTN_Q_TILE
    grid_spec = pltpu.PrefetchScalarGridSpec(
        num_scalar_prefetch=1,
        grid=(nb, s // tq),
        in_specs=[
            pl.BlockSpec((1, A_HEADS, 2 * HEAD_DIM, tq), lambda b, i, l: (b, 0, 0, i)),
            pl.BlockSpec((1, nc, tk, KV_WIDTH), lambda b, i, l: (b, 0, 0, 0)),
            pl.BlockSpec((1, nc, A_KV_HEADS, VT_ROWS, tk), lambda b, i, l: (b, 0, 0, 0, 0)),
            pl.BlockSpec((1, A_HEADS, tq), lambda b, i, l: (b, 0, i)),
            pl.BlockSpec((1, nc, A_HEADS, tk), lambda b, i, l: (b, 0, 0, 0)),
            pl.BlockSpec((1, A_WIDTH, 1), lambda b, i, l: (l[0], 0, 0)),
        ],
        out_specs=pl.BlockSpec((1, A_WIDTH, tq), lambda b, i, l: (b, 0, i)),
        scratch_shapes=[pltpu.VMEM((1, tq), F32)] * A_HEADS + [pltpu.VMEM((VT_ROWS, tq), F32)] * A_HEADS,
    )
    return pl.pallas_call(
        _gattn_kernel, out_shape=jax.ShapeDtypeStruct((nb, A_WIDTH, s), BF16), grid_spec=grid_spec,
        compiler_params=_params(("parallel", "parallel")), name="gattn",
    )(lidx, qz, ka, vat, qnorm, knorm, ga_col)


def _nattn_kernel(nblk, q_ref, k_ref, v_ref, bias_ref, o_ref):
    first_head = lax.broadcasted_iota(jnp.int32, (QBLK, LANES), 1) < HEAD_DIM
    dn = (((1,), (1,)), ((), ()))
    j0 = pl.program_id(1) * NATTN_QBLKS
    fetched = _fetch_start_blk(j0, nblk)
    work = [(qb, h) for qb in range(NATTN_QBLKS) for h in range(B_HEADS)]
    offs = [_band_start_blk(j0 + qb, nblk) - fetched for qb in range(NATTN_QBLKS)]
    variants = [_bias_variant(j0 + qb, nblk) for qb in range(NATTN_QBLKS)]
    band_first = lax.broadcasted_iota(jnp.int32, (BAND_ROWS * GRID_W, LANES), 1) < HEAD_DIM
    own_lanes = (band_first, jnp.logical_not(band_first))

    def band(ref, qb, c):
        return jnp.concatenate([ref[0, offs[qb] + t, :, c * LANES:(c + 1) * LANES] for t in range(BAND_BLKS)], axis=0)

    def scores(qb, h):
        c, hh = h // 2, h % 2
        qp = q_ref[0, qb * QBLK:(qb + 1) * QBLK, c * LANES:(c + 1) * LANES]
        qm = jnp.where(first_head if hh == 0 else jnp.logical_not(first_head), qp, jnp.zeros_like(qp))
        return lax.dot_general(qm, band(k_ref, qb, c), dn, preferred_element_type=F32) + bias_ref[variants[qb], h]

    outs = []
    pending = [scores(*work[i]) for i in range(NATTN_AHEAD)]
    for i, (qb, h) in enumerate(work):
        s = pending.pop(0)
        if i + NATTN_AHEAD < len(work):
            pending.append(scores(*work[i + NATTN_AHEAD]))
        m = jnp.max(s, axis=-1, keepdims=True)
        p = jnp.exp2(s - m).astype(BF16)
        vp = band(v_ref, qb, h // 2)
        vp = jnp.where(own_lanes[h % 2], vp, jnp.ones_like(vp))
        o = jnp.dot(p, vp, preferred_element_type=F32)
        outs.append(o * (1.0 / pltpu.roll(o, HEAD_DIM, 1)))
    for qb in range(NATTN_QBLKS):
        o = outs[qb * B_HEADS:(qb + 1) * B_HEADS]
        pairs = [jnp.where(first_head, o[2 * c], o[2 * c + 1]) for c in range(B_HEADS // 2)]
        o_ref[0, qb * QBLK:(qb + 1) * QBLK, :] = jnp.concatenate(pairs, axis=-1)


def _band_start_blk(j, nblk):
    rows_per_blk = QBLK // GRID_W
    rows = nblk * rows_per_blk
    bs = jnp.clip(j * rows_per_blk - WIN_R // 2, 0, rows - BAND_ROWS)
    return bs // rows_per_blk


def _bias_variant(j, nblk):
    return jnp.where(j < 2, j, jnp.where(j >= nblk - 2, j - nblk + N_BIAS_VARIANTS, 2))


def _fetch_start_blk(j0, nblk):
    return jnp.minimum(_band_start_blk(j0, nblk), nblk - NATTN_FETCH_BLKS)


def _nattn(lidx, qn, kn, vn, bias):
    nb, s, _ = qn.shape
    nblk = s // QBLK
    assert nblk % NATTN_QBLKS == 0 and nblk >= NATTN_FETCH_BLKS, "sequence too short for the neighbourhood tiling"
    tq = NATTN_QBLKS * QBLK
    band_spec = pl.BlockSpec((pl.Element(1), pl.Element(NATTN_FETCH_BLKS), pl.Element(QBLK), pl.Element(B_WIDTH)),
                             lambda b, j, l: (b, _fetch_start_blk(j * NATTN_QBLKS, nblk), 0, 0))
    bias_spec = pl.BlockSpec((N_BIAS_VARIANTS, B_HEADS, QBLK, BAND_ROWS * GRID_W), lambda b, j, l: (l[0], 0, 0, 0))
    grid_spec = pltpu.PrefetchScalarGridSpec(
        num_scalar_prefetch=1,
        grid=(nb, nblk // NATTN_QBLKS),
        in_specs=[pl.BlockSpec((1, tq, B_WIDTH), lambda b, j, l: (b, j, 0)), band_spec, band_spec, bias_spec],
        out_specs=pl.BlockSpec((1, tq, B_WIDTH), lambda b, j, l: (b, j, 0)),
    )

    def kernel(lidx_ref, *refs):
        del lidx_ref
        _nattn_kernel(nblk, *refs)

    return pl.pallas_call(
        kernel, out_shape=jax.ShapeDtypeStruct((nb, s, B_WIDTH), F32), grid_spec=grid_spec,
        compiler_params=_params(("parallel", "arbitrary")), name="nattn",
    )(lidx, qn, kn, vn, bias)


def _nattn_bias_tables(rpb):
    n_layers = rpb.shape[0]
    n_r, n_c = 2 * WIN_R - 1, 2 * WIN_C - 1
    qc = np.arange(GRID_W)[:, None]
    kc = np.arange(GRID_W)[None, :]
    ic = np.clip(kc - qc + WIN_C - 1, 0, n_c - 1)
    cs = np.clip(qc - WIN_C // 2, 0, GRID_W - WIN_C)
    col_ok = (kc >= cs) & (kc < cs + WIN_C)
    onehot = jnp.asarray((ic.reshape(-1)[None, :] == np.arange(n_c)[:, None]).astype(np.float32))
    cols = jnp.dot(rpb.astype(F32).reshape(-1, n_c), onehot, precision=lax.Precision.HIGHEST)
    cols = cols.reshape(n_layers, B_HEADS, n_r, GRID_W, GRID_W) * LOG2_E
    rows_per_blk = QBLK // GRID_W
    rows = 16
    variants = []
    for r0 in (0, 2, 4, rows - 4, rows - 2):
        bs = min(max(r0 - WIN_R // 2, 0), rows - BAND_ROWS)
        q_rows = []
        for qo in range(rows_per_blk):
            qr = r0 + qo
            rs = min(max(qr - WIN_R // 2, 0), rows - WIN_R)
            k_rows = []
            for ko in range(BAND_ROWS):
                kr = bs + ko
                ir = min(max(kr - qr + WIN_R - 1, 0), n_r - 1)
                if rs <= kr < rs + WIN_R:
                    k_rows.append(jnp.where(jnp.asarray(col_ok)[None, None], cols[:, :, ir], NEG_INF))
                else:
                    k_rows.append(jnp.full((n_layers, B_HEADS, GRID_W, GRID_W), NEG_INF, F32))
            q_rows.append(jnp.concatenate(k_rows, axis=-1))
        variants.append(jnp.concatenate(q_rows, axis=-2))
    t = jnp.stack(variants, axis=1)
    return t.reshape(n_layers * N_BIAS_VARIANTS, B_HEADS, QBLK, BAND_ROWS * GRID_W)


def _group_member(x, d, wrapped):
    return jnp.where(wrapped, pltpu.roll(x, EXPERTS_PER_GROUP - d, 0), pltpu.roll(x, N_EXPERTS - d, 0))


def _route_t(logits_t):
    mx = jnp.max(logits_t, axis=0, keepdims=True)
    ex = jnp.exp(logits_t - mx)
    probs = ex / jnp.sum(ex, axis=0, keepdims=True)
    e_idx = lax.broadcasted_iota(jnp.int32, probs.shape, 0)
    pos = e_idx % EXPERTS_PER_GROUP
    rank = jnp.zeros(probs.shape, jnp.int32)
    for d in range(1, EXPERTS_PER_GROUP):
        wrapped = pos + d >= EXPERTS_PER_GROUP
        other = _group_member(probs, d, wrapped)
        beats = (other > probs) | ((other == probs) & wrapped)
        rank = rank + beats.astype(jnp.int32)
    top = rank < 2
    kept = jnp.where(top, probs, 0.0)
    score = kept
    for d in range(1, EXPERTS_PER_GROUP):
        score = score + _group_member(kept, d, pos + d >= EXPERTS_PER_GROUP)
    grp = e_idx // EXPERTS_PER_GROUP
    lost = jnp.zeros(probs.shape, jnp.bool_)
    for d in range(1, N_GROUPS):
        other = pltpu.roll(score, N_EXPERTS - d * EXPERTS_PER_GROUP, 0)
        earlier = grp + d >= N_GROUPS
        lost = lost | (other > score) | ((other == score) & earlier)
    won = jnp.logical_not(lost)
    gates = jnp.where(top & won, probs / score, 0.0)
    group_onehot = jnp.where(won & (pos == 0), 1.0, 0.0)
    return gates, group_onehot


def _post_attn_kernel(lidx_ref, oat_ref, ob_ref, x_ref, wo_ref, gb_ref, lng_ref, lnb_ref,
                      wrh_ref, wrl_ref, br_ref, h_ref, hb_ref, gates_ref, goh_ref, bounds_ref):
    del lidx_ref
    tm = x_ref.shape[0]
    n_part = tm // POST_ROWS
    parts = [pl.ds(r * POST_ROWS, POST_ROWS) for r in range(n_part)]
    dn = (((1,), (1,)), ((), ()))
    dn_t = (((0,), (0,)), ((), ()))

    def rms_b(rows):
        b = ob_ref[rows, :]
        return (b * lax.rsqrt(jnp.mean(b * b, axis=-1, keepdims=True) + LN_EPS) * gb_ref[0]).astype(BF16)

    nbs = [rms_b(rows) for rows in parts]
    mixes = [lax.dot_general(oat_ref[0, :, rows], wo_ref[0, 0:A_WIDTH, :], dn_t, preferred_element_type=F32)
             + jnp.dot(nb_, wo_ref[0, A_WIDTH:, :], preferred_element_type=F32) for rows, nb_ in zip(parts, nbs)]
    hs = [_layer_norm(ALPHA * x_ref[rows, :] + mix, lng_ref[0, 0:1], lnb_ref[0, 0:1]) for rows, mix in zip(parts, mixes)]
    for rows, h in zip(parts, hs):
        h_ref[rows, :] = h
        h_hi = h.astype(BF16)
        hb_ref[rows, :] = h_hi
        h_lo = (h - h_hi.astype(F32)).astype(BF16)
        logits_t = (lax.dot_general(wrh_ref[...], h_hi, dn, preferred_element_type=F32)
                    + lax.dot_general(wrh_ref[...], h_lo, dn, preferred_element_type=F32)
                    + lax.dot_general(wrl_ref[...], h_hi, dn, preferred_element_type=F32)) + br_ref[...]
        gates_t, group_onehot = _route_t(logits_t)
        goh_ref[:, rows] = group_onehot.astype(BF16)
        gates_ref[:, rows] = gates_t
    cnt = jnp.sum(goh_ref[...].astype(F32), axis=1, keepdims=True)
    slot_row = lax.broadcasted_iota(jnp.int32, (2 * N_GROUPS, LANES), 0)
    bounds = jnp.zeros((2 * N_GROUPS, LANES), F32)
    start = jnp.zeros((1, 1), F32)
    for g in range(N_GROUPS):
        end = start + cnt[g * EXPERTS_PER_GROUP:g * EXPERTS_PER_GROUP + 1]
        bounds = jnp.where(slot_row == g, start, jnp.where(slot_row == N_GROUPS + g, end, bounds))
        start = end
    bounds_ref[0] = bounds.astype(jnp.int32)


def _post_attn(lidx, oat, ob, x, wo, gb, lng, lnb, wrh, wrl, br):
    t = x.shape[0]
    tm = ROW_TILE
    tiles_per_seq = oat.shape[2] // tm
    row = lambda i, l: (i, 0)
    lay = lambda i, l: (l[0], 0, 0)
    full = lambda i, l: (0, 0)
    grid_spec = pltpu.PrefetchScalarGridSpec(
        num_scalar_prefetch=1,
        grid=(t // tm,),
        in_specs=[
            pl.BlockSpec((1, A_WIDTH, tm), lambda i, l: (i // tiles_per_seq, 0, i % tiles_per_seq)),
            pl.BlockSpec((tm, B_WIDTH), row),
            pl.BlockSpec((tm, D_MODEL), row),
            pl.BlockSpec((1, D_MODEL, D_MODEL), lay),
            pl.BlockSpec((1, 1, B_WIDTH), lay),
            pl.BlockSpec((1, 3, D_MODEL), lay),
            pl.BlockSpec((1, 3, D_MODEL), lay),
            pl.BlockSpec((N_EXPERTS, D_MODEL), full),
            pl.BlockSpec((N_EXPERTS, D_MODEL), full),
            pl.BlockSpec((N_EXPERTS, 1), full),
        ],
        out_specs=[
            pl.BlockSpec((tm, D_MODEL), row),
            pl.BlockSpec((tm, D_MODEL), row),
            pl.BlockSpec((N_EXPERTS, tm), lambda i, l: (0, i)),
            pl.BlockSpec((N_EXPERTS, tm), lambda i, l: (0, i)),
            pl.BlockSpec((1, 2 * N_GROUPS, LANES), lambda i, l: (i, 0, 0)),
        ],
    )
    return pl.pallas_call(
        _post_attn_kernel,
        out_shape=(jax.ShapeDtypeStruct((t, D_MODEL), F32),
                   jax.ShapeDtypeStruct((t, D_MODEL), BF16),
                   jax.ShapeDtypeStruct((N_EXPERTS, t), F32),
                   jax.ShapeDtypeStruct((N_EXPERTS, t), BF16),
                   jax.ShapeDtypeStruct((t // tm, 2 * N_GROUPS, LANES), jnp.int32)),
        grid_spec=grid_spec,
        compiler_params=_params(("parallel",)), name="post_attn",
    )(lidx, oat, ob, x, wo, gb, lng, lnb, wrh, wrl, br)


def _moe_kernel(lidx_ref, bounds_ref, hb_ref, gates_ref, goh_ref, upper_ref, wg_ref, wu_ref, wd_ref, o_ref,
                perm_ref, xs_ref, gs_ref, ys_ref):
    del lidx_ref
    i = pl.program_id(0)
    e = pl.program_id(1)
    tm = hb_ref.shape[0]
    base = i * (2 * N_GROUPS)

    @pl.when(e == 0)
    def _sort():
        goh = goh_ref[...]
        ranks = jnp.dot(goh, upper_ref[...], preferred_element_type=F32)
        row = lax.broadcasted_iota(jnp.int32, (N_EXPERTS, 1), 0)
        start = jnp.zeros((N_EXPERTS, 1), F32)
        for g in range(N_GROUPS):
            start = jnp.where(row == g * EXPERTS_PER_GROUP, bounds_ref[base + g].astype(F32), start)
        pos = jnp.sum(goh.astype(F32) * (ranks + start), axis=0, keepdims=True)
        slot = lax.broadcasted_iota(jnp.int32, (tm, tm), 0)
        perm = jnp.where(slot == pos.astype(jnp.int32), 1.0, 0.0).astype(BF16)
        perm_ref[...] = perm
        xs_ref[0:tm, :] = jnp.dot(perm, hb_ref[...], preferred_element_type=F32).astype(BF16)
        xs_ref[tm:, :] = jnp.zeros((MOE_PAD_ROWS, D_MODEL), BF16)
        gs_ref[tm:, :] = jnp.zeros((MOE_PAD_ROWS, LANES), F32)
        gates_t = gates_ref[...]
        g_hi = gates_t.astype(BF16)
        g_lo = (gates_t - g_hi.astype(F32)).astype(BF16)
        both = lax.dot_general(jnp.concatenate([g_hi, g_lo], axis=0), perm, (((1,), (1,)), ((), ())),
                               preferred_element_type=F32)
        sorted_t = both[0:N_EXPERTS] + both[N_EXPERTS:2 * N_EXPERTS]
        gs_ref[0:tm, :] = jnp.concatenate([sorted_t, jnp.zeros((LANES - N_EXPERTS, tm), F32)], axis=0).T
        ys_ref[...] = jnp.zeros(ys_ref.shape, F32)

    grp = (e * EXPERTS_PER_STEP) // EXPERTS_PER_GROUP
    first = bounds_ref[base + grp] // MOE_SUB
    stop = (bounds_ref[base + N_GROUPS + grp] + MOE_SUB - 1) // MOE_SUB

    def experts(row0, n_rows):
        rows = pl.ds(pl.multiple_of(row0, MOE_SUB), n_rows)
        lane = lax.broadcasted_iota(jnp.int32, (n_rows, LANES), 1)
        x = xs_ref[rows, :]
        gs = gs_ref[rows, :]
        total = None
        for j in range(EXPERTS_PER_STEP):
            gate = jnp.dot(x, wg_ref[0, 0, j], preferred_element_type=F32)
            up = jnp.dot(x, wu_ref[0, 0, j], preferred_element_type=F32)
            act = (gate * jax.nn.sigmoid(gate) * up).astype(BF16)
            y = jnp.dot(act, wd_ref[0, 0, j], preferred_element_type=F32)
            g_col = jnp.sum(jnp.where(lane == e * EXPERTS_PER_STEP + j, gs, 0.0), axis=1, keepdims=True)
            total = g_col * y if total is None else total + g_col * y
        ys_ref[rows, :] += total

    n_full = (stop - first) // MOE_MAX_UNITS
    rest = stop - first - MOE_MAX_UNITS * n_full
    tail = (first + MOE_MAX_UNITS * n_full) * MOE_SUB

    def full_body(s, carry):
        experts((first + MOE_MAX_UNITS * s) * MOE_SUB, MOE_MAX_UNITS * MOE_SUB)
        return carry

    lax.fori_loop(0, n_full, full_body, 0)
    for lo, units in MOE_TAIL_WINDOWS:
        @pl.when((rest > lo) & (rest <= units))
        def _(units=units):
            experts(tail, units * MOE_SUB)

    @pl.when(e == N_EXPERTS // EXPERTS_PER_STEP - 1)
    def _unsort():
        o_ref[...] = lax.dot_general(perm_ref[...], ys_ref[0:tm, :].astype(BF16), (((0,), (0,)), ((), ())),
                                     preferred_element_type=F32)


def _moe(lidx, hb, gates, goh, tile_bounds, wg, wu, wd):
    t = hb.shape[0]
    tm = MOE_TILE
    assert tm == ROW_TILE and tile_bounds.shape[0] == t // tm, "group bounds come per tile of the token-wise kernel"
    bounds = tile_bounds[:, :, 0].reshape(-1)
    n_steps = N_EXPERTS // EXPERTS_PER_STEP
    wg, wu, wd = (w.reshape(w.shape[0], n_steps, EXPERTS_PER_STEP, *w.shape[2:]) for w in (wg, wu, wd))
    upper = (jnp.arange(tm)[:, None] < jnp.arange(tm)[None, :]).astype(BF16)
    grid_spec = pltpu.PrefetchScalarGridSpec(
        num_scalar_prefetch=2,
        grid=(t // tm, n_steps),
        in_specs=[
            pl.BlockSpec((tm, D_MODEL), lambda i, e, l, b: (i, 0)),
            pl.BlockSpec((N_EXPERTS, tm), lambda i, e, l, b: (0, i)),
            pl.BlockSpec((N_EXPERTS, tm), lambda i, e, l, b: (0, i)),
            pl.BlockSpec((tm, tm), lambda i, e, l, b: (0, 0)),
            pl.BlockSpec((1, 1, EXPERTS_PER_STEP, D_MODEL, D_FF_EXPERT), lambda i, e, l, b: (l[0], e, 0, 0, 0)),
            pl.BlockSpec((1, 1, EXPERTS_PER_STEP, D_MODEL, D_FF_EXPERT), lambda i, e, l, b: (l[0], e, 0, 0, 0)),
            pl.BlockSpec((1, 1, EXPERTS_PER_STEP, D_FF_EXPERT, D_MODEL), lambda i, e, l, b: (l[0], e, 0, 0, 0)),
        ],
        out_specs=pl.BlockSpec((tm, D_MODEL), lambda i, e, l, b: (i, 0)),
        scratch_shapes=[pltpu.VMEM((tm, tm), BF16), pltpu.VMEM((tm + MOE_PAD_ROWS, D_MODEL), BF16),
                        pltpu.VMEM((tm + MOE_PAD_ROWS, LANES), F32), pltpu.VMEM((tm + MOE_PAD_ROWS, D_MODEL), F32)],
    )
    return pl.pallas_call(
        _moe_kernel, out_shape=jax.ShapeDtypeStruct((t, D_MODEL), F32), grid_spec=grid_spec,
        compiler_params=_params(("parallel", "arbitrary")), name="moe",
    )(lidx, bounds, hb, gates, goh, upper, wg, wu, wd)


def _post_moe_kernel(lidx_ref, h_ref, f_ref, p_ref, wpg_ref, bpg_ref, wpp_ref, lng_ref, lnb_ref, x_ref):
    del lidx_ref
    parts = [pl.ds(r * POST_ROWS, POST_ROWS) for r in range(h_ref.shape[0] // POST_ROWS)]
    h2s = [_layer_norm(ALPHA * h_ref[rows, :] + f_ref[rows, :], lng_ref[0, 1:2], lnb_ref[0, 1:2]) for rows in parts]
    zs = [jnp.dot(h2.astype(BF16), wpg_ref[0], preferred_element_type=F32) + bpg_ref[0] for h2 in h2s]
    projs = [jnp.dot(p_ref[0, rows, :].astype(BF16), wpp_ref[0], preferred_element_type=F32) for rows in parts]
    for rows, h2, z, proj in zip(parts, h2s, zs, projs):
        pe = jax.nn.sigmoid(z) * proj
        x_ref[rows, :] = _layer_norm(ALPHA * h2 + pe, lng_ref[0, 2:3], lnb_ref[0, 2:3])


def _post_moe(lidx, h, ffn, p, wpg, bpg, wpp, lng, lnb):
    t = h.shape[0]
    tm = ROW_TILE
    row = lambda i, l: (i, 0)
    lay = lambda i, l: (l[0], 0, 0)
    grid_spec = pltpu.PrefetchScalarGridSpec(
        num_scalar_prefetch=1,
        grid=(t // tm,),
        in_specs=[
            pl.BlockSpec((tm, D_MODEL), row),
            pl.BlockSpec((tm, D_MODEL), row),
            pl.BlockSpec((1, tm, PLE_DIM), lambda i, l: (l[0], i, 0)),
            pl.BlockSpec((1, D_MODEL, D_MODEL), lay),
            pl.BlockSpec((1, 1, D_MODEL), lay),
            pl.BlockSpec((1, PLE_DIM, D_MODEL), lay),
            pl.BlockSpec((1, 3, D_MODEL), lay),
            pl.BlockSpec((1, 3, D_MODEL), lay),
        ],
        out_specs=pl.BlockSpec((tm, D_MODEL), row),
    )
    return pl.pallas_call(
        _post_moe_kernel, out_shape=jax.ShapeDtypeStruct((t, D_MODEL), F32), grid_spec=grid_spec,
        compiler_params=_params(("parallel",)), name="post_moe",
    )(lidx, h, ffn, p, wpg, bpg, wpp, lng, lnb)


def _rope_tables_t(s):
    t = jnp.arange(s, dtype=jnp.int32)
    row = (t // GRID_W).astype(F32)
    col = (t % GRID_W).astype(F32)
    n_freq = HEAD_DIM // 4
    inv_freq = ROPE_THETA ** (-jnp.arange(n_freq, dtype=F32) / n_freq)
    ang = jnp.concatenate([inv_freq[:, None] * row[None, :], inv_freq[:, None] * col[None, :]], axis=0)
    return jnp.cos(ang), jnp.sin(ang)


def _prepare_weights(w_in, q_norm_a, k_norm_a, rpb_b, g_out_a, g_out_b, w_out, w_router, b_router,
                     w_gate, w_up, w_down, w_ple_gate, b_ple_gate, w_ple_proj):
    n_layers = w_in.shape[0]
    qa, ka, va, qn, kn, vn = jnp.split(w_in, [512, 640, 768, 1280, 1792], axis=-1)
    wt = jnp.swapaxes(jnp.concatenate([qa, ka, va], axis=-1), 1, 2).astype(BF16)
    wn = jnp.concatenate([qn, kn, vn], axis=-1).astype(BF16)
    wr_t = w_router.T
    wr_hi = wr_t.astype(BF16)
    wr_lo = (wr_t - wr_hi.astype(F32)).astype(BF16)
    return dict(
        wt=wt, wn=wn,
        qg=q_norm_a.reshape(n_layers, HEAD_DIM, 1), kg=k_norm_a.reshape(n_layers, HEAD_DIM, 1),
        bias=_nattn_bias_tables(rpb_b),
        ga=g_out_a.reshape(n_layers, A_WIDTH, 1), gb=g_out_b.reshape(n_layers, 1, B_WIDTH),
        wo=w_out.astype(BF16), wr_hi=wr_hi, wr_lo=wr_lo, br=b_router.reshape(N_EXPERTS, 1),
        wg=w_gate.astype(BF16), wu=w_up.astype(BF16), wd=w_down.astype(BF16),
        wpg=w_ple_gate.astype(BF16), bpg=b_ple_gate.reshape(n_layers, 1, D_MODEL), wpp=w_ple_proj.astype(BF16),
    )


def _layer(lidx, x, p_flat, w, ln_g, ln_b, rope):
    nb, s, _ = x.shape
    qz, ka, vat, qn, kn, vn, qnorm, knorm = _inproj(lidx, x, w["wt"], w["wn"], rope[0], rope[1], w["qg"], w["kg"])
    oat = _gattn(lidx, qz, ka, vat, qnorm, knorm, w["ga"])
    ob = _nattn(lidx, qn, kn, vn, w["bias"])
    t = nb * s
    h, hb, gates, goh, tile_bounds = _post_attn(lidx, oat, ob.reshape(t, B_WIDTH), x.reshape(t, D_MODEL),
                                   w["wo"], w["gb"], ln_g, ln_b, w["wr_hi"], w["wr_lo"], w["br"])
    ffn = _moe(lidx, hb, gates, goh, tile_bounds, w["wg"], w["wu"], w["wd"])
    xo = _post_moe(lidx, h, ffn, p_flat, w["wpg"], w["bpg"], w["wpp"], ln_g, ln_b)
    return xo.reshape(nb, s, D_MODEL)


def kernel(x_prompt, x_sample, p_prompt, p_sample, w_in, q_norm_a, k_norm_a, rpb_b, g_out_a, g_out_b, w_out, ln_g, ln_b, w_router, b_router, w_gate, w_up, w_down, w_ple_gate, b_ple_gate, w_ple_proj):
    n_layers = w_in.shape[0]
    w = _prepare_weights(w_in, q_norm_a, k_norm_a, rpb_b, g_out_a, g_out_b, w_out, w_router, b_router,
                         w_gate, w_up, w_down, w_ple_gate, b_ple_gate, w_ple_proj)
    trunks = []
    for x, p in ((x_prompt, p_prompt), (x_sample, p_sample)):
        nb, s, _ = x.shape
        trunks.append((x, p.reshape(n_layers, nb * s, PLE_DIM), _rope_tables_t(s)))

    outs = []
    for x, p_flat, rope in trunks:
        for layer in range(n_layers):
            x = _layer(jnp.full((1,), layer, jnp.int32), x, p_flat, w, ln_g, ln_b, rope)
        outs.append(x)
    return tuple(outs)
```
